```python
import jax, jax.numpy as jnp
from jax import lax
import numpy as np

D_MODEL = 1024
BATCH = 8
SEQ = 2048
DEPTH = 1
DEC_BATCH = 128
DEC_SEQ = 8
PAST_LEN = 8192
PAGE_SIZE = 128

HEAD_DIM = 64
ATT_HEADS = 8
ATT_W = ATT_HEADS * HEAD_DIM
RWKV_HEADS = 8
RWKV_W = RWKV_HEADS * HEAD_DIM
MIX_W = ATT_W + RWKV_W
DIL_BRANCHES = ((128, 1), (512, 4), (2048, 16))
MAX_WINDOW = 2048
ROT_DIM = HEAD_DIM // 4
ROPE_THETA = 500000.0
DECAY_LORA = 64
AAA_LORA = 64
GATE_LORA = 128
RWKV_IN = 3 * RWKV_W + DECAY_LORA + AAA_LORA + GATE_LORA
IN_W = 3 * ATT_W + RWKV_IN
D_FF = 4 * D_MODEL
NORM_EPS = 1e-6
LNX_EPS = 64e-5

kernel_name = 'hybrid_dilated_attn_rwkv7_step'


def rms_norm(x, g):
    xf = x.astype(jnp.float32)
    return xf * lax.rsqrt(jnp.mean(xf * xf, axis=-1, keepdims=True) + NORM_EPS) * g.astype(jnp.float32)


def ada_modulation(c, w_ada, b_ada):
    m = jax.nn.silu(c.astype(jnp.float32)) @ w_ada.astype(jnp.float32) + b_ada.astype(jnp.float32)
    return jnp.split(m[:, None, :], 6, axis=-1)


def partial_rope(t, pos):
    half = ROT_DIM // 2
    inv = ROPE_THETA ** (-jnp.arange(half, dtype=jnp.float32) * (2.0 / ROT_DIM))
    ang = pos.astype(jnp.float32)[:, None] * inv[None, :]
    cos = jnp.cos(ang)[None, :, None, :]
    sin = jnp.sin(ang)[None, :, None, :]
    t = t.astype(jnp.float32)
    t1, t2, rest = t[..., :half], t[..., half:ROT_DIM], t[..., ROT_DIM:]
    return jnp.concatenate([t1 * cos - t2 * sin, t1 * sin + t2 * cos, rest], axis=-1)


def dilated_branch_prompt(q, k, v, window, dil):
    B, S, H, E = q.shape
    nw = window // dil
    span = dil * nw
    Sp = -(-S // span) * span
    pad = ((0, 0), (0, Sp - S), (0, 0), (0, 0))
    n = Sp // dil
    nb = n // nw

    def to_blocks(t):
        t = jnp.pad(t, pad).reshape(B, n, dil, H, E).transpose(0, 2, 1, 3, 4)
        return t.reshape(B, dil, nb, nw, H, E)

    def with_prev(t):
        prev = jnp.pad(t, ((0, 0), (0, 0), (1, 0), (0, 0), (0, 0), (0, 0)))[:, :, :-1]
        return jnp.concatenate([prev, t], axis=3)

    qb = to_blocks(q)
    k2 = with_prev(to_blocks(k))
    v2 = with_prev(to_blocks(v))
    s = jnp.einsum('brnqhe,brnkhe->brnhqk', qb, k2.astype(jnp.float32)) * (HEAD_DIM ** -0.5)
    a_idx = jnp.arange(nw)[:, None]
    c_idx = jnp.arange(2 * nw)[None, :]
    dist = nw + a_idx - c_idx
    band = (dist >= 0) & (dist <= nw)
    has_prev = (jnp.arange(nb) > 0)[:, None, None] | (c_idx >= nw)[None]
    mask = (band[None] & has_prev)[None, None, :, None]
    s = jnp.where(mask, s, -jnp.inf)
    lse = jax.nn.logsumexp(s, axis=-1)
    p = jnp.exp(s - lse[..., None])
    o = jnp.einsum('brnhqk,brnkhe->brnqhe', p, v2.astype(jnp.float32))
    o = o.reshape(B, dil, n, H, E).transpose(0, 2, 1, 3, 4).reshape(B, Sp, H, E)[:, :S]
    lse = lse.transpose(0, 1, 2, 4, 3).reshape(B, dil, n, H).transpose(0, 2, 1, 3).reshape(B, Sp, H)[:, :S]
    return o, lse


def dilated_branch_sample(q, k, v, k_buf, v_buf, window, dil):
    T = q.shape[1]
    WB = k_buf.shape[1]
    nw = window // dil
    scale = HEAD_DIM ** -0.5
    i = jnp.arange(T)
    rel = i[:, None] - dil * jnp.arange(nw + 1)[None, :]
    buf_idx = WB + rel
    buf_ok = (rel < 0) & (buf_idx >= 0)
    buf_idx = jnp.clip(buf_idx, 0, WB - 1)
    kg = k_buf[:, buf_idx].astype(jnp.float32)
    vg = v_buf[:, buf_idx].astype(jnp.float32)
    s_buf = jnp.einsum('bthe,btjhe->bhtj', q, kg) * scale
    s_buf = jnp.where(buf_ok[None, None], s_buf, -jnp.inf)
    diff = i[:, None] - i[None, :]
    new_ok = (diff >= 0) & (diff % dil == 0) & (diff <= window)
    s_new = jnp.einsum('bthe,bshe->bhts', q, k.astype(jnp.float32)) * scale
    s_new = jnp.where(new_ok[None, None], s_new, -jnp.inf)
    s = jnp.concatenate([s_buf, s_new], axis=-1)
    lse = jax.nn.logsumexp(s, axis=-1)
    p = jnp.exp(s - lse[..., None])
    o = (jnp.einsum('bhtj,btjhe->bthe', p[..., :nw + 1], vg)
         + jnp.einsum('bhts,bshe->bthe', p[..., nw + 1:], v.astype(jnp.float32)))
    return o, lse.transpose(0, 2, 1)


def combine_dilations(branches):
    o = jnp.stack([b[0] for b in branches])
    lse = jnp.stack([b[1] for b in branches])
    wts = jax.nn.softmax(lse, axis=0)
    return jnp.sum(wts[..., None] * o, axis=0)


def rwkv7_inputs(P, P_prev, mu, w0, w2, a0, a2, g2, k_k, k_a):
    Pm = P + (P_prev - P) * mu.astype(jnp.float32)
    cuts = [RWKV_W, 2 * RWKV_W, 3 * RWKV_W, 3 * RWKV_W + DECAY_LORA, 3 * RWKV_W + DECAY_LORA + AAA_LORA]
    r, k, v, wl, al, gl = jnp.split(Pm, cuts, axis=-1)
    w = -jax.nn.softplus(-(w0 + jnp.tanh(wl) @ w2)) - 0.5
    decay = jnp.exp(-jnp.exp(w))
    a = jax.nn.sigmoid(a0 + al @ a2)
    g = jax.nn.sigmoid(gl) @ g2
    B, T = P.shape[:2]
    heads = lambda t: t.reshape(B, T, RWKV_HEADS, HEAD_DIM)
    kk = heads(k * k_k)
    kk = kk / jnp.maximum(jnp.sqrt(jnp.sum(kk * kk, axis=-1, keepdims=True)), 1e-12)
    k = k * (1.0 + (a - 1.0) * k_a)
    return heads(r), heads(decay), heads(k), heads(v), kk, heads(a), g


def wkv7_scan(S0, r, w, k, v, kk, a):
    def step(S, inp):
        r_t, w_t, k_t, v_t, kk_t, a_t = inp
        sa = jnp.einsum('bhij,bhj->bhi', S, -kk_t)
        S = (S * w_t[:, :, None, :] + sa[..., None] * (kk_t * a_t)[:, :, None, :]
             + v_t[..., None] * k_t[:, :, None, :])
        return S, jnp.einsum('bhij,bhj->bhi', S, r_t)
    xs = tuple(jnp.swapaxes(t, 0, 1) for t in (r, w, k, v, kk, a))
    S, out = lax.scan(step, S0.astype(jnp.float32), xs)
    return jnp.swapaxes(out, 0, 1), S


def rwkv7_output(o, r, k, v, g, r_k, lnx_g, lnx_b):
    B, T = o.shape[:2]
    mean = jnp.mean(o, axis=-1, keepdims=True)
    var = jnp.mean(jnp.square(o - mean), axis=-1, keepdims=True)
    on = ((o - mean) * lax.rsqrt(var + LNX_EPS)).reshape(B, T, RWKV_W) * lnx_g + lnx_b
    bonus = jnp.sum(r * k * r_k, axis=-1, keepdims=True) * v
    return (on + bonus.reshape(B, T, RWKV_W)) * g


def decoder_layer(x, c, pos, k_past, v_past, wkv0, shift0, norm1_g, norm2_g, w_ada, b_ada, w_in, w_out,
                  mu, w0, w2, a0, a2, g2, k_k, k_a, r_k, lnx_g, lnx_b, w_ff1, w_ff2):
    sh1, sc1, gt1, sh2, sc2, gt2 = ada_modulation(c, w_ada, b_ada)
    h = rms_norm(x, norm1_g) * (1.0 + sc1) + sh1
    proj = h @ w_in
    B, T = proj.shape[:2]
    q, k, v = jnp.split(proj[..., :3 * ATT_W], 3, axis=-1)
    q = partial_rope(q.reshape(B, T, ATT_HEADS, HEAD_DIM), pos)
    k = partial_rope(k.reshape(B, T, ATT_HEADS, HEAD_DIM), pos)
    v = v.reshape(B, T, ATT_HEADS, HEAD_DIM)
    if k_past is None:
        branches = [dilated_branch_prompt(q, k, v, win, dil) for win, dil in DIL_BRANCHES]
        keep = min(MAX_WINDOW, T)
        k_state, v_state = k[:, T - keep:], v[:, T - keep:]
    else:
        branches = [dilated_branch_sample(q, k, v, k_past, v_past, win, dil) for win, dil in DIL_BRANCHES]
        k_state, v_state = k, v
    att = combine_dilations(branches).reshape(B, T, ATT_W)
    P = proj[..., 3 * ATT_W:].astype(jnp.float32)
    P_prev = jnp.concatenate([shift0[:, None, :].astype(jnp.float32), P[:, :-1]], axis=1)
    r, decay, kr, vr, kk, a, g = rwkv7_inputs(P, P_prev, mu, w0, w2, a0, a2, g2, k_k, k_a)
    o, S = wkv7_scan(wkv0, r, decay, kr, vr, kk, a)
    rw = rwkv7_output(o, r, kr, vr, g, r_k, lnx_g, lnx_b)
    mix = jnp.concatenate([att, rw], axis=-1) @ w_out
    x = x + gt1 * mix
    h2 = rms_norm(x, norm2_g) * (1.0 + sc2) + sh2
    x = x + gt2 * (jnp.square(jax.nn.relu(h2 @ w_ff1)) @ w_ff2)
    return x, k_state, v_state, S, P[:, -1]


def setup_inputs(seed: int = 0) -> dict:
    key = jax.random.key(seed)
    ks = jax.random.split(key, 32)
    f32 = jnp.float32
    nrm = lambda kk, shape, s: jax.random.normal(kk, shape, f32) * s
    WB = min(MAX_WINDOW, PAST_LEN)
    return {
        'x_prompt': nrm(ks[0], (BATCH, SEQ, D_MODEL), 1.0),
        'x_sample': nrm(ks[1], (DEC_BATCH, DEC_SEQ, D_MODEL), 1.0),
        'cache_k': nrm(ks[2], (DEPTH, DEC_BATCH, WB, ATT_HEADS, HEAD_DIM), 1.0),
        'cache_v': nrm(ks[3], (DEPTH, DEC_BATCH, WB, ATT_HEADS, HEAD_DIM), 1.0),
        'state_wkv': nrm(ks[4], (DEPTH, DEC_BATCH, RWKV_HEADS, HEAD_DIM, HEAD_DIM), 0.3),
        'state_shift': nrm(ks[5], (DEPTH, DEC_BATCH, RWKV_IN), 1.0),
        'c_prompt': nrm(ks[6], (BATCH, D_MODEL), 1.0),
        'c_sample': nrm(ks[7], (DEC_BATCH, D_MODEL), 1.0),
        'norm1_g': 1.0 + nrm(ks[8], (DEPTH, D_MODEL), 0.05),
        'norm2_g': 1.0 + nrm(ks[9], (DEPTH, D_MODEL), 0.05),
        'w_ada': nrm(ks[10], (DEPTH, D_MODEL, 6 * D_MODEL), 0.5 * D_MODEL ** -0.5),
        'b_ada': nrm(ks[11], (DEPTH, 6 * D_MODEL), 0.02),
        'w_in': nrm(ks[12], (DEPTH, D_MODEL, IN_W), D_MODEL ** -0.5),
        'w_out': nrm(ks[13], (DEPTH, MIX_W, D_MODEL), MIX_W ** -0.5),
        'mu': jax.random.uniform(ks[14], (DEPTH, RWKV_IN), f32),
        'w0': jax.random.uniform(ks[15], (DEPTH, RWKV_W), f32, -4.0, 1.0),
        'w2': nrm(ks[16], (DEPTH, DECAY_LORA, RWKV_W), 0.5 * DECAY_LORA ** -0.5),
        'a0': nrm(ks[17], (DEPTH, RWKV_W), 0.5),
        'a2': nrm(ks[18], (DEPTH, AAA_LORA, RWKV_W), 0.5 * AAA_LORA ** -0.5),
        'g2': nrm(ks[19], (DEPTH, GATE_LORA, RWKV_W), GATE_LORA ** -0.5),
        'k_k': 0.85 + nrm(ks[20], (DEPTH, RWKV_W), 0.1),
        'k_a': 1.0 + nrm(ks[21], (DEPTH, RWKV_W), 0.1),
        'r_k': nrm(ks[22], (DEPTH, RWKV_HEADS, HEAD_DIM), 0.1),
        'lnx_g': 1.0 + nrm(ks[23], (DEPTH, RWKV_W), 0.05),
        'lnx_b': nrm(ks[24], (DEPTH, RWKV_W), 0.02),
        'w_ff1': nrm(ks[25], (DEPTH, D_MODEL, D_FF), D_MODEL ** -0.5),
        'w_ff2': nrm(ks[26], (DEPTH, D_FF, D_MODEL), D_FF ** -0.5),
        'normf_g': 1.0 + nrm(ks[27], (D_MODEL,), 0.05),
    }


def reference(x_prompt, x_sample, cache_k, cache_v, state_wkv, state_shift, c_prompt, c_sample,
              norm1_g, norm2_g, w_ada, b_ada, w_in, w_out, mu, w0, w2, a0, a2, g2, k_k, k_a, r_k,
              lnx_g, lnx_b, w_ff1, w_ff2, normf_g):
    B, S = x_prompt.shape[:2]
    DB, T = x_sample.shape[:2]
    pos_p = jnp.arange(S)
    pos_s = PAST_LEN + jnp.arange(T)
    wkv_zero = jnp.zeros((B, RWKV_HEADS, HEAD_DIM, HEAD_DIM), jnp.float32)
    shift_zero = jnp.zeros((B, RWKV_IN), jnp.float32)
    hp, hs = x_prompt, x_sample
    kp_l, vp_l, sp_l, shp_l, ks_l, vs_l, ss_l, shs_l = [], [], [], [], [], [], [], []
    for l in range(DEPTH):
        lw = (norm1_g[l], norm2_g[l], w_ada[l], b_ada[l], w_in[l], w_out[l], mu[l], w0[l], w2[l],
              a0[l], a2[l], g2[l], k_k[l], k_a[l], r_k[l], lnx_g[l], lnx_b[l], w_ff1[l], w_ff2[l])
        hp, kp, vp, sp, shp = decoder_layer(hp, c_prompt, pos_p, None, None, wkv_zero, shift_zero, *lw)
        hs, kn, vn, sn, shn = decoder_layer(hs, c_sample, pos_s, cache_k[l], cache_v[l],
                                            state_wkv[l], state_shift[l], *lw)
        kp_l.append(kp); vp_l.append(vp); sp_l.append(sp); shp_l.append(shp)
        ks_l.append(kn); vs_l.append(vn); ss_l.append(sn); shs_l.append(shn)
    dt = x_prompt.dtype
    y_prompt = rms_norm(hp, normf_g).astype(dt)
    y_sample = rms_norm(hs, normf_g).astype(x_sample.dtype)
    return (y_prompt, y_sample,
            jnp.stack(kp_l).astype(dt), jnp.stack(vp_l).astype(dt),
            jnp.stack(sp_l).astype(dt), jnp.stack(shp_l).astype(dt),
            jnp.stack(ks_l).astype(dt), jnp.stack(vs_l).astype(dt),
            jnp.stack(ss_l).astype(dt), jnp.stack(shs_l).astype(dt))
```

```python
import functools

import numpy as np
import jax
import jax.numpy as jnp
from jax import lax
from jax.experimental import pallas as pl
from jax.experimental.pallas import tpu as pltpu

F32 = jnp.float32
BF16 = jnp.bfloat16

HEAD_DIM = 64
ATT_HEADS = 8
ATT_W = ATT_HEADS * HEAD_DIM
RWKV_HEADS = 8
RWKV_W = RWKV_HEADS * HEAD_DIM
DIL_BRANCHES = ((128, 1), (512, 4), (2048, 16))
ROT_DIM = HEAD_DIM // 4
ROPE_THETA = 500000.0
DECAY_LORA = 64
AAA_LORA = 64
GATE_LORA = 128
RWKV_IN = 3 * RWKV_W + DECAY_LORA + AAA_LORA + GATE_LORA
NORM_EPS = 1e-6
LNX_EPS = 64e-5
PAST_LEN = 8192
ATT_BLOCK = 128
NEG_BIG = -1e30
VMEM_LIMIT = 56 * 1024 * 1024
ROW_TILE = 512
RWKV_CHUNK = 64


def _dot(a, b):
    return jnp.dot(a, b, preferred_element_type=F32)


def _dot_nt(a, b):
    return lax.dot_general(a, b, (((1,), (1,)), ((), ())), preferred_element_type=F32)


def _dot_tn(a, b):
    return lax.dot_general(a, b, (((0,), (0,)), ((), ())), preferred_element_type=F32)


def _split2(x):
    hi = x.astype(BF16)
    lo = (x - hi.astype(F32)).astype(BF16)
    return hi, lo


def _dot3(x, y):
    xh, xl = _split2(x)
    yh, yl = _split2(y)
    return _dot(xh, yh) + (_dot(xh, yl) + _dot(xl, yh))


def _dot_exact_rhs(x, e):
    xh, xl = _split2(x)
    return _dot(xh, e) + _dot(xl, e)


def _params(*sem):
    return pltpu.CompilerParams(dimension_semantics=sem, vmem_limit_bytes=VMEM_LIMIT)


def _ada_kernel(c_ref, w_ref, b_ref, o_ref):
    c = c_ref[...]
    s = c * jax.nn.sigmoid(c)
    o_ref[...] = _dot(s.astype(BF16), w_ref[...].astype(BF16)) + b_ref[...]


def _ada(c, w_ada, b_ada):
    n, d = c.shape
    cols = w_ada.shape[1]
    tn = 512
    return pl.pallas_call(
        _ada_kernel,
        grid=(cols // tn,),
        in_specs=[pl.BlockSpec((n, d), lambda j: (0, 0)),
                  pl.BlockSpec((d, tn), lambda j: (0, j)),
                  pl.BlockSpec((1, tn), lambda j: (0, j))],
        out_specs=pl.BlockSpec((n, tn), lambda j: (0, j)),
        out_shape=jax.ShapeDtypeStruct((n, cols), F32),
        compiler_params=_params("arbitrary"),
        name="ada",
    )(c, w_ada, b_ada.reshape(1, cols))


def _rope_store(t, cos, s1, s2, refs_scales):
    for c in range(ATT_W // 128):
        tc = t[:, 128 * c:128 * (c + 1)]
        r = tc * cos + pltpu.roll(tc, 128 - ROT_DIM // 2, 1) * s1 + pltpu.roll(tc, ROT_DIM // 2, 1) * s2
        for ref, scale in refs_scales:
            val = r if scale is None else r * scale
            ref[:, 128 * c:128 * (c + 1)] = val.astype(ref.dtype)


def _inproj_kernel(x_ref, sc_ref, sh_ref, g_ref, w_ref, cos_ref, s1_ref, s2_ref,
                   q_ref, k_ref, v_ref, kb_ref, vb_ref, p_ref):
    x = x_ref[...]
    ms = jnp.mean(x * x, axis=-1, keepdims=True)
    h = x * lax.rsqrt(ms + NORM_EPS) * g_ref[...]
    h = (h * (1.0 + sc_ref[...]) + sh_ref[...]).astype(BF16)
    cos, s1, s2 = cos_ref[...], s1_ref[...], s2_ref[...]
    q = _dot(h, w_ref[:, 0:ATT_W])
    _rope_store(q, cos, s1, s2, [(q_ref, HEAD_DIM ** -0.5)])
    k = _dot(h, w_ref[:, ATT_W:2 * ATT_W])
    _rope_store(k, cos, s1, s2, [(k_ref, None), (kb_ref, None)])
    v = _dot(h, w_ref[:, 2 * ATT_W:3 * ATT_W])
    v_ref[...] = v
    vb_ref[...] = v.astype(BF16)
    p_ref[...] = _dot(h, w_ref[:, 3 * ATT_W:])


def _inproj(x, sc, sh, g, w_in_b, tabs, tm, per_row_mod):
    G, R, D = x.shape
    nin = w_in_b.shape[1]
    row_spec = lambda w: pl.BlockSpec((None, tm, w), lambda b, i: (b, i, 0))
    if per_row_mod:
        mod_spec = pl.BlockSpec((None, tm, D), lambda b, i: (b, i, 0))
    else:
        mod_spec = pl.BlockSpec((None, 1, D), lambda b, i: (b, 0, 0))
    tab_spec = pl.BlockSpec((tm, 128), lambda b, i: (i, 0))
    const = lambda shape: pl.BlockSpec(shape, lambda b, i: (0, 0))
    return pl.pallas_call(
        _inproj_kernel,
        grid=(G, R // tm),
        in_specs=[row_spec(D), mod_spec, mod_spec, const((1, D)), const((D, nin)),
                  tab_spec, tab_spec, tab_spec],
        out_specs=[row_spec(ATT_W)] * 5 + [row_spec(RWKV_IN)],
        out_shape=[jax.ShapeDtypeStruct((G, R, ATT_W), BF16),
                   jax.ShapeDtypeStruct((G, R, ATT_W), F32),
                   jax.ShapeDtypeStruct((G, R, ATT_W), F32),
                   jax.ShapeDtypeStruct((G, R, ATT_W), BF16),
                   jax.ShapeDtypeStruct((G, R, ATT_W), BF16),
                   jax.ShapeDtypeStruct((G, R, RWKV_IN), F32)],
        compiler_params=_params("arbitrary", "arbitrary"),
        name="inproj",
    )(x, sc, sh, g.reshape(1, D), w_in_b, *tabs)


def _rope_tables(pos):
    half = ROT_DIM // 2
    inv = ROPE_THETA ** (-jnp.arange(half, dtype=F32) * (2.0 / ROT_DIM))
    ang = pos.astype(F32)[:, None] * inv[None, :]
    cos, sin = jnp.cos(ang), jnp.sin(ang)
    n = pos.shape[0]
    ones = jnp.ones((n, HEAD_DIM - ROT_DIM), F32)
    zeros = jnp.zeros((n, HEAD_DIM - ROT_DIM), F32)
    zh = jnp.zeros((n, half), F32)
    ctab = jnp.concatenate([cos, cos, ones], axis=1)
    s1 = jnp.concatenate([-sin, zh, zeros], axis=1)
    s2 = jnp.concatenate([zh, sin, zeros], axis=1)
    return tuple(jnp.tile(t, (1, 128 // HEAD_DIM)) for t in (ctab, s1, s2))


def _attn_kernel(*refs, use_prev, j_axis):
    if use_prev:
        q_ref, kp_ref, kc_ref, vp_ref, vc_ref, o_ref, lse_ref = refs
    else:
        q_ref, kc_ref, vc_ref, o_ref, lse_ref = refs
    nw = ATT_BLOCK
    a = lax.broadcasted_iota(jnp.int32, (nw, nw), 0)
    c = lax.broadcasted_iota(jnp.int32, (nw, nw), 1)
    cur_ok = a >= c
    if use_prev:
        prev_ok = (c - a) >= jnp.where(pl.program_id(j_axis) > 0, 0, nw)
    lane = lax.broadcasted_iota(jnp.int32, (nw, ATT_HEADS), 1)
    lse_tile = jnp.zeros((nw, ATT_HEADS), F32)
    for h in range(ATT_HEADS):
        sl = slice(HEAD_DIM * h, HEAD_DIM * (h + 1))
        qh = q_ref[:, sl]
        s_c = jnp.where(cur_ok, _dot_nt(qh, kc_ref[:, sl]), NEG_BIG)
        m = jnp.max(s_c, axis=-1, keepdims=True)
        if use_prev:
            s_p = jnp.where(prev_ok, _dot_nt(qh, kp_ref[:, sl]), NEG_BIG)
            m = jnp.maximum(m, jnp.max(s_p, axis=-1, keepdims=True))
        p_c = jnp.exp(s_c - m)
        l = jnp.sum(p_c, axis=-1, keepdims=True)
        acc = _dot(p_c.astype(BF16), vc_ref[:, sl])
        if use_prev:
            p_p = jnp.exp(s_p - m)
            l = l + jnp.sum(p_p, axis=-1, keepdims=True)
            acc = acc + _dot(p_p.astype(BF16), vp_ref[:, sl])
        o_ref[:, sl] = (acc / l).astype(o_ref.dtype)
        lse_tile = jnp.where(lane == h, m + jnp.log(l), lse_tile)
    lse_ref[...] = lse_tile


def _attn_branch(qb, kb, vb, dil):
    B, S, W = qb.shape
    nw = ATT_BLOCK
    n = S // dil
    nb = n // nw
    assert n % nw == 0
    view = lambda t: t.reshape(B, n, dil * W)
    blk = lambda imap: pl.BlockSpec((None, nw, W), imap)
    cur = lambda b, r, j: (b, j, r)
    prev = lambda b, r, j: (b, jnp.maximum(j - 1, 0), r)
    use_prev = nb > 1
    if use_prev:
        in_specs = [blk(cur), blk(prev), blk(cur), blk(prev), blk(cur)]
        args = (view(qb), view(kb), view(kb), view(vb), view(vb))
    else:
        in_specs = [blk(cur), blk(cur), blk(cur)]
        args = (view(qb), view(kb), view(vb))
    o, lse = pl.pallas_call(
        functools.partial(_attn_kernel, use_prev=use_prev, j_axis=2),
        grid=(B, dil, nb),
        in_specs=in_specs,
        out_specs=[blk(cur),
                   pl.BlockSpec((None, None, nw, ATT_HEADS), lambda b, r, j: (b, r, j, 0))],
        out_shape=[jax.ShapeDtypeStruct((B, n, dil * W), BF16),
                   jax.ShapeDtypeStruct((B, dil, n, ATT_HEADS), F32)],
        compiler_params=_params("arbitrary", "arbitrary", "arbitrary"),
        name=f"attn_d{dil}",
    )(*args)
    lse = jnp.swapaxes(lse, 1, 2).reshape(B, S, ATT_HEADS)
    return o.reshape(B, S, W), lse


def _combine_kernel(o1_ref, o2_ref, o3_ref, l1_ref, l2_ref, l3_ref, e_ref, att_ref):
    l1, l2, l3 = l1_ref[...], l2_ref[...], l3_ref[...]
    m = jnp.maximum(jnp.maximum(l1, l2), l3)
    e1, e2, e3 = jnp.exp(l1 - m), jnp.exp(l2 - m), jnp.exp(l3 - m)
    inv = 1.0 / (e1 + e2 + e3)
    e = e_ref[...]
    acc = _dot_exact_rhs(e1 * inv, e) * o1_ref[...].astype(F32)
    acc = acc + _dot_exact_rhs(e2 * inv, e) * o2_ref[...].astype(F32)
    acc = acc + _dot_exact_rhs(e3 * inv, e) * o3_ref[...].astype(F32)
    att_ref[...] = acc.astype(att_ref.dtype)


def _head_expand():
    e = np.zeros((ATT_HEADS, ATT_W), np.float32)
    for h in range(ATT_HEADS):
        e[h, h * HEAD_DIM:(h + 1) * HEAD_DIM] = 1.0
    return jnp.asarray(e, BF16)


def _combine(os_, lses, tm):
    B, S, W = os_[0].shape
    ospec = pl.BlockSpec((None, tm, W), lambda b, i: (b, i, 0))
    lspec = pl.BlockSpec((None, tm, ATT_HEADS), lambda b, i: (b, i, 0))
    return pl.pallas_call(
        _combine_kernel,
        grid=(B, S // tm),
        in_specs=[ospec] * 3 + [lspec] * 3 + [pl.BlockSpec((ATT_HEADS, W), lambda b, i: (0, 0))],
        out_specs=ospec,
        out_shape=jax.ShapeDtypeStruct((B, S, W), BF16),
        compiler_params=_params("arbitrary", "arbitrary"),
        name="combine",
    )(*os_, *lses, _head_expand())


def _sample_counts(T, WB):
    i = np.arange(T)
    c = np.arange(WB)
    delta = WB + i[None, :] - c[:, None]
    cnt_c = np.zeros((WB, T), np.float32)
    diff = i[None, :] - i[:, None]
    cnt_n = np.zeros((T, T), np.float32)
    for win, dil in DIL_BRANCHES:
        cnt_c += ((delta > 0) & (delta % dil == 0) & (delta <= win)).astype(np.float32)
        cnt_n += ((diff >= 0) & (diff % dil == 0) & (diff <= win)).astype(np.float32)
    return cnt_c, cnt_n


def _sample_attn_kernel(q_ref, kn_ref, vn_ref, ka_ref, kb_ref, va_ref, vb_ref,
                        ca_ref, cb_ref, cn_ref, mask_ref, o_ref):
    T = q_ref.shape[0]
    mask = mask_ref[...]
    qbd = jnp.concatenate([q_ref[...]] * ATT_HEADS, axis=0) * mask
    qbd_b = qbd.astype(BF16)

    def flat(ref):
        t = ref[...]
        return t.reshape(t.shape[0] * t.shape[1], t.shape[2]).astype(BF16)

    ka, kb, va, vb = flat(ka_ref), flat(kb_ref), flat(va_ref), flat(vb_ref)
    ca, cb, cn = ca_ref[...], cb_ref[...], cn_ref[...]
    s_a = jnp.where(ca > 0, _dot_nt(qbd_b, ka), NEG_BIG)
    s_b = jnp.where(cb > 0, _dot_nt(qbd_b, kb), NEG_BIG)
    s_n = jnp.where(cn > 0, _dot_nt(qbd, kn_ref[...]), NEG_BIG)
    m = jnp.maximum(jnp.maximum(jnp.max(s_a, axis=-1, keepdims=True), jnp.max(s_b, axis=-1, keepdims=True)),
                    jnp.max(s_n, axis=-1, keepdims=True))
    p_a = ca * jnp.exp(s_a - m)
    p_b = cb * jnp.exp(s_b - m)
    p_n = cn * jnp.exp(s_n - m)
    l = (jnp.sum(p_a, axis=-1, keepdims=True) + jnp.sum(p_b, axis=-1, keepdims=True)
         + jnp.sum(p_n, axis=-1, keepdims=True))
    acc = _dot(p_a.astype(BF16), va) + _dot(p_b.astype(BF16), vb) + _dot(p_n, vn_ref[...])
    out = (acc / l) * mask
    o = out[0:T]
    for h in range(1, ATT_HEADS):
        o = o + out[h * T:(h + 1) * T]
    o_ref[...] = o.astype(o_ref.dtype)


def _sample_attn(q, k_new, v_new, cache_k, cache_v):
    DB, T, W = q.shape
    WB = cache_k.shape[1]
    max_dil = max(d for _, d in DIL_BRANCHES)
    mid_win = sorted(w for w, _ in DIL_BRANCHES)[1]
    assert T <= max_dil // 2 and WB % max_dil == 0 and mid_win % max_dil == 0
    nm = WB // max_dil
    nmb = mid_win // max_dil
    half = max_dil // 2
    cnt_c, cnt_n = _sample_counts(T, WB)
    cc = cnt_c.reshape(nm, max_dil, T)
    assert not cc[:nm - nmb, half:].any(), "cache rows outside the two staged pieces must be unused"
    rep = lambda cnt: jnp.asarray(np.tile(cnt.T, (ATT_HEADS, 1)))
    ca = rep(cc[:, :half].reshape(nm * half, T))
    cb = rep(cc[nm - nmb:, half:].reshape(nmb * half, T))
    cn = rep(cnt_n)
    mask = np.zeros((ATT_HEADS * T, W), np.float32)
    for h in range(ATT_HEADS):
        mask[h * T:(h + 1) * T, h * HEAD_DIM:(h + 1) * HEAD_DIM] = 1.0
    ck = cache_k.reshape(DB, nm, max_dil, W)
    cv = cache_v.reshape(DB, nm, max_dil, W)
    new_spec = pl.BlockSpec((None, T, W), lambda b: (b, 0, 0))
    a_spec = pl.BlockSpec((None, nm, half, W), lambda b: (b, 0, 0, 0))
    b_spec = pl.BlockSpec((None, nmb, half, W), lambda b: (b, nm // nmb - 1, 1, 0))
    const = lambda arr: pl.BlockSpec(arr.shape, lambda b: (0, 0))
    return pl.pallas_call(
        _sample_attn_kernel,
        grid=(DB,),
        in_specs=[new_spec, new_spec, new_spec, a_spec, b_spec, a_spec, b_spec,
                  const(ca), const(cb), const(cn), const(mask)],
        out_specs=new_spec,
        out_shape=jax.ShapeDtypeStruct((DB, T, W), BF16),
        compiler_params=_params("arbitrary"),
        name="sample_attn",
    )(q, k_new, v_new, ck, ck, cv, cv, ca, cb, cn, jnp.asarray(mask))


def _rwkv_kernel(p_ref, shift_ref, s0_ref, mu_ref, w0_ref, w2_ref, a0_ref, a2_ref, g2_ref,
                 kk_ref, ka_ref, rk_ref, lg_ref, lb_ref, e_ref,
                 rw_ref, sout_ref, shout_ref, s_scr, prev_scr, *, n_double):
    C = p_ref.shape[0]
    first = pl.program_id(1) == 0

    @pl.when(first)
    def _():
        s_scr[...] = s0_ref[...]
        prev_scr[...] = shift_ref[...]

    P = p_ref[...]
    row = lax.broadcasted_iota(jnp.int32, (C, 1), 0)
    p_prev = jnp.where(row == 0, prev_scr[...], pltpu.roll(P, 1, 0))
    pm = P + (p_prev - P) * mu_ref[...]
    prev_scr[...] = P[C - 1:C, :]
    shout_ref[...] = P[C - 1:C, :]

    W = RWKV_W
    r, k, v = pm[:, 0:W], pm[:, W:2 * W], pm[:, 2 * W:3 * W]
    lora_wa = pm[:, 3 * W:3 * W + DECAY_LORA + AAA_LORA]
    gl = pm[:, 3 * W + DECAY_LORA + AAA_LORA:]
    wpre = w0_ref[...] + _dot(jnp.tanh(lora_wa).astype(BF16), w2_ref[...])
    neg = -wpre
    softplus = jnp.maximum(neg, 0.0) + jnp.log(1.0 + jnp.exp(-jnp.abs(neg)))
    logw = -jnp.exp(-softplus - 0.5)
    a = jax.nn.sigmoid(a0_ref[...] + _dot(lora_wa.astype(BF16), a2_ref[...]))
    g = _dot(jax.nn.sigmoid(gl).astype(BF16), g2_ref[...])
    e = e_ref[...]
    kkr = k * kk_ref[...]
    nrm = jnp.sqrt(_dot_exact_rhs(kkr * kkr, e))
    kk = kkr / jnp.maximum(nrm, 1e-12)
    kmod = k * (1.0 + (a - 1.0) * ka_ref[...])
    bonus = _dot_exact_rhs(r * kmod * rk_ref[...], e) * v

    ti = lax.broadcasted_iota(jnp.int32, (C, C), 0)
    si = lax.broadcasted_iota(jnp.int32, (C, C), 1)
    tri_incl = (ti >= si)
    tri_strict = (ti > si)
    tri_b = jnp.where(tri_incl, 1.0, 0.0).astype(BF16)
    lw_h = logw.astype(BF16)
    lw_r = logw - lw_h.astype(F32)
    lw_m = lw_r.astype(BF16)
    lw_l = (lw_r - lw_m.astype(F32)).astype(BF16)
    L = _dot(tri_b, lw_h) + (_dot(tri_b, lw_m) + _dot(tri_b, lw_l))
    e_l = jnp.exp(L)
    e_nl = jnp.exp(-L)
    a_t = -kk * jnp.exp(L - logw)
    r_t = r * e_l
    b_t = kk * a * e_nl
    k_t = kmod * e_nl
    w_c = e_l[C - 1:C, :]

    eye = jnp.where(ti == si, 1.0, 0.0)
    o_heads = []
    for h in range(RWKV_HEADS):
        sl = slice(HEAD_DIM * h, HEAD_DIM * (h + 1))
        s0 = s_scr[h]
        ar = jnp.concatenate([a_t[:, sl], r_t[:, sl]], axis=0).astype(BF16)
        bk = jnp.concatenate([b_t[:, sl], k_t[:, sl]], axis=0).astype(BF16)
        vh = v[:, sl]
        gm = _dot_nt(ar, bk)
        m_ab = jnp.where(tri_strict, gm[0:C, 0:C], 0.0)
        m_ak = jnp.where(tri_strict, gm[0:C, C:2 * C], 0.0)
        n_r = jnp.concatenate([jnp.where(tri_incl, gm[C:2 * C, 0:C], 0.0),
                               jnp.where(tri_incl, gm[C:2 * C, C:2 * C], 0.0)], axis=1)
        ars = _dot_nt(ar, s0.astype(BF16))
        x = ars[0:C] + _dot(m_ak.astype(BF16), vh.astype(BF16))
        t_inv = eye + m_ab
        mp = m_ab
        for _ in range(n_double):
            mp = _dot3(mp, mp)
            t_inv = t_inv + _dot3(t_inv, mp)
        u = _dot3(t_inv, x)
        uv = jnp.concatenate([u, vh], axis=0).astype(BF16)
        o_heads.append(ars[C:2 * C] + _dot(n_r.astype(BF16), uv))
        s_new = (s0 + _dot_tn(uv, bk)) * w_c[:, sl]
        s_scr[h] = s_new
        sout_ref[h] = s_new

    o = jnp.concatenate(o_heads, axis=1)
    mean = _dot_exact_rhs(o, e) * (1.0 / HEAD_DIM)
    d = o - mean
    var = _dot_exact_rhs(d * d, e) * (1.0 / HEAD_DIM)
    on = d * lax.rsqrt(var + LNX_EPS) * lg_ref[...] + lb_ref[...]
    rw_ref[...] = ((on + bonus) * g).astype(rw_ref.dtype)


def _rwkv(P, shift0, s0, lw, C):
    B, T, _ = P.shape
    assert T % C == 0 and C & (C - 1) == 0
    n_double = max(int(np.log2(C)) - 1, 0)
    W = RWKV_W
    e = np.zeros((W, W), np.float32)
    for h in range(RWKV_HEADS):
        e[h * HEAD_DIM:(h + 1) * HEAD_DIM, h * HEAD_DIM:(h + 1) * HEAD_DIM] = 1.0
    zpad = jnp.zeros((DECAY_LORA, W), F32)
    w2p = jnp.concatenate([lw["w2"], jnp.zeros((AAA_LORA, W), F32)], axis=0).astype(BF16)
    a2p = jnp.concatenate([zpad, lw["a2"]], axis=0).astype(BF16)
    row = lambda t: t.reshape(1, -1)
    consts = [row(lw["mu"]), row(lw["w0"]), w2p, row(lw["a0"]), a2p, lw["g2"].astype(BF16),
              row(lw["k_k"]), row(lw["k_a"]), row(lw["r_k"]), row(lw["lnx_g"]), row(lw["lnx_b"]),
              jnp.asarray(e, BF16)]
    const = lambda arr: pl.BlockSpec(arr.shape, lambda b, c: (0, 0))
    st_spec = pl.BlockSpec((None, RWKV_HEADS, HEAD_DIM, HEAD_DIM), lambda b, c: (b, 0, 0, 0))
    sh_spec = pl.BlockSpec((None, 1, RWKV_IN), lambda b, c: (b, 0, 0))
    rw, s_out, sh_out = pl.pallas_call(
        functools.partial(_rwkv_kernel, n_double=n_double),
        grid=(B, T // C),
        in_specs=[pl.BlockSpec((None, C, RWKV_IN), lambda b, c: (b, c, 0)), sh_spec, st_spec]
                 + [const(t) for t in consts],
        out_specs=[pl.BlockSpec((None, C, W), lambda b, c: (b, c, 0)), st_spec, sh_spec],
        out_shape=[jax.ShapeDtypeStruct((B, T, W), BF16),
                   jax.ShapeDtypeStruct((B, RWKV_HEADS, HEAD_DIM, HEAD_DIM), F32),
                   jax.ShapeDtypeStruct((B, 1, RWKV_IN), F32)],
        scratch_shapes=[pltpu.VMEM((RWKV_HEADS, HEAD_DIM, HEAD_DIM), F32),
                        pltpu.VMEM((1, RWKV_IN), F32)],
        compiler_params=_params("arbitrary", "arbitrary"),
        name=f"rwkv_c{C}",
    )(P, shift0.reshape(B, 1, RWKV_IN), s0, *consts)
    return rw, s_out, sh_out.reshape(B, RWKV_IN)


def _mlp_kernel(att_ref, rw_ref, x_ref, gt1_ref, sc2_ref, sh2_ref, gt2_ref, g2_ref, gf_ref,
                wo_ref, w1_ref, w2_ref, y_ref, *, final_norm, ff_chunk):
    half = wo_ref.shape[0] // 2
    mix = _dot(att_ref[...], wo_ref[0:half, :]) + _dot(rw_ref[...], wo_ref[half:, :])
    x1 = x_ref[...] + gt1_ref[...] * mix
    ms = jnp.mean(x1 * x1, axis=-1, keepdims=True)
    h2 = x1 * lax.rsqrt(ms + NORM_EPS) * g2_ref[...]
    h2 = (h2 * (1.0 + sc2_ref[...]) + sh2_ref[...]).astype(BF16)
    acc = jnp.zeros_like(x1)
    for c in range(w1_ref.shape[1] // ff_chunk):
        hid = _dot(h2, w1_ref[:, c * ff_chunk:(c + 1) * ff_chunk])
        hid = jnp.square(jnp.maximum(hid, 0.0)).astype(BF16)
        acc = acc + _dot(hid, w2_ref[c * ff_chunk:(c + 1) * ff_chunk, :])
    x2 = x1 + gt2_ref[...] * acc
    if final_norm:
        ms2 = jnp.mean(x2 * x2, axis=-1, keepdims=True)
        x2 = x2 * lax.rsqrt(ms2 + NORM_EPS) * gf_ref[...]
    y_ref[...] = x2


def _mlp(att, rw, x, gt1, sc2, sh2, gt2, g2, gf, wo_b, w1_b, w2_b, tm, per_row_mod, final_norm):
    G, R, D = x.shape
    dff = w1_b.shape[1]
    row_spec = lambda w: pl.BlockSpec((None, tm, w), lambda b, i: (b, i, 0))
    if per_row_mod:
        mod_spec = pl.BlockSpec((None, tm, D), lambda b, i: (b, i, 0))
    else:
        mod_spec = pl.BlockSpec((None, 1, D), lambda b, i: (b, 0, 0))
    const = lambda shape: pl.BlockSpec(shape, lambda b, i: (0, 0), pipeline_mode=pl.Buffered(1))
    return pl.pallas_call(
        functools.partial(_mlp_kernel, final_norm=final_norm, ff_chunk=1024),
        grid=(G, R // tm),
        in_specs=[row_spec(att.shape[-1]), row_spec(rw.shape[-1]), row_spec(D),
                  mod_spec, mod_spec, mod_spec, mod_spec, const((1, D)), const((1, D)),
                  const(wo_b.shape), const((D, dff)), const((dff, D))],
        out_specs=row_spec(D),
        out_shape=jax.ShapeDtypeStruct((G, R, D), F32),
        compiler_params=_params("arbitrary", "arbitrary"),
        name="mlp",
    )(att, rw, x, gt1, sc2, sh2, gt2, g2.reshape(1, D), gf.reshape(1, D), wo_b, w1_b, w2_b)


def kernel(x_prompt, x_sample, cache_k, cache_v, state_wkv, state_shift, c_prompt, c_sample, norm1_g, norm2_g, w_ada, b_ada, w_in, w_out, mu, w0, w2, a0, a2, g2, k_k, k_a, r_k, lnx_g, lnx_b, w_ff1, w_ff2, normf_g):
    B, S, D = x_prompt.shape
    DB, T, _ = x_sample.shape
    depth = w_in.shape[0]
    dt = x_prompt.dtype
    tabs_p = _rope_tables(jnp.arange(S))
    tabs_s = tuple(jnp.tile(t, (DB, 1)) for t in _rope_tables(PAST_LEN + jnp.arange(T)))
    c_all = jnp.concatenate([c_prompt, c_sample], axis=0)
    hp = x_prompt
    hs = x_sample.reshape(1, DB * T, D)
    outs = [[] for _ in range(8)]
    tm_s = min(ROW_TILE, DB * T)
    for l in range(depth):
        last = l == depth - 1
        mod = _ada(c_all, w_ada[l], b_ada[l])
        mod_p = [m.reshape(B, 1, D) for m in jnp.split(mod[:B], 6, axis=-1)]
        mod_s = [jnp.repeat(m, T, axis=0).reshape(1, DB * T, D) for m in jnp.split(mod[B:], 6, axis=-1)]
        w_in_b, wo_b = w_in[l].astype(BF16), w_out[l].astype(BF16)
        w1_b, w2_b = w_ff1[l].astype(BF16), w_ff2[l].astype(BF16)
        lw = dict(mu=mu[l], w0=w0[l], w2=w2[l], a0=a0[l], a2=a2[l], g2=g2[l], k_k=k_k[l], k_a=k_a[l],
                  r_k=r_k[l], lnx_g=lnx_g[l], lnx_b=lnx_b[l])

        sh1, sc1, gt1, sh2, sc2, gt2 = mod_p
        q, k, v, kb, vb, P = _inproj(hp, sc1, sh1, norm1_g[l], w_in_b, tabs_p, ROW_TILE, False)
        branches = [_attn_branch(q, kb, vb, dil) for _, dil in DIL_BRANCHES]
        att = _combine([b[0] for b in branches], [b[1] for b in branches], ROW_TILE)
        rw, s_p, shift_p = _rwkv(P, jnp.zeros((B, RWKV_IN), F32),
                                 jnp.zeros((B, RWKV_HEADS, HEAD_DIM, HEAD_DIM), F32), lw, RWKV_CHUNK)
        hp = _mlp(att, rw, hp, gt1, sc2, sh2, gt2, norm2_g[l], normf_g, wo_b, w1_b, w2_b, ROW_TILE, False, last)
        keep = min(max(w for w, _ in DIL_BRANCHES), S)
        outs[0].append(k[:, S - keep:].reshape(B, keep, ATT_HEADS, HEAD_DIM))
        outs[1].append(v[:, S - keep:].reshape(B, keep, ATT_HEADS, HEAD_DIM))
        outs[2].append(s_p)
        outs[3].append(shift_p)

        sh1, sc1, gt1, sh2, sc2, gt2 = mod_s
        q, k, v, _, _, P = _inproj(hs, sc1, sh1, norm1_g[l], w_in_b, tabs_s, tm_s, True)
        WB = cache_k.shape[2]
        att = _sample_attn(q.reshape(DB, T, ATT_W).astype(F32), k.reshape(DB, T, ATT_W), v.reshape(DB, T, ATT_W),
                           cache_k[l].reshape(DB, WB, ATT_W), cache_v[l].reshape(DB, WB, ATT_W))
        rw, s_s, shift_s = _rwkv(P.reshape(DB, T, RWKV_IN), state_shift[l], state_wkv[l], lw, T)
        hs = _mlp(att.reshape(1, DB * T, ATT_W), rw.reshape(1, DB * T, RWKV_W), hs, gt1, sc2, sh2, gt2,
                  norm2_g[l], normf_g, wo_b, w1_b, w2_b, tm_s, True, last)
        outs[4].append(k.reshape(DB, T, ATT_HEADS, HEAD_DIM))
        outs[5].append(v.reshape(DB, T, ATT_HEADS, HEAD_DIM))
        outs[6].append(s_s)
        outs[7].append(shift_s)
    stack = lambda xs: jnp.stack(xs).astype(dt)
    return (hp.astype(dt), hs.reshape(DB, T, D).astype(x_sample.dtype), *[stack(o) for o in outs])
```

```python
import functools

import numpy as np
import jax
import jax.numpy as jnp
from jax import lax
from jax.experimental import pallas as pl
from jax.experimental.pallas import tpu as pltpu

F32 = jnp.float32
BF16 = jnp.bfloat16

HEAD_DIM = 64
ATT_HEADS = 8
ATT_W = ATT_HEADS * HEAD_DIM
RWKV_HEADS = 8
RWKV_W = RWKV_HEADS * HEAD_DIM
DIL_BRANCHES = ((128, 1), (512, 4), (2048, 16))
ROT_DIM = HEAD_DIM // 4
ROPE_THETA = 500000.0
DECAY_LORA = 64
AAA_LORA = 64
GATE_LORA = 128
RWKV_IN = 3 * RWKV_W + DECAY_LORA + AAA_LORA + GATE_LORA
NORM_EPS = 1e-6
LNX_EPS = 64e-5
PAST_LEN = 8192
ATT_BLOCK = 128
NEG_BIG = -1e30
VMEM_LIMIT = 56 * 1024 * 1024
ROW_TILE = 512
RWKV_CHUNK = 64


def _dot(a, b):
    return jnp.dot(a, b, preferred_element_type=F32)


def _dot_nt(a, b):
    return lax.dot_general(a, b, (((1,), (1,)), ((), ())), preferred_element_type=F32)


def _dot_tn(a, b):
    return lax.dot_general(a, b, (((0,), (0,)), ((), ())), preferred_element_type=F32)


def _split2(x):
    hi = x.astype(BF16)
    lo = (x - hi.astype(F32)).astype(BF16)
    return hi, lo


def _dot3(x, y):
    xh, xl = _split2(x)
    yh, yl = _split2(y)
    return _dot(xh, yh) + (_dot(xh, yl) + _dot(xl, yh))


def _dot_exact_rhs(x, e):
    xh, xl = _split2(x)
    return _dot(xh, e) + _dot(xl, e)


def _params(*sem):
    return pltpu.CompilerParams(dimension_semantics=sem, vmem_limit_bytes=VMEM_LIMIT)


def _ada_kernel(c_ref, w_ref, b_ref, o_ref):
    c = c_ref[...]
    s = c * jax.nn.sigmoid(c)
    o_ref[...] = _dot(s.astype(BF16), w_ref[...].astype(BF16)) + b_ref[...]


def _ada(c, w_ada, b_ada):
    n, d = c.shape
    cols = w_ada.shape[1]
    tn = 512
    return pl.pallas_call(
        _ada_kernel,
        grid=(cols // tn,),
        in_specs=[pl.BlockSpec((n, d), lambda j: (0, 0)),
                  pl.BlockSpec((d, tn), lambda j: (0, j)),
                  pl.BlockSpec((1, tn), lambda j: (0, j))],
        out_specs=pl.BlockSpec((n, tn), lambda j: (0, j)),
        out_shape=jax.ShapeDtypeStruct((n, cols), F32),
        compiler_params=_params("arbitrary"),
        name="ada",
    )(c, w_ada, b_ada.reshape(1, cols))


def _rope(t, cos, s1, s2):
    parts = []
    for c in range(ATT_W // 128):
        tc = t[:, 128 * c:128 * (c + 1)]
        parts.append(tc * cos + pltpu.roll(tc, 128 - ROT_DIM // 2, 1) * s1 + pltpu.roll(tc, ROT_DIM // 2, 1) * s2)
    return jnp.concatenate(parts, axis=1)


def _inproj_kernel(x_ref, sc_ref, sh_ref, g_ref, w_ref, cos_ref, s1_ref, s2_ref, *out_refs,
                   nat_f32, transposed):
    q_ref, kb_ref, vb_ref, p_ref = out_refs[:4]
    rest = list(out_refs[4:])
    x = x_ref[...]
    ms = jnp.mean(x * x, axis=-1, keepdims=True)
    h = x * lax.rsqrt(ms + NORM_EPS) * g_ref[...]
    h = (h * (1.0 + sc_ref[...]) + sh_ref[...]).astype(BF16)
    cos, s1, s2 = cos_ref[...], s1_ref[...], s2_ref[...]
    q = _rope(_dot(h, w_ref[:, 0:ATT_W]), cos, s1, s2)
    q_ref[...] = (q * HEAD_DIM ** -0.5).astype(q_ref.dtype)
    k = _rope(_dot(h, w_ref[:, ATT_W:2 * ATT_W]), cos, s1, s2)
    v = _dot(h, w_ref[:, 2 * ATT_W:3 * ATT_W])
    kb_ref[...] = k.astype(BF16)
    vb_ref[...] = v.astype(BF16)
    p_ref[...] = _dot(h, w_ref[:, 3 * ATT_W:])
    if nat_f32:
        rest.pop(0)[...] = k
        rest.pop(0)[...] = v
    if transposed:
        rest.pop(0)[...] = k.T
        rest.pop(0)[...] = v.T


def _inproj(x, sc, sh, g, w_in_b, tabs, tm, mod_mode, nat_f32, transposed, nq=1):
    G, R, D = x.shape
    nin = w_in_b.shape[1]
    row_spec = lambda w: pl.BlockSpec((None, tm, w), lambda b, i: (b, i, 0))
    mod_spec = {"batch": pl.BlockSpec((None, 1, D), lambda b, i: (b, 0, 0)),
                "row": pl.BlockSpec((None, tm, D), lambda b, i: (b, i, 0)),
                "tile": pl.BlockSpec((None, tm, D), lambda b, i: (b, 0, 0))}[mod_mode]
    tab_spec = pl.BlockSpec((tm, 128), lambda b, i: (i, 0))
    const = lambda shape: pl.BlockSpec(shape, lambda b, i: (0, 0))
    out_specs = [row_spec(ATT_W)] * 3 + [row_spec(RWKV_IN)]
    out_shape = [jax.ShapeDtypeStruct((G, R, ATT_W), BF16)] * 3 + [jax.ShapeDtypeStruct((G, R, RWKV_IN), F32)]
    if nat_f32:
        out_specs += [row_spec(ATT_W)] * 2
        out_shape += [jax.ShapeDtypeStruct((G, R, ATT_W), F32)] * 2
    if transposed:
        spq = (R // nq) // tm
        t_spec = pl.BlockSpec((None, None, ATT_W, tm), lambda b, i: (b, i // spq, 0, i % spq))
        out_specs += [t_spec] * 2
        out_shape += [jax.ShapeDtypeStruct((G, nq, ATT_W, R // nq), F32)] * 2
    return pl.pallas_call(
        functools.partial(_inproj_kernel, nat_f32=nat_f32, transposed=transposed),
        grid=(G, R // tm),
        in_specs=[row_spec(D), mod_spec, mod_spec, const((1, D)), const((D, nin)),
                  tab_spec, tab_spec, tab_spec],
        out_specs=out_specs,
        out_shape=out_shape,
        compiler_params=_params("arbitrary", "arbitrary"),
        name="inproj",
    )(x, sc, sh, g.reshape(1, D), w_in_b, *tabs)


def _rope_tables(pos):
    half = ROT_DIM // 2
    inv = ROPE_THETA ** (-jnp.arange(half, dtype=F32) * (2.0 / ROT_DIM))
    ang = pos.astype(F32)[:, None] * inv[None, :]
    cos, sin = jnp.cos(ang), jnp.sin(ang)
    n = pos.shape[0]
    ones = jnp.ones((n, HEAD_DIM - ROT_DIM), F32)
    zeros = jnp.zeros((n, HEAD_DIM - ROT_DIM), F32)
    zh = jnp.zeros((n, half), F32)
    ctab = jnp.concatenate([cos, cos, ones], axis=1)
    s1 = jnp.concatenate([-sin, zh, zeros], axis=1)
    s2 = jnp.concatenate([zh, sin, zeros], axis=1)
    return tuple(jnp.tile(t, (1, 128 // HEAD_DIM)) for t in (ctab, s1, s2))


def _attn_kernel(*refs, use_prev, j_axis):
    if use_prev:
        q_ref, kp_ref, kc_ref, vp_ref, vc_ref, o_ref, lse_ref = refs
    else:
        q_ref, kc_ref, vc_ref, o_ref, lse_ref = refs
    nw = ATT_BLOCK
    a = lax.broadcasted_iota(jnp.int32, (nw, nw), 0)
    c = lax.broadcasted_iota(jnp.int32, (nw, nw), 1)
    cur_ok = a >= c
    if use_prev:
        prev_ok = (c - a) >= jnp.where(pl.program_id(j_axis) > 0, 0, nw)
    lane = lax.broadcasted_iota(jnp.int32, (nw, ATT_HEADS), 1)
    lse_tile = jnp.zeros((nw, ATT_HEADS), F32)
    for h in range(ATT_HEADS):
        sl = slice(HEAD_DIM * h, HEAD_DIM * (h + 1))
        qh = q_ref[:, sl]
        s_c = jnp.where(cur_ok, _dot_nt(qh, kc_ref[:, sl]), NEG_BIG)
        m = jnp.max(s_c, axis=-1, keepdims=True)
        if use_prev:
            s_p = jnp.where(prev_ok, _dot_nt(qh, kp_ref[:, sl]), NEG_BIG)
            m = jnp.maximum(m, jnp.max(s_p, axis=-1, keepdims=True))
        p_c = jnp.exp(s_c - m)
        l = jnp.sum(p_c, axis=-1, keepdims=True)
        acc = _dot(p_c.astype(BF16), vc_ref[:, sl])
        if use_prev:
            p_p = jnp.exp(s_p - m)
            l = l + jnp.sum(p_p, axis=-1, keepdims=True)
            acc = acc + _dot(p_p.astype(BF16), vp_ref[:, sl])
        o_ref[:, sl] = (acc / l).astype(o_ref.dtype)
        lse_tile = jnp.where(lane == h, m + jnp.log(l), lse_tile)
    lse_ref[...] = lse_tile


def _attn_branch(qb, kb, vb, dil):
    B, S, W = qb.shape
    nw = ATT_BLOCK
    n = S // dil
    nb = n // nw
    assert n % nw == 0
    view = lambda t: t.reshape(B, n, dil * W)
    blk = lambda imap: pl.BlockSpec((None, nw, W), imap)
    cur = lambda b, r, j: (b, j, r)
    prev = lambda b, r, j: (b, jnp.maximum(j - 1, 0), r)
    use_prev = nb > 1
    if use_prev:
        in_specs = [blk(cur), blk(prev), blk(cur), blk(prev), blk(cur)]
        args = (view(qb), view(kb), view(kb), view(vb), view(vb))
    else:
        in_specs = [blk(cur), blk(cur), blk(cur)]
        args = (view(qb), view(kb), view(vb))
    o, lse = pl.pallas_call(
        functools.partial(_attn_kernel, use_prev=use_prev, j_axis=2),
        grid=(B, dil, nb),
        in_specs=in_specs,
        out_specs=[blk(cur),
                   pl.BlockSpec((None, None, nw, ATT_HEADS), lambda b, r, j: (b, r, j, 0))],
        out_shape=[jax.ShapeDtypeStruct((B, n, dil * W), BF16),
                   jax.ShapeDtypeStruct((B, dil, n, ATT_HEADS), F32)],
        compiler_params=_params("arbitrary", "arbitrary", "arbitrary"),
        name=f"attn_d{dil}",
    )(*args)
    lse = jnp.swapaxes(lse, 1, 2).reshape(B, S, ATT_HEADS)
    return o.reshape(B, S, W), lse


def _combine_kernel(o1_ref, o2_ref, o3_ref, l1_ref, l2_ref, l3_ref, e_ref, att_ref):
    l1, l2, l3 = l1_ref[...], l2_ref[...], l3_ref[...]
    m = jnp.maximum(jnp.maximum(l1, l2), l3)
    e1, e2, e3 = jnp.exp(l1 - m), jnp.exp(l2 - m), jnp.exp(l3 - m)
    inv = 1.0 / (e1 + e2 + e3)
    e = e_ref[...]
    acc = _dot_exact_rhs(e1 * inv, e) * o1_ref[...].astype(F32)
    acc = acc + _dot_exact_rhs(e2 * inv, e) * o2_ref[...].astype(F32)
    acc = acc + _dot_exact_rhs(e3 * inv, e) * o3_ref[...].astype(F32)
    att_ref[...] = acc.astype(att_ref.dtype)


def _head_expand():
    e = np.zeros((ATT_HEADS, ATT_W), np.float32)
    for h in range(ATT_HEADS):
        e[h, h * HEAD_DIM:(h + 1) * HEAD_DIM] = 1.0
    return jnp.asarray(e, BF16)


def _combine(os_, lses, tm):
    B, S, W = os_[0].shape
    ospec = pl.BlockSpec((None, tm, W), lambda b, i: (b, i, 0))
    lspec = pl.BlockSpec((None, tm, ATT_HEADS), lambda b, i: (b, i, 0))
    return pl.pallas_call(
        _combine_kernel,
        grid=(B, S // tm),
        in_specs=[ospec] * 3 + [lspec] * 3 + [pl.BlockSpec((ATT_HEADS, W), lambda b, i: (0, 0))],
        out_specs=ospec,
        out_shape=jax.ShapeDtypeStruct((B, S, W), BF16),
        compiler_params=_params("arbitrary", "arbitrary"),
        name="combine",
    )(*os_, *lses, _head_expand())


def _sample_counts(T, WB):
    i = np.arange(T)
    c = np.arange(WB)
    delta = WB + i[None, :] - c[:, None]
    cnt_c = np.zeros((WB, T), np.float32)
    diff = i[None, :] - i[:, None]
    cnt_n = np.zeros((T, T), np.float32)
    for win, dil in DIL_BRANCHES:
        cnt_c += ((delta > 0) & (delta % dil == 0) & (delta <= win)).astype(np.float32)
        cnt_n += ((diff >= 0) & (diff % dil == 0) & (diff <= win)).astype(np.float32)
    return cnt_c, cnt_n


def _sample_attn_kernel(q_ref, kn_ref, vn_ref, kt_ref, vt_ref, cc_ref, cn_ref, o_ref):
    cc, cn = cc_ref[...], cn_ref[...]
    outs = []
    for h in range(ATT_HEADS):
        sl = slice(HEAD_DIM * h, HEAD_DIM * (h + 1))
        qh = q_ref[:, sl]
        s_c = jnp.where(cc > 0, _dot(qh.astype(BF16), kt_ref[h].astype(BF16)), NEG_BIG)
        s_n = jnp.where(cn > 0, _dot_nt(qh, kn_ref[:, sl]), NEG_BIG)
        m = jnp.maximum(jnp.max(s_c, axis=-1, keepdims=True), jnp.max(s_n, axis=-1, keepdims=True))
        p_c = cc * jnp.exp(s_c - m)
        p_n = cn * jnp.exp(s_n - m)
        l = jnp.sum(p_c, axis=-1, keepdims=True) + jnp.sum(p_n, axis=-1, keepdims=True)
        acc = _dot_nt(p_c.astype(BF16), vt_ref[h].astype(BF16)) + _dot(p_n, vn_ref[:, sl])
        outs.append(acc / l)
    o_ref[...] = jnp.concatenate(outs, axis=1).astype(o_ref.dtype)


def _sample_attn(q, k_new, v_new, cache_kt, cache_vt):
    DB, T, W = q.shape
    WB = cache_kt.shape[-1]
    cnt_c, cnt_n = _sample_counts(T, WB)
    cc = jnp.asarray(cnt_c.T)
    cn = jnp.asarray(cnt_n.T)
    new_spec = pl.BlockSpec((None, T, W), lambda b: (b, 0, 0))
    cache_spec = pl.BlockSpec((None, ATT_HEADS, HEAD_DIM, WB), lambda b: (b, 0, 0, 0))
    const = lambda arr: pl.BlockSpec(arr.shape, lambda b: (0, 0))
    return pl.pallas_call(
        _sample_attn_kernel,
        grid=(DB,),
        in_specs=[new_spec, new_spec, new_spec, cache_spec, cache_spec, const(cc), const(cn)],
        out_specs=new_spec,
        out_shape=jax.ShapeDtypeStruct((DB, T, W), BF16),
        compiler_params=_params("arbitrary"),
        name="sample_attn",
    )(q, k_new, v_new, cache_kt, cache_vt, cc, cn)


def _rwkv_kernel(p_ref, shift_ref, s0_ref, mu_ref, w0_ref, w2_ref, a0_ref, a2_ref, g2_ref,
                 kk_ref, ka_ref, rk_ref, lg_ref, lb_ref, e_ref,
                 rw_ref, sout_ref, shout_ref, s_scr, prev_scr, *, n_double):
    C = p_ref.shape[0]
    first = pl.program_id(1) == 0

    @pl.when(first)
    def _():
        s_scr[...] = s0_ref[...]
        prev_scr[...] = shift_ref[...]

    P = p_ref[...]
    row = lax.broadcasted_iota(jnp.int32, (C, 1), 0)
    p_prev = jnp.where(row == 0, prev_scr[...], pltpu.roll(P, 1, 0))
    pm = P + (p_prev - P) * mu_ref[...]
    prev_scr[...] = P[C - 1:C, :]
    shout_ref[...] = P[C - 1:C, :]

    W = RWKV_W
    r, k, v = pm[:, 0:W], pm[:, W:2 * W], pm[:, 2 * W:3 * W]
    lora_wa = pm[:, 3 * W:3 * W + DECAY_LORA + AAA_LORA]
    gl = pm[:, 3 * W + DECAY_LORA + AAA_LORA:]
    wpre = w0_ref[...] + _dot(jnp.tanh(lora_wa).astype(BF16), w2_ref[...])
    neg = -wpre
    softplus = jnp.maximum(neg, 0.0) + jnp.log(1.0 + jnp.exp(-jnp.abs(neg)))
    logw = -jnp.exp(-softplus - 0.5)
    a = jax.nn.sigmoid(a0_ref[...] + _dot(lora_wa.astype(BF16), a2_ref[...]))
    g = _dot(jax.nn.sigmoid(gl).astype(BF16), g2_ref[...])
    e = e_ref[...]
    kkr = k * kk_ref[...]
    nrm = jnp.sqrt(_dot_exact_rhs(kkr * kkr, e))
    kk = kkr / jnp.maximum(nrm, 1e-12)
    kmod = k * (1.0 + (a - 1.0) * ka_ref[...])
    bonus = _dot_exact_rhs(r * kmod * rk_ref[...], e) * v

    ti = lax.broadcasted_iota(jnp.int32, (C, C), 0)
    si = lax.broadcasted_iota(jnp.int32, (C, C), 1)
    tri_incl = (ti >= si)
    tri_strict = (ti > si)
    tri_b = jnp.where(tri_incl, 1.0, 0.0).astype(BF16)
    lw_h = logw.astype(BF16)
    lw_r = logw - lw_h.astype(F32)
    lw_m = lw_r.astype(BF16)
    lw_l = (lw_r - lw_m.astype(F32)).astype(BF16)
    L = _dot(tri_b, lw_h) + (_dot(tri_b, lw_m) + _dot(tri_b, lw_l))
    e_l = jnp.exp(L)
    e_nl = jnp.exp(-L)
    a_t = -kk * jnp.exp(L - logw)
    r_t = r * e_l
    b_t = kk * a * e_nl
    k_t = kmod * e_nl
    w_c = e_l[C - 1:C, :]

    eye = jnp.where(ti == si, 1.0, 0.0)
    o_heads = []
    for h in range(RWKV_HEADS):
        sl = slice(HEAD_DIM * h, HEAD_DIM * (h + 1))
        s0 = s_scr[h]
        ar = jnp.concatenate([a_t[:, sl], r_t[:, sl]], axis=0).astype(BF16)
        bk = jnp.concatenate([b_t[:, sl], k_t[:, sl]], axis=0).astype(BF16)
        vh = v[:, sl]
        gm = _dot_nt(ar, bk)
        m_ab = jnp.where(tri_strict, gm[0:C, 0:C], 0.0)
        m_ak = jnp.where(tri_strict, gm[0:C, C:2 * C], 0.0)
        n_r = jnp.concatenate([jnp.where(tri_incl, gm[C:2 * C, 0:C], 0.0),
                               jnp.where(tri_incl, gm[C:2 * C, C:2 * C], 0.0)], axis=1)
        ars = _dot_nt(ar, s0.astype(BF16))
        x = ars[0:C] + _dot(m_ak.astype(BF16), vh.astype(BF16))
        t_inv = eye + m_ab
        mp = m_ab
        for _ in range(n_double):
            mp = _dot3(mp, mp)
            t_inv = t_inv + _dot3(t_inv, mp)
        u = _dot3(t_inv, x)
        uv = jnp.concatenate([u, vh], axis=0).astype(BF16)
        o_heads.append(ars[C:2 * C] + _dot(n_r.astype(BF16), uv))
        s_new = (s0 + _dot_tn(uv, bk)) * w_c[:, sl]
        s_scr[h] = s_new
        sout_ref[h] = s_new

    o = jnp.concatenate(o_heads, axis=1)
    mean = _dot_exact_rhs(o, e) * (1.0 / HEAD_DIM)
    d = o - mean
    var = _dot_exact_rhs(d * d, e) * (1.0 / HEAD_DIM)
    on = d * lax.rsqrt(var + LNX_EPS) * lg_ref[...] + lb_ref[...]
    rw_ref[...] = ((on + bonus) * g).astype(rw_ref.dtype)


def _rwkv(P, shift0, s0, lw, C):
    B, T, _ = P.shape
    assert T % C == 0 and C & (C - 1) == 0
    n_double = max(int(np.log2(C)) - 1, 0)
    W = RWKV_W
    e = np.zeros((W, W), np.float32)
    for h in range(RWKV_HEADS):
        e[h * HEAD_DIM:(h + 1) * HEAD_DIM, h * HEAD_DIM:(h + 1) * HEAD_DIM] = 1.0
    zpad = jnp.zeros((DECAY_LORA, W), F32)
    w2p = jnp.concatenate([lw["w2"], jnp.zeros((AAA_LORA, W), F32)], axis=0).astype(BF16)
    a2p = jnp.concatenate([zpad, lw["a2"]], axis=0).astype(BF16)
    row = lambda t: t.reshape(1, -1)
    consts = [row(lw["mu"]), row(lw["w0"]), w2p, row(lw["a0"]), a2p, lw["g2"].astype(BF16),
              row(lw["k_k"]), row(lw["k_a"]), row(lw["r_k"]), row(lw["lnx_g"]), row(lw["lnx_b"]),
              jnp.asarray(e, BF16)]
    const = lambda arr: pl.BlockSpec(arr.shape, lambda b, c: (0, 0))
    st_spec = pl.BlockSpec((None, RWKV_HEADS, HEAD_DIM, HEAD_DIM), lambda b, c: (b, 0, 0, 0))
    sh_spec = pl.BlockSpec((None, 1, RWKV_IN), lambda b, c: (b, 0, 0))
    rw, s_out, sh_out = pl.pallas_call(
        functools.partial(_rwkv_kernel, n_double=n_double),
        grid=(B, T // C),
        in_specs=[pl.BlockSpec((None, C, RWKV_IN), lambda b, c: (b, c, 0)), sh_spec, st_spec]
                 + [const(t) for t in consts],
        out_specs=[pl.BlockSpec((None, C, W), lambda b, c: (b, c, 0)), st_spec, sh_spec],
        out_shape=[jax.ShapeDtypeStruct((B, T, W), BF16),
                   jax.ShapeDtypeStruct((B, RWKV_HEADS, HEAD_DIM, HEAD_DIM), F32),
                   jax.ShapeDtypeStruct((B, 1, RWKV_IN), F32)],
        scratch_shapes=[pltpu.VMEM((RWKV_HEADS, HEAD_DIM, HEAD_DIM), F32),
                        pltpu.VMEM((1, RWKV_IN), F32)],
        compiler_params=_params("arbitrary", "arbitrary"),
        name=f"rwkv_c{C}",
    )(P, shift0.reshape(B, 1, RWKV_IN), s0, *consts)
    return rw, s_out, sh_out.reshape(B, RWKV_IN)


def _mlp_kernel(att_ref, rw_ref, x_ref, gt1_ref, sc2_ref, sh2_ref, gt2_ref, g2_ref, gf_ref,
                wo_ref, w1_ref, w2_ref, y_ref, *, final_norm, ff_chunk):
    half = wo_ref.shape[0] // 2
    mix = _dot(att_ref[...], wo_ref[0:half, :]) + _dot(rw_ref[...], wo_ref[half:, :])
    x1 = x_ref[...] + gt1_ref[...] * mix
    ms = jnp.mean(x1 * x1, axis=-1, keepdims=True)
    h2 = x1 * lax.rsqrt(ms + NORM_EPS) * g2_ref[...]
    h2 = (h2 * (1.0 + sc2_ref[...]) + sh2_ref[...]).astype(BF16)
    acc = jnp.zeros_like(x1)
    for c in range(w1_ref.shape[1] // ff_chunk):
        hid = _dot(h2, w1_ref[:, c * ff_chunk:(c + 1) * ff_chunk])
        hid = jnp.square(jnp.maximum(hid, 0.0)).astype(BF16)
        acc = acc + _dot(hid, w2_ref[c * ff_chunk:(c + 1) * ff_chunk, :])
    x2 = x1 + gt2_ref[...] * acc
    if final_norm:
        ms2 = jnp.mean(x2 * x2, axis=-1, keepdims=True)
        x2 = x2 * lax.rsqrt(ms2 + NORM_EPS) * gf_ref[...]
    y_ref[...] = x2


def _mlp(att, rw, x, gt1, sc2, sh2, gt2, g2, gf, wo_b, w1_b, w2_b, tm, per_row_mod, final_norm):
    G, R, D = x.shape
    dff = w1_b.shape[1]
    row_spec = lambda w: pl.BlockSpec((None, tm, w), lambda b, i: (b, i, 0))
    if per_row_mod:
        mod_spec = pl.BlockSpec((None, tm, D), lambda b, i: (b, i, 0))
    else:
        mod_spec = pl.BlockSpec((None, 1, D), lambda b, i: (b, 0, 0))
    const = lambda shape: pl.BlockSpec(shape, lambda b, i: (0, 0), pipeline_mode=pl.Buffered(1))
    return pl.pallas_call(
        functools.partial(_mlp_kernel, final_norm=final_norm, ff_chunk=1024),
        grid=(G, R // tm),
        in_specs=[row_spec(att.shape[-1]), row_spec(rw.shape[-1]), row_spec(D),
                  mod_spec, mod_spec, mod_spec, mod_spec, const((1, D)), const((1, D)),
                  const(wo_b.shape), const((D, dff)), const((dff, D))],
        out_specs=row_spec(D),
        out_shape=jax.ShapeDtypeStruct((G, R, D), F32),
        compiler_params=_params("arbitrary", "arbitrary"),
        name="mlp",
    )(att, rw, x, gt1, sc2, sh2, gt2, g2.reshape(1, D), gf.reshape(1, D), wo_b, w1_b, w2_b)


def kernel(x_prompt, x_sample, cache_k, cache_v, state_wkv, state_shift, c_prompt, c_sample, norm1_g, norm2_g, w_ada, b_ada, w_in, w_out, mu, w0, w2, a0, a2, g2, k_k, k_a, r_k, lnx_g, lnx_b, w_ff1, w_ff2, normf_g):
    B, S, D = x_prompt.shape
    DB, T, _ = x_sample.shape
    depth = w_in.shape[0]
    dt = x_prompt.dtype
    tabs_p = _rope_tables(jnp.arange(S))
    tabs_s = tuple(jnp.tile(t, (DB, 1)) for t in _rope_tables(PAST_LEN + jnp.arange(T)))
    tabs_t = tuple(jnp.repeat(t, DB, axis=0) for t in _rope_tables(PAST_LEN + jnp.arange(T)))
    c_all = jnp.concatenate([c_prompt, c_sample], axis=0)
    hp = x_prompt
    hs = x_sample.reshape(1, DB * T, D)
    outs = [[] for _ in range(8)]
    tm_s = min(ROW_TILE, DB * T)
    for l in range(depth):
        last = l == depth - 1
        mod = _ada(c_all, w_ada[l], b_ada[l])
        mod_p = [m.reshape(B, 1, D) for m in jnp.split(mod[:B], 6, axis=-1)]
        mod_s = [jnp.repeat(m, T, axis=0).reshape(1, DB * T, D) for m in jnp.split(mod[B:], 6, axis=-1)]
        mod_t = [m.reshape(1, DB, D) for m in jnp.split(mod[B:], 6, axis=-1)]
        w_in_b, wo_b = w_in[l].astype(BF16), w_out[l].astype(BF16)
        w1_b, w2_b = w_ff1[l].astype(BF16), w_ff2[l].astype(BF16)
        lw = dict(mu=mu[l], w0=w0[l], w2=w2[l], a0=a0[l], a2=a2[l], g2=g2[l], k_k=k_k[l], k_a=k_a[l],
                  r_k=r_k[l], lnx_g=lnx_g[l], lnx_b=lnx_b[l])

        sh1, sc1, gt1, sh2, sc2, gt2 = mod_p
        q, kb, vb, P, kt, vt = _inproj(hp, sc1, sh1, norm1_g[l], w_in_b, tabs_p, ROW_TILE, "batch", False, True)
        branches = [_attn_branch(q, kb, vb, dil) for _, dil in DIL_BRANCHES]
        att = _combine([b[0] for b in branches], [b[1] for b in branches], ROW_TILE)
        rw, s_p, shift_p = _rwkv(P, jnp.zeros((B, RWKV_IN), F32),
                                 jnp.zeros((B, RWKV_HEADS, HEAD_DIM, HEAD_DIM), F32), lw, RWKV_CHUNK)
        hp = _mlp(att, rw, hp, gt1, sc2, sh2, gt2, norm2_g[l], normf_g, wo_b, w1_b, w2_b, ROW_TILE, False, last)
        keep = min(max(w for w, _ in DIL_BRANCHES), S)
        win = lambda t: jnp.transpose(t.reshape(B, ATT_HEADS, HEAD_DIM, S)[..., S - keep:], (0, 3, 1, 2))
        outs[0].append(win(kt))
        outs[1].append(win(vt))
        outs[2].append(s_p)
        outs[3].append(shift_p)

        sh1, sc1, gt1, sh2, sc2, gt2 = mod_s
        q, _, _, P, k, v = _inproj(hs, sc1, sh1, norm1_g[l], w_in_b, tabs_s, tm_s, "row", True, False)
        hs_t = jnp.transpose(hs.reshape(DB, T, D), (1, 0, 2)).reshape(1, T * DB, D)
        *_, kt, vt = _inproj(hs_t, mod_t[1], mod_t[0], norm1_g[l], w_in_b, tabs_t, DB, "tile", False, True, nq=T)
        att = _sample_attn(q.reshape(DB, T, ATT_W).astype(F32), k.reshape(DB, T, ATT_W), v.reshape(DB, T, ATT_W),
                           jnp.transpose(cache_k[l], (0, 2, 3, 1)), jnp.transpose(cache_v[l], (0, 2, 3, 1)))
        rw, s_s, shift_s = _rwkv(P.reshape(DB, T, RWKV_IN), state_shift[l], state_wkv[l], lw, T)
        hs = _mlp(att.reshape(1, DB * T, ATT_W), rw.reshape(1, DB * T, RWKV_W), hs, gt1, sc2, sh2, gt2,
                  norm2_g[l], normf_g, wo_b, w1_b, w2_b, tm_s, True, last)
        new = lambda t: jnp.transpose(t.reshape(T, ATT_HEADS, HEAD_DIM, DB), (3, 0, 1, 2))
        outs[4].append(new(kt))
        outs[5].append(new(vt))
        outs[6].append(s_s)
        outs[7].append(shift_s)
    stack = lambda xs: jnp.stack(xs).astype(dt)
    return (hp.astype(dt), hs.reshape(DB, T, D).astype(x_sample.dtype), *[stack(o) for o in outs])
```

```python
import functools

import numpy as np
import jax
import jax.numpy as jnp
from jax import lax
from jax.experimental import pallas as pl
from jax.experimental.pallas import tpu as pltpu

F32 = jnp.float32
BF16 = jnp.bfloat16

HEAD_DIM = 64
ATT_HEADS = 8
ATT_W = ATT_HEADS * HEAD_DIM
RWKV_HEADS = 8
RWKV_W = RWKV_HEADS * HEAD_DIM
DIL_BRANCHES = ((128, 1), (512, 4), (2048, 16))
ROT_DIM = HEAD_DIM // 4
ROPE_THETA = 500000.0
DECAY_LORA = 64
AAA_LORA = 64
GATE_LORA = 128
RWKV_IN = 3 * RWKV_W + DECAY_LORA + AAA_LORA + GATE_LORA
NORM_EPS = 1e-6
LNX_EPS = 64e-5
PAST_LEN = 8192
ATT_BLOCK = 128
NEG_BIG = -1e30
VMEM_LIMIT = 56 * 1024 * 1024
ROW_TILE = 512
RWKV_CHUNK = 64


def _dot(a, b):
    return jnp.dot(a, b, preferred_element_type=F32)


def _dot_nt(a, b):
    return lax.dot_general(a, b, (((1,), (1,)), ((), ())), preferred_element_type=F32)


def _dot_tn(a, b):
    return lax.dot_general(a, b, (((0,), (0,)), ((), ())), preferred_element_type=F32)


def _split2(x):
    hi = x.astype(BF16)
    lo = (x - hi.astype(F32)).astype(BF16)
    return hi, lo


def _dot3(x, y):
    xh, xl = _split2(x)
    yh, yl = _split2(y)
    return _dot(xh, yh) + (_dot(xh, yl) + _dot(xl, yh))


def _dot_exact_rhs(x, e):
    xh, xl = _split2(x)
    return _dot(xh, e) + _dot(xl, e)


def _params(*sem):
    return pltpu.CompilerParams(dimension_semantics=sem, vmem_limit_bytes=VMEM_LIMIT)


def _ada_kernel(c_ref, w_ref, b_ref, o_ref):
    c = c_ref[...]
    s = c * jax.nn.sigmoid(c)
    o_ref[...] = _dot(s.astype(BF16), w_ref[...].astype(BF16)) + b_ref[...]


def _ada(c, w_ada, b_ada):
    n, d = c.shape
    cols = w_ada.shape[1]
    tn = 512
    return pl.pallas_call(
        _ada_kernel,
        grid=(cols // tn,),
        in_specs=[pl.BlockSpec((n, d), lambda j: (0, 0)),
                  pl.BlockSpec((d, tn), lambda j: (0, j)),
                  pl.BlockSpec((1, tn), lambda j: (0, j))],
        out_specs=pl.BlockSpec((n, tn), lambda j: (0, j)),
        out_shape=jax.ShapeDtypeStruct((n, cols), F32),
        compiler_params=_params("arbitrary"),
        name="ada",
    )(c, w_ada, b_ada.reshape(1, cols))


def _rope(t, cos, s1, s2):
    parts = []
    for c in range(ATT_W // 128):
        tc = t[:, 128 * c:128 * (c + 1)]
        parts.append(tc * cos + pltpu.roll(tc, 128 - ROT_DIM // 2, 1) * s1 + pltpu.roll(tc, ROT_DIM // 2, 1) * s2)
    return jnp.concatenate(parts, axis=1)


def _inproj_kernel(x_ref, sc_ref, sh_ref, g_ref, w_ref, cos_ref, s1_ref, s2_ref, *out_refs,
                   nat_f32, transposed):
    q_ref, kb_ref, vb_ref, p_ref = out_refs[:4]
    rest = list(out_refs[4:])
    x = x_ref[...]
    ms = jnp.mean(x * x, axis=-1, keepdims=True)
    h = x * lax.rsqrt(ms + NORM_EPS) * g_ref[...]
    h = (h * (1.0 + sc_ref[...]) + sh_ref[...]).astype(BF16)
    cos, s1, s2 = cos_ref[...], s1_ref[...], s2_ref[...]
    q = _rope(_dot(h, w_ref[:, 0:ATT_W]), cos, s1, s2)
    q_ref[...] = (q * HEAD_DIM ** -0.5).astype(q_ref.dtype)
    k = _rope(_dot(h, w_ref[:, ATT_W:2 * ATT_W]), cos, s1, s2)
    v = _dot(h, w_ref[:, 2 * ATT_W:3 * ATT_W])
    kb_ref[...] = k.astype(BF16)
    vb_ref[...] = v.astype(BF16)
    p_ref[...] = _dot(h, w_ref[:, 3 * ATT_W:])
    if nat_f32:
        rest.pop(0)[...] = k
        rest.pop(0)[...] = v
    if transposed:
        rest.pop(0)[...] = k.T
        rest.pop(0)[...] = v.T


def _inproj(x, sc, sh, g, w_in_b, tabs, tm, mod_mode, nat_f32, transposed, nq=1):
    G, R, D = x.shape
    nin = w_in_b.shape[1]
    row_spec = lambda w: pl.BlockSpec((None, tm, w), lambda b, i: (b, i, 0))
    mod_spec = {"batch": pl.BlockSpec((None, 1, D), lambda b, i: (b, 0, 0)),
                "row": pl.BlockSpec((None, tm, D), lambda b, i: (b, i, 0)),
                "tile": pl.BlockSpec((None, tm, D), lambda b, i: (b, 0, 0))}[mod_mode]
    tab_spec = pl.BlockSpec((tm, 128), lambda b, i: (i, 0))
    const = lambda shape: pl.BlockSpec(shape, lambda b, i: (0, 0))
    out_specs = [row_spec(ATT_W)] * 3 + [row_spec(RWKV_IN)]
    out_shape = [jax.ShapeDtypeStruct((G, R, ATT_W), BF16)] * 3 + [jax.ShapeDtypeStruct((G, R, RWKV_IN), F32)]
    if nat_f32:
        out_specs += [row_spec(ATT_W)] * 2
        out_shape += [jax.ShapeDtypeStruct((G, R, ATT_W), F32)] * 2
    if transposed:
        spq = (R // nq) // tm
        t_spec = pl.BlockSpec((None, None, ATT_W, tm), lambda b, i: (b, i // spq, 0, i % spq))
        out_specs += [t_spec] * 2
        out_shape += [jax.ShapeDtypeStruct((G, nq, ATT_W, R // nq), F32)] * 2
    return pl.pallas_call(
        functools.partial(_inproj_kernel, nat_f32=nat_f32, transposed=transposed),
        grid=(G, R // tm),
        in_specs=[row_spec(D), mod_spec, mod_spec, const((1, D)), const((D, nin)),
                  tab_spec, tab_spec, tab_spec],
        out_specs=out_specs,
        out_shape=out_shape,
        compiler_params=_params("arbitrary", "arbitrary"),
        name="inproj",
    )(x, sc, sh, g.reshape(1, D), w_in_b, *tabs)


def _rope_tables(pos):
    half = ROT_DIM // 2
    inv = ROPE_THETA ** (-jnp.arange(half, dtype=F32) * (2.0 / ROT_DIM))
    ang = pos.astype(F32)[:, None] * inv[None, :]
    cos, sin = jnp.cos(ang), jnp.sin(ang)
    n = pos.shape[0]
    ones = jnp.ones((n, HEAD_DIM - ROT_DIM), F32)
    zeros = jnp.zeros((n, HEAD_DIM - ROT_DIM), F32)
    zh = jnp.zeros((n, half), F32)
    ctab = jnp.concatenate([cos, cos, ones], axis=1)
    s1 = jnp.concatenate([-sin, zh, zeros], axis=1)
    s2 = jnp.concatenate([zh, sin, zeros], axis=1)
    return tuple(jnp.tile(t, (1, 128 // HEAD_DIM)) for t in (ctab, s1, s2))


def _attn_kernel(*refs, use_prev, j_axis):
    if use_prev:
        q_ref, kp_ref, kc_ref, vp_ref, vc_ref, o_ref, lse_ref = refs
    else:
        q_ref, kc_ref, vc_ref, o_ref, lse_ref = refs
    nw = ATT_BLOCK
    a = lax.broadcasted_iota(jnp.int32, (nw, nw), 0)
    c = lax.broadcasted_iota(jnp.int32, (nw, nw), 1)
    cur_ok = a >= c
    if use_prev:
        prev_ok = (c - a) >= jnp.where(pl.program_id(j_axis) > 0, 0, nw)
    lane = lax.broadcasted_iota(jnp.int32, (nw, ATT_HEADS), 1)
    lse_tile = jnp.zeros((nw, ATT_HEADS), F32)
    for h in range(ATT_HEADS):
        sl = slice(HEAD_DIM * h, HEAD_DIM * (h + 1))
        qh = q_ref[:, sl]
        s_c = jnp.where(cur_ok, _dot_nt(qh, kc_ref[:, sl]), NEG_BIG)
        m = jnp.max(s_c, axis=-1, keepdims=True)
        if use_prev:
            s_p = jnp.where(prev_ok, _dot_nt(qh, kp_ref[:, sl]), NEG_BIG)
            m = jnp.maximum(m, jnp.max(s_p, axis=-1, keepdims=True))
        p_c = jnp.exp(s_c - m)
        l = jnp.sum(p_c, axis=-1, keepdims=True)
        acc = _dot(p_c.astype(BF16), vc_ref[:, sl])
        if use_prev:
            p_p = jnp.exp(s_p - m)
            l = l + jnp.sum(p_p, axis=-1, keepdims=True)
            acc = acc + _dot(p_p.astype(BF16), vp_ref[:, sl])
        o_ref[:, sl] = (acc / l).astype(o_ref.dtype)
        lse_tile = jnp.where(lane == h, m + jnp.log(l), lse_tile)
    lse_ref[...] = lse_tile


def _attn_branch(qb, kb, vb, dil):
    B, S, W = qb.shape
    nw = ATT_BLOCK
    n = S // dil
    nb = n // nw
    assert n % nw == 0
    view = lambda t: t.reshape(B, n, dil * W)
    blk = lambda imap: pl.BlockSpec((None, nw, W), imap)
    cur = lambda b, r, j: (b, j, r)
    prev = lambda b, r, j: (b, jnp.maximum(j - 1, 0), r)
    use_prev = nb > 1
    if use_prev:
        in_specs = [blk(cur), blk(prev), blk(cur), blk(prev), blk(cur)]
        args = (view(qb), view(kb), view(kb), view(vb), view(vb))
    else:
        in_specs = [blk(cur), blk(cur), blk(cur)]
        args = (view(qb), view(kb), view(vb))
    o, lse = pl.pallas_call(
        functools.partial(_attn_kernel, use_prev=use_prev, j_axis=2),
        grid=(B, dil, nb),
        in_specs=in_specs,
        out_specs=[blk(cur),
                   pl.BlockSpec((None, None, nw, ATT_HEADS), lambda b, r, j: (b, r, j, 0))],
        out_shape=[jax.ShapeDtypeStruct((B, n, dil * W), BF16),
                   jax.ShapeDtypeStruct((B, dil, n, ATT_HEADS), F32)],
        compiler_params=_params("arbitrary", "arbitrary", "arbitrary"),
        name=f"attn_d{dil}",
    )(*args)
    lse = jnp.swapaxes(lse, 1, 2).reshape(B, S, ATT_HEADS)
    return o.reshape(B, S, W), lse


def _combine_kernel(o1_ref, o2_ref, o3_ref, l1_ref, l2_ref, l3_ref, e_ref, att_ref):
    l1, l2, l3 = l1_ref[...], l2_ref[...], l3_ref[...]
    m = jnp.maximum(jnp.maximum(l1, l2), l3)
    e1, e2, e3 = jnp.exp(l1 - m), jnp.exp(l2 - m), jnp.exp(l3 - m)
    inv = 1.0 / (e1 + e2 + e3)
    e = e_ref[...]
    acc = _dot_exact_rhs(e1 * inv, e) * o1_ref[...].astype(F32)
    acc = acc + _dot_exact_rhs(e2 * inv, e) * o2_ref[...].astype(F32)
    acc = acc + _dot_exact_rhs(e3 * inv, e) * o3_ref[...].astype(F32)
    att_ref[...] = acc.astype(att_ref.dtype)


def _head_expand():
    e = np.zeros((ATT_HEADS, ATT_W), np.float32)
    for h in range(ATT_HEADS):
        e[h, h * HEAD_DIM:(h + 1) * HEAD_DIM] = 1.0
    return jnp.asarray(e, BF16)


def _combine(os_, lses, tm):
    B, S, W = os_[0].shape
    ospec = pl.BlockSpec((None, tm, W), lambda b, i: (b, i, 0))
    lspec = pl.BlockSpec((None, tm, ATT_HEADS), lambda b, i: (b, i, 0))
    return pl.pallas_call(
        _combine_kernel,
        grid=(B, S // tm),
        in_specs=[ospec] * 3 + [lspec] * 3 + [pl.BlockSpec((ATT_HEADS, W), lambda b, i: (0, 0))],
        out_specs=ospec,
        out_shape=jax.ShapeDtypeStruct((B, S, W), BF16),
        compiler_params=_params("arbitrary", "arbitrary"),
        name="combine",
    )(*os_, *lses, _head_expand())


def _sample_counts(T, WB):
    i = np.arange(T)
    c = np.arange(WB)
    delta = WB + i[None, :] - c[:, None]
    cnt_c = np.zeros((WB, T), np.float32)
    diff = i[None, :] - i[:, None]
    cnt_n = np.zeros((T, T), np.float32)
    for win, dil in DIL_BRANCHES:
        cnt_c += ((delta > 0) & (delta % dil == 0) & (delta <= win)).astype(np.float32)
        cnt_n += ((diff >= 0) & (diff % dil == 0) & (diff <= win)).astype(np.float32)
    return cnt_c, cnt_n


def _sample_attn_kernel(q_ref, kn_ref, vn_ref, kt_ref, vt_ref, cc_ref, cn_ref, o_ref):
    cc, cn = cc_ref[...], cn_ref[...]
    outs = []
    for h in range(ATT_HEADS):
        sl = slice(HEAD_DIM * h, HEAD_DIM * (h + 1))
        qh = q_ref[:, sl]
        s_c = jnp.where(cc > 0, _dot(qh.astype(BF16), kt_ref[h].astype(BF16)), NEG_BIG)
        s_n = jnp.where(cn > 0, _dot_nt(qh, kn_ref[:, sl]), NEG_BIG)
        m = jnp.maximum(jnp.max(s_c, axis=-1, keepdims=True), jnp.max(s_n, axis=-1, keepdims=True))
        p_c = cc * jnp.exp(s_c - m)
        p_n = cn * jnp.exp(s_n - m)
        l = jnp.sum(p_c, axis=-1, keepdims=True) + jnp.sum(p_n, axis=-1, keepdims=True)
        acc = _dot_nt(p_c.astype(BF16), vt_ref[h].astype(BF16)) + _dot(p_n, vn_ref[:, sl])
        outs.append(acc / l)
    o_ref[...] = jnp.concatenate(outs, axis=1).astype(o_ref.dtype)


def _sample_attn(q, k_new, v_new, cache_kt, cache_vt):
    DB, T, W = q.shape
    WB = cache_kt.shape[-1]
    cnt_c, cnt_n = _sample_counts(T, WB)
    cc = jnp.asarray(cnt_c.T)
    cn = jnp.asarray(cnt_n.T)
    new_spec = pl.BlockSpec((None, T, W), lambda b: (b, 0, 0))
    cache_spec = pl.BlockSpec((None, ATT_HEADS, HEAD_DIM, WB), lambda b: (b, 0, 0, 0))
    const = lambda arr: pl.BlockSpec(arr.shape, lambda b: (0, 0))
    return pl.pallas_call(
        _sample_attn_kernel,
        grid=(DB,),
        in_specs=[new_spec, new_spec, new_spec, cache_spec, cache_spec, const(cc), const(cn)],
        out_specs=new_spec,
        out_shape=jax.ShapeDtypeStruct((DB, T, W), BF16),
        compiler_params=_params("arbitrary"),
        name="sample_attn",
    )(q, k_new, v_new, cache_kt, cache_vt, cc, cn)


def _rwkv_prologue(p_ref, shift_ref, prev_scr, shout_ref, mu_ref, w0_ref, w2_ref, a0_ref, a2_ref, g2_ref,
                   kk_ref, ka_ref, rk_ref, e):
    C = p_ref.shape[0]
    P = p_ref[...]
    row = lax.broadcasted_iota(jnp.int32, (C, 1), 0)
    p_prev = jnp.where(row == 0, prev_scr[...], pltpu.roll(P, 1, 0))
    pm = P + (p_prev - P) * mu_ref[...]
    prev_scr[...] = P[C - 1:C, :]
    shout_ref[...] = P[C - 1:C, :]

    W = RWKV_W
    r, k, v = pm[:, 0:W], pm[:, W:2 * W], pm[:, 2 * W:3 * W]
    lora_wa = pm[:, 3 * W:3 * W + DECAY_LORA + AAA_LORA]
    gl = pm[:, 3 * W + DECAY_LORA + AAA_LORA:]
    wpre = w0_ref[...] + _dot(jnp.tanh(lora_wa).astype(BF16), w2_ref[...])
    neg = -wpre
    softplus = jnp.maximum(neg, 0.0) + jnp.log(1.0 + jnp.exp(-jnp.abs(neg)))
    logw = -jnp.exp(-softplus - 0.5)
    a = jax.nn.sigmoid(a0_ref[...] + _dot(lora_wa.astype(BF16), a2_ref[...]))
    g = _dot(jax.nn.sigmoid(gl).astype(BF16), g2_ref[...])
    kkr = k * kk_ref[...]
    nrm = jnp.sqrt(_dot_exact_rhs(kkr * kkr, e))
    kk = kkr / jnp.maximum(nrm, 1e-12)
    kmod = k * (1.0 + (a - 1.0) * ka_ref[...])
    bonus = _dot_exact_rhs(r * kmod * rk_ref[...], e) * v

    ti = lax.broadcasted_iota(jnp.int32, (C, C), 0)
    si = lax.broadcasted_iota(jnp.int32, (C, C), 1)
    tri_b = jnp.where(ti >= si, 1.0, 0.0).astype(BF16)
    lw_h = logw.astype(BF16)
    lw_r = logw - lw_h.astype(F32)
    lw_m = lw_r.astype(BF16)
    lw_l = (lw_r - lw_m.astype(F32)).astype(BF16)
    L = _dot(tri_b, lw_h) + (_dot(tri_b, lw_m) + _dot(tri_b, lw_l))
    e_l = jnp.exp(L)
    e_nl = jnp.exp(-L)
    a_t = -kk * jnp.exp(L - logw)
    r_t = r * e_l
    b_t = kk * a * e_nl
    k_t = kmod * e_nl
    w_c = e_l[C - 1:C, :]
    return v, a_t, r_t, b_t, k_t, w_c, bonus, g


def _rwkv_epilogue(o, bonus, g, e, lg_ref, lb_ref, rw_ref):
    mean = _dot_exact_rhs(o, e) * (1.0 / HEAD_DIM)
    d = o - mean
    var = _dot_exact_rhs(d * d, e) * (1.0 / HEAD_DIM)
    on = d * lax.rsqrt(var + LNX_EPS) * lg_ref[...] + lb_ref[...]
    rw_ref[...] = ((on + bonus) * g).astype(rw_ref.dtype)


def _rwkv_kernel(p_ref, shift_ref, s0_ref, mu_ref, w0_ref, w2_ref, a0_ref, a2_ref, g2_ref,
                 kk_ref, ka_ref, rk_ref, lg_ref, lb_ref, e_ref,
                 rw_ref, sout_ref, shout_ref, s_scr, prev_scr, *, n_double):
    C = p_ref.shape[0]

    @pl.when(pl.program_id(1) == 0)
    def _():
        s_scr[...] = s0_ref[...]
        prev_scr[...] = shift_ref[...]

    e = e_ref[...]
    v, a_t, r_t, b_t, k_t, w_c, bonus, g = _rwkv_prologue(
        p_ref, shift_ref, prev_scr, shout_ref, mu_ref, w0_ref, w2_ref, a0_ref, a2_ref, g2_ref,
        kk_ref, ka_ref, rk_ref, e)
    ti = lax.broadcasted_iota(jnp.int32, (C, C), 0)
    si = lax.broadcasted_iota(jnp.int32, (C, C), 1)
    tri_incl = (ti >= si)
    tri_strict = (ti > si)
    eye = jnp.where(ti == si, 1.0, 0.0)
    o_heads = []
    for h in range(RWKV_HEADS):
        sl = slice(HEAD_DIM * h, HEAD_DIM * (h + 1))
        s0 = s_scr[h]
        ar = jnp.concatenate([a_t[:, sl], r_t[:, sl]], axis=0).astype(BF16)
        bk = jnp.concatenate([b_t[:, sl], k_t[:, sl]], axis=0).astype(BF16)
        vh = v[:, sl]
        gm = _dot_nt(ar, bk)
        m_ab = jnp.where(tri_strict, gm[0:C, 0:C], 0.0)
        m_ak = jnp.where(tri_strict, gm[0:C, C:2 * C], 0.0)
        n_r = jnp.concatenate([jnp.where(tri_incl, gm[C:2 * C, 0:C], 0.0),
                               jnp.where(tri_incl, gm[C:2 * C, C:2 * C], 0.0)], axis=1)
        ars = _dot_nt(ar, s0.astype(BF16))
        x = ars[0:C] + _dot(m_ak.astype(BF16), vh.astype(BF16))
        t_inv = eye + m_ab
        mp = m_ab
        for _ in range(n_double):
            mp = _dot3(mp, mp)
            t_inv = t_inv + _dot3(t_inv, mp)
        u = _dot3(t_inv, x)
        uv = jnp.concatenate([u, vh], axis=0).astype(BF16)
        o_heads.append(ars[C:2 * C] + _dot(n_r.astype(BF16), uv))
        s_new = (s0 + _dot_tn(uv, bk)) * w_c[:, sl]
        s_scr[h] = s_new
        sout_ref[h] = s_new
    _rwkv_epilogue(jnp.concatenate(o_heads, axis=1), bonus, g, e, lg_ref, lb_ref, rw_ref)


def _rwkv_pair_kernel(p_ref, shift_ref, s0_ref, mu_ref, w0_ref, w2_ref, a0_ref, a2_ref, g2_ref,
                      kk_ref, ka_ref, rk_ref, lg_ref, lb_ref, e_ref,
                      rw_ref, sout_ref, shout_ref, s_scr, prev_scr, *, n_double):
    C = p_ref.shape[0]
    n_pairs = RWKV_HEADS // 2
    PW = 2 * HEAD_DIM

    @pl.when(pl.program_id(1) == 0)
    def _():
        s_scr[...] = s0_ref[...]
        prev_scr[...] = shift_ref[...]

    e = e_ref[...]
    v, a_t, r_t, b_t, k_t, w_c, bonus, g = _rwkv_prologue(
        p_ref, shift_ref, prev_scr, shout_ref, mu_ref, w0_ref, w2_ref, a0_ref, a2_ref, g2_ref,
        kk_ref, ka_ref, rk_ref, e)

    lane = lax.broadcasted_iota(jnp.int32, (C, PW), 1)
    ti = lax.broadcasted_iota(jnp.int32, (C, PW), 0)
    si = lane & (HEAD_DIM - 1)
    first_head = lane < HEAD_DIM
    tri_incl = ti >= si
    tri_strict = ti > si
    eye = jnp.where(ti == si, 1.0, 0.0)
    rows2 = lax.broadcasted_iota(jnp.int32, (2 * C, PW), 0)
    lanes2 = lax.broadcasted_iota(jnp.int32, (2 * C, PW), 1)
    same_head = (rows2 < C) == (lanes2 < HEAD_DIM)

    def bd(x):
        zero = jnp.zeros_like(x)
        return jnp.concatenate([jnp.where(first_head, x, zero), jnp.where(first_head, zero, x)], axis=0)

    def dot3_bd(x, y):
        xh, xl = _split2(x)
        yh, yl = _split2(y)
        ybh = bd(yh)
        return (_dot(jnp.concatenate([xh, xl], axis=1), jnp.concatenate([ybh, ybh], axis=0))
                + _dot(xh, bd(yl)))

    pairs = range(n_pairs)
    sl = [slice(PW * p, PW * (p + 1)) for p in pairs]
    ar = [jnp.concatenate([a_t[:, sl[p]], r_t[:, sl[p]]], axis=0).astype(BF16) for p in pairs]
    bk = [jnp.concatenate([b_t[:, sl[p]], k_t[:, sl[p]]], axis=0).astype(BF16) for p in pairs]
    bk_bd = [jnp.concatenate([bd(bk[p][0:C]), bd(bk[p][C:2 * C])], axis=0) for p in pairs]
    gm = [_dot_nt(ar[p], bk_bd[p]) for p in pairs]
    m_ab = [jnp.where(tri_strict, gm[p][0:C, 0:PW], 0.0) for p in pairs]
    m_ak = [jnp.where(tri_strict, gm[p][0:C, PW:2 * PW], 0.0).astype(BF16) for p in pairs]
    n_r = [jnp.concatenate([jnp.where(tri_incl, gm[p][C:2 * C, 0:PW], 0.0),
                            jnp.where(tri_incl, gm[p][C:2 * C, PW:2 * PW], 0.0)], axis=1).astype(BF16)
           for p in pairs]
    t_inv = [eye + m_ab[p] for p in pairs]
    mp = m_ab
    for _ in range(n_double):
        mp = [dot3_bd(mp[p], mp[p]) for p in pairs]
        t_inv = [t_inv[p] + dot3_bd(t_inv[p], mp[p]) for p in pairs]
    v_bd = [bd(v[:, sl[p]].astype(BF16)) for p in pairs]
    mv = [_dot(m_ak[p], v_bd[p]) for p in pairs]
    s0 = [s_scr[p] for p in pairs]
    ars = [_dot_nt(ar[p], s0[p].astype(BF16)) for p in pairs]
    x = [ars[p][0:C] + mv[p] for p in pairs]
    u = [dot3_bd(t_inv[p], x[p]) for p in pairs]
    u_b = [u[p].astype(BF16) for p in pairs]
    o = [ars[p][C:2 * C] + _dot(n_r[p], jnp.concatenate([bd(u_b[p]), v_bd[p]], axis=0)) for p in pairs]
    for p in pairs:
        uv = jnp.concatenate([u_b[p], v[:, sl[p]].astype(BF16)], axis=0)
        upd = jnp.where(same_head, _dot_tn(uv, bk[p]), 0.0)
        s_new = (s0[p] + upd) * w_c[:, sl[p]]
        s_scr[p] = s_new
        sout_ref[p] = s_new
    _rwkv_epilogue(jnp.concatenate(o, axis=1), bonus, g, e, lg_ref, lb_ref, rw_ref)


def _rwkv(P, shift0, s0, lw, C):
    B, T, _ = P.shape
    assert T % C == 0 and C & (C - 1) == 0
    n_double = max(int(np.log2(C)) - 1, 0)
    W = RWKV_W
    e = np.zeros((W, W), np.float32)
    for h in range(RWKV_HEADS):
        e[h * HEAD_DIM:(h + 1) * HEAD_DIM, h * HEAD_DIM:(h + 1) * HEAD_DIM] = 1.0
    zpad = jnp.zeros((DECAY_LORA, W), F32)
    w2p = jnp.concatenate([lw["w2"], jnp.zeros((AAA_LORA, W), F32)], axis=0).astype(BF16)
    a2p = jnp.concatenate([zpad, lw["a2"]], axis=0).astype(BF16)
    row = lambda t: t.reshape(1, -1)
    consts = [row(lw["mu"]), row(lw["w0"]), w2p, row(lw["a0"]), a2p, lw["g2"].astype(BF16),
              row(lw["k_k"]), row(lw["k_a"]), row(lw["r_k"]), row(lw["lnx_g"]), row(lw["lnx_b"]),
              jnp.asarray(e, BF16)]
    paired = C == HEAD_DIM
    if paired:
        sp = s0.reshape(B, RWKV_HEADS // 2, 2, HEAD_DIM, HEAD_DIM)
        z = jnp.zeros_like(sp[:, :, 0])
        s0 = jnp.concatenate([jnp.concatenate([sp[:, :, 0], z], axis=-1),
                              jnp.concatenate([z, sp[:, :, 1]], axis=-1)], axis=-2)
    st_shape = s0.shape[1:]
    const = lambda arr: pl.BlockSpec(arr.shape, lambda b, c: (0, 0))
    st_spec = pl.BlockSpec((None,) + st_shape, lambda b, c: (b, 0, 0, 0))
    sh_spec = pl.BlockSpec((None, 1, RWKV_IN), lambda b, c: (b, 0, 0))
    rw, s_out, sh_out = pl.pallas_call(
        functools.partial(_rwkv_pair_kernel if paired else _rwkv_kernel, n_double=n_double),
        grid=(B, T // C),
        in_specs=[pl.BlockSpec((None, C, RWKV_IN), lambda b, c: (b, c, 0)), sh_spec, st_spec]
                 + [const(t) for t in consts],
        out_specs=[pl.BlockSpec((None, C, W), lambda b, c: (b, c, 0)), st_spec, sh_spec],
        out_shape=[jax.ShapeDtypeStruct((B, T, W), BF16),
                   jax.ShapeDtypeStruct((B,) + st_shape, F32),
                   jax.ShapeDtypeStruct((B, 1, RWKV_IN), F32)],
        scratch_shapes=[pltpu.VMEM(st_shape, F32),
                        pltpu.VMEM((1, RWKV_IN), F32)],
        compiler_params=_params("arbitrary", "arbitrary"),
        name=f"rwkv_c{C}",
    )(P, shift0.reshape(B, 1, RWKV_IN), s0, *consts)
    if paired:
        s_out = jnp.stack([s_out[:, :, :HEAD_DIM, :HEAD_DIM], s_out[:, :, HEAD_DIM:, HEAD_DIM:]], axis=2)
        s_out = s_out.reshape(B, RWKV_HEADS, HEAD_DIM, HEAD_DIM)
    return rw, s_out, sh_out.reshape(B, RWKV_IN)


def _mlp_kernel(att_ref, rw_ref, x_ref, gt1_ref, sc2_ref, sh2_ref, gt2_ref, g2_ref, gf_ref,
                wo_ref, w1_ref, w2_ref, y_ref, *, final_norm, ff_chunk):
    half = wo_ref.shape[0] // 2
    mix = _dot(att_ref[...], wo_ref[0:half, :]) + _dot(rw_ref[...], wo_ref[half:, :])
    x1 = x_ref[...] + gt1_ref[...] * mix
    ms = jnp.mean(x1 * x1, axis=-1, keepdims=True)
    h2 = x1 * lax.rsqrt(ms + NORM_EPS) * g2_ref[...]
    h2 = (h2 * (1.0 + sc2_ref[...]) + sh2_ref[...]).astype(BF16)
    acc = jnp.zeros_like(x1)
    for c in range(w1_ref.shape[1] // ff_chunk):
        hid = _dot(h2, w1_ref[:, c * ff_chunk:(c + 1) * ff_chunk])
        hid = jnp.square(jnp.maximum(hid, 0.0)).astype(BF16)
        acc = acc + _dot(hid, w2_ref[c * ff_chunk:(c + 1) * ff_chunk, :])
    x2 = x1 + gt2_ref[...] * acc
    if final_norm:
        ms2 = jnp.mean(x2 * x2, axis=-1, keepdims=True)
        x2 = x2 * lax.rsqrt(ms2 + NORM_EPS) * gf_ref[...]
    y_ref[...] = x2


def _mlp(att, rw, x, gt1, sc2, sh2, gt2, g2, gf, wo_b, w1_b, w2_b, tm, per_row_mod, final_norm):
    G, R, D = x.shape
    dff = w1_b.shape[1]
    row_spec = lambda w: pl.BlockSpec((None, tm, w), lambda b, i: (b, i, 0))
    if per_row_mod:
        mod_spec = pl.BlockSpec((None, tm, D), lambda b, i: (b, i, 0))
    else:
        mod_spec = pl.BlockSpec((None, 1, D), lambda b, i: (b, 0, 0))
    const = lambda shape: pl.BlockSpec(shape, lambda b, i: (0, 0), pipeline_mode=pl.Buffered(1))
    return pl.pallas_call(
        functools.partial(_mlp_kernel, final_norm=final_norm, ff_chunk=1024),
        grid=(G, R // tm),
        in_specs=[row_spec(att.shape[-1]), row_spec(rw.shape[-1]), row_spec(D),
                  mod_spec, mod_spec, mod_spec, mod_spec, const((1, D)), const((1, D)),
                  const(wo_b.shape), const((D, dff)), const((dff, D))],
        out_specs=row_spec(D),
        out_shape=jax.ShapeDtypeStruct((G, R, D), F32),
        compiler_params=_params("arbitrary", "arbitrary"),
        name="mlp",
    )(att, rw, x, gt1, sc2, sh2, gt2, g2.reshape(1, D), gf.reshape(1, D), wo_b, w1_b, w2_b)


def kernel(x_prompt, x_sample, cache_k, cache_v, state_wkv, state_shift, c_prompt, c_sample, norm1_g, norm2_g, w_ada, b_ada, w_in, w_out, mu, w0, w2, a0, a2, g2, k_k, k_a, r_k, lnx_g, lnx_b, w_ff1, w_ff2, normf_g):
    B, S, D = x_prompt.shape
    DB, T, _ = x_sample.shape
    depth = w_in.shape[0]
    dt = x_prompt.dtype
    tabs_p = _rope_tables(jnp.arange(S))
    tabs_s = tuple(jnp.tile(t, (DB, 1)) for t in _rope_tables(PAST_LEN + jnp.arange(T)))
    tabs_t = tuple(jnp.repeat(t, DB, axis=0) for t in _rope_tables(PAST_LEN + jnp.arange(T)))
    c_all = jnp.concatenate([c_prompt, c_sample], axis=0)
    hp = x_prompt
    hs = x_sample.reshape(1, DB * T, D)
    outs = [[] for _ in range(8)]
    tm_s = min(ROW_TILE, DB * T)
    for l in range(depth):
        last = l == depth - 1
        mod = _ada(c_all, w_ada[l], b_ada[l])
        mod_p = [m.reshape(B, 1, D) for m in jnp.split(mod[:B], 6, axis=-1)]
        mod_s = [jnp.repeat(m, T, axis=0).reshape(1, DB * T, D) for m in jnp.split(mod[B:], 6, axis=-1)]
        mod_t = [m.reshape(1, DB, D) for m in jnp.split(mod[B:], 6, axis=-1)]
        w_in_b, wo_b = w_in[l].astype(BF16), w_out[l].astype(BF16)
        w1_b, w2_b = w_ff1[l].astype(BF16), w_ff2[l].astype(BF16)
        lw = dict(mu=mu[l], w0=w0[l], w2=w2[l], a0=a0[l], a2=a2[l], g2=g2[l], k_k=k_k[l], k_a=k_a[l],
                  r_k=r_k[l], lnx_g=lnx_g[l], lnx_b=lnx_b[l])

        sh1, sc1, gt1, sh2, sc2, gt2 = mod_p
        q, kb, vb, P, kt, vt = _inproj(hp, sc1, sh1, norm1_g[l], w_in_b, tabs_p, ROW_TILE, "batch", False, True)
        branches = [_attn_branch(q, kb, vb, dil) for _, dil in DIL_BRANCHES]
        att = _combine([b[0] for b in branches], [b[1] for b in branches], ROW_TILE)
        rw, s_p, shift_p = _rwkv(P, jnp.zeros((B, RWKV_IN), F32),
                                 jnp.zeros((B, RWKV_HEADS, HEAD_DIM, HEAD_DIM), F32), lw, RWKV_CHUNK)
        hp = _mlp(att, rw, hp, gt1, sc2, sh2, gt2, norm2_g[l], normf_g, wo_b, w1_b, w2_b, ROW_TILE, False, last)
        keep = min(max(w for w, _ in DIL_BRANCHES), S)
        win = lambda t: jnp.transpose(t.reshape(B, ATT_HEADS, HEAD_DIM, S)[..., S - keep:], (0, 3, 1, 2))
        outs[0].append(win(kt))
        outs[1].append(win(vt))
        outs[2].append(s_p)
        outs[3].append(shift_p)

        sh1, sc1, gt1, sh2, sc2, gt2 = mod_s
        q, _, _, P, k, v = _inproj(hs, sc1, sh1, norm1_g[l], w_in_b, tabs_s, tm_s, "row", True, False)
        hs_t = jnp.transpose(hs.reshape(DB, T, D), (1, 0, 2)).reshape(1, T * DB, D)
        *_, kt, vt = _inproj(hs_t, mod_t[1], mod_t[0], norm1_g[l], w_in_b, tabs_t, DB, "tile", False, True, nq=T)
        att = _sample_attn(q.reshape(DB, T, ATT_W).astype(F32), k.reshape(DB, T, ATT_W), v.reshape(DB, T, ATT_W),
                           jnp.transpose(cache_k[l], (0, 2, 3, 1)), jnp.transpose(cache_v[l], (0, 2, 3, 1)))
        rw, s_s, shift_s = _rwkv(P.reshape(DB, T, RWKV_IN), state_shift[l], state_wkv[l], lw, T)
        hs = _mlp(att.reshape(1, DB * T, ATT_W), rw.reshape(1, DB * T, RWKV_W), hs, gt1, sc2, sh2, gt2,
                  norm2_g[l], normf_g, wo_b, w1_b, w2_b, tm_s, True, last)
        new = lambda t: jnp.transpose(t.reshape(T, ATT_HEADS, HEAD_DIM, DB), (3, 0, 1, 2))
        outs[4].append(new(kt))
        outs[5].append(new(vt))
        outs[6].append(s_s)
        outs[7].append(shift_s)
    stack = lambda xs: jnp.stack(xs).astype(dt)
    return (hp.astype(dt), hs.reshape(DB, T, D).astype(x_sample.dtype), *[stack(o) for o in outs])
```

```python
import functools

import numpy as np
import jax
import jax.numpy as jnp
from jax import lax
from jax.experimental import pallas as pl
from jax.experimental.pallas import tpu as pltpu

F32 = jnp.float32
BF16 = jnp.bfloat16

HEAD_DIM = 64
ATT_HEADS = 8
ATT_W = ATT_HEADS * HEAD_DIM
RWKV_HEADS = 8
RWKV_W = RWKV_HEADS * HEAD_DIM
DIL_BRANCHES = ((128, 1), (512, 4), (2048, 16))
ROT_DIM = HEAD_DIM // 4
ROPE_THETA = 500000.0
DECAY_LORA = 64
AAA_LORA = 64
GATE_LORA = 128
RWKV_IN = 3 * RWKV_W + DECAY_LORA + AAA_LORA + GATE_LORA
NORM_EPS = 1e-6
LNX_EPS = 64e-5
PAST_LEN = 8192
ATT_BLOCK = 128
NEG_BIG = -1e30
VMEM_LIMIT = 56 * 1024 * 1024
ROW_TILE = 512
RWKV_CHUNK = 64


def _dot(a, b):
    return jnp.dot(a, b, preferred_element_type=F32)


def _dot_nt(a, b):
    return lax.dot_general(a, b, (((1,), (1,)), ((), ())), preferred_element_type=F32)


def _dot_tn(a, b):
    return lax.dot_general(a, b, (((0,), (0,)), ((), ())), preferred_element_type=F32)


def _split2(x):
    hi = x.astype(BF16)
    lo = (x - hi.astype(F32)).astype(BF16)
    return hi, lo


def _dot3(x, y):
    xh, xl = _split2(x)
    yh, yl = _split2(y)
    return _dot(xh, yh) + (_dot(xh, yl) + _dot(xl, yh))


def _dot_exact_rhs(x, e):
    xh, xl = _split2(x)
    return _dot(xh, e) + _dot(xl, e)


def _params(*sem):
    return pltpu.CompilerParams(dimension_semantics=sem, vmem_limit_bytes=VMEM_LIMIT)


def _ada_kernel(c_ref, w_ref, b_ref, o_ref):
    c = c_ref[...]
    s = c * jax.nn.sigmoid(c)
    o_ref[...] = _dot(s.astype(BF16), w_ref[...].astype(BF16)) + b_ref[...]


def _ada(c, w_ada, b_ada):
    n, d = c.shape
    cols = w_ada.shape[1]
    tn = 512
    return pl.pallas_call(
        _ada_kernel,
        grid=(cols // tn,),
        in_specs=[pl.BlockSpec((n, d), lambda j: (0, 0)),
                  pl.BlockSpec((d, tn), lambda j: (0, j)),
                  pl.BlockSpec((1, tn), lambda j: (0, j))],
        out_specs=pl.BlockSpec((n, tn), lambda j: (0, j)),
        out_shape=jax.ShapeDtypeStruct((n, cols), F32),
        compiler_params=_params("arbitrary"),
        name="ada",
    )(c, w_ada, b_ada.reshape(1, cols))


def _rope(t, cos, s1, s2):
    parts = []
    for c in range(ATT_W // 128):
        tc = t[:, 128 * c:128 * (c + 1)]
        parts.append(tc * cos + pltpu.roll(tc, 128 - ROT_DIM // 2, 1) * s1 + pltpu.roll(tc, ROT_DIM // 2, 1) * s2)
    return jnp.concatenate(parts, axis=1)


def _inproj_kernel(x_ref, sc_ref, sh_ref, g_ref, w_ref, cos_ref, s1_ref, s2_ref, *out_refs,
                   nat_f32, transposed, p_t):
    q_ref, kb_ref, vb_ref, p_ref = out_refs[:4]
    rest = list(out_refs[4:])
    x = x_ref[...]
    ms = jnp.mean(x * x, axis=-1, keepdims=True)
    h = x * lax.rsqrt(ms + NORM_EPS) * g_ref[...]
    h = (h * (1.0 + sc_ref[...]) + sh_ref[...]).astype(BF16)
    cos, s1, s2 = cos_ref[...], s1_ref[...], s2_ref[...]
    q = _rope(_dot(h, w_ref[:, 0:ATT_W]), cos, s1, s2)
    q_ref[...] = (q * HEAD_DIM ** -0.5).astype(q_ref.dtype)
    k = _rope(_dot(h, w_ref[:, ATT_W:2 * ATT_W]), cos, s1, s2)
    v = _dot(h, w_ref[:, 2 * ATT_W:3 * ATT_W])
    kb_ref[...] = k.astype(BF16)
    vb_ref[...] = v.astype(BF16)
    p = _dot(h, w_ref[:, 3 * ATT_W:])
    p_ref[...] = p
    if nat_f32:
        rest.pop(0)[...] = k
        rest.pop(0)[...] = v
    if transposed:
        rest.pop(0)[...] = k.T
        rest.pop(0)[...] = v.T
    if p_t:
        rest.pop(0)[...] = p.T


def _inproj(x, sc, sh, g, w_in_b, tabs, tm, mod_mode, nat_f32, transposed, nq=1, p_t=False):
    G, R, D = x.shape
    nin = w_in_b.shape[1]
    row_spec = lambda w: pl.BlockSpec((None, tm, w), lambda b, i: (b, i, 0))
    mod_spec = {"batch": pl.BlockSpec((None, 1, D), lambda b, i: (b, 0, 0)),
                "row": pl.BlockSpec((None, tm, D), lambda b, i: (b, i, 0)),
                "tile": pl.BlockSpec((None, tm, D), lambda b, i: (b, 0, 0))}[mod_mode]
    tab_spec = pl.BlockSpec((tm, 128), lambda b, i: (i, 0))
    const = lambda shape: pl.BlockSpec(shape, lambda b, i: (0, 0))
    out_specs = [row_spec(ATT_W)] * 3 + [row_spec(RWKV_IN)]
    out_shape = [jax.ShapeDtypeStruct((G, R, ATT_W), BF16)] * 3 + [jax.ShapeDtypeStruct((G, R, RWKV_IN), F32)]
    if nat_f32:
        out_specs += [row_spec(ATT_W)] * 2
        out_shape += [jax.ShapeDtypeStruct((G, R, ATT_W), F32)] * 2
    if transposed:
        spq = (R // nq) // tm
        t_spec = pl.BlockSpec((None, None, ATT_W, tm), lambda b, i: (b, i // spq, 0, i % spq))
        out_specs += [t_spec] * 2
        out_shape += [jax.ShapeDtypeStruct((G, nq, ATT_W, R // nq), F32)] * 2
        if p_t:
            out_specs.append(pl.BlockSpec((None, None, RWKV_IN, tm), lambda b, i: (b, i // spq, 0, i % spq)))
            out_shape.append(jax.ShapeDtypeStruct((G, nq, RWKV_IN, R // nq), F32))
    return pl.pallas_call(
        functools.partial(_inproj_kernel, nat_f32=nat_f32, transposed=transposed, p_t=p_t),
        grid=(G, R // tm),
        in_specs=[row_spec(D), mod_spec, mod_spec, const((1, D)), const((D, nin)),
                  tab_spec, tab_spec, tab_spec],
        out_specs=out_specs,
        out_shape=out_shape,
        compiler_params=_params("arbitrary", "arbitrary"),
        name="inproj",
    )(x, sc, sh, g.reshape(1, D), w_in_b, *tabs)


def _rope_tables(pos):
    half = ROT_DIM // 2
    inv = ROPE_THETA ** (-jnp.arange(half, dtype=F32) * (2.0 / ROT_DIM))
    ang = pos.astype(F32)[:, None] * inv[None, :]
    cos, sin = jnp.cos(ang), jnp.sin(ang)
    n = pos.shape[0]
    ones = jnp.ones((n, HEAD_DIM - ROT_DIM), F32)
    zeros = jnp.zeros((n, HEAD_DIM - ROT_DIM), F32)
    zh = jnp.zeros((n, half), F32)
    ctab = jnp.concatenate([cos, cos, ones], axis=1)
    s1 = jnp.concatenate([-sin, zh, zeros], axis=1)
    s2 = jnp.concatenate([zh, sin, zeros], axis=1)
    return tuple(jnp.tile(t, (1, 128 // HEAD_DIM)) for t in (ctab, s1, s2))


def _attn_kernel(*refs, use_prev, j_axis):
    if use_prev:
        q_ref, kp_ref, kc_ref, vp_ref, vc_ref, o_ref, lse_ref = refs
    else:
        q_ref, kc_ref, vc_ref, o_ref, lse_ref = refs
    nw = ATT_BLOCK
    a = lax.broadcasted_iota(jnp.int32, (nw, nw), 0)
    c = lax.broadcasted_iota(jnp.int32, (nw, nw), 1)
    cur_ok = a >= c
    if use_prev:
        prev_ok = (c - a) >= jnp.where(pl.program_id(j_axis) > 0, 0, nw)
    lane = lax.broadcasted_iota(jnp.int32, (nw, ATT_HEADS), 1)
    lse_tile = jnp.zeros((nw, ATT_HEADS), F32)
    for h in range(ATT_HEADS):
        sl = slice(HEAD_DIM * h, HEAD_DIM * (h + 1))
        qh = q_ref[:, sl]
        s_c = jnp.where(cur_ok, _dot_nt(qh, kc_ref[:, sl]), NEG_BIG)
        m = jnp.max(s_c, axis=-1, keepdims=True)
        if use_prev:
            s_p = jnp.where(prev_ok, _dot_nt(qh, kp_ref[:, sl]), NEG_BIG)
            m = jnp.maximum(m, jnp.max(s_p, axis=-1, keepdims=True))
        p_c = jnp.exp(s_c - m)
        l = jnp.sum(p_c, axis=-1, keepdims=True)
        acc = _dot(p_c.astype(BF16), vc_ref[:, sl])
        if use_prev:
            p_p = jnp.exp(s_p - m)
            l = l + jnp.sum(p_p, axis=-1, keepdims=True)
            acc = acc + _dot(p_p.astype(BF16), vp_ref[:, sl])
        o_ref[:, sl] = (acc / l).astype(o_ref.dtype)
        lse_tile = jnp.where(lane == h, m + jnp.log(l), lse_tile)
    lse_ref[...] = lse_tile


def _attn_branch(qb, kb, vb, dil):
    B, S, W = qb.shape
    nw = ATT_BLOCK
    n = S // dil
    nb = n // nw
    assert n % nw == 0
    view = lambda t: t.reshape(B, n, dil * W)
    blk = lambda imap: pl.BlockSpec((None, nw, W), imap)
    cur = lambda b, r, j: (b, j, r)
    prev = lambda b, r, j: (b, jnp.maximum(j - 1, 0), r)
    use_prev = nb > 1
    if use_prev:
        in_specs = [blk(cur), blk(prev), blk(cur), blk(prev), blk(cur)]
        args = (view(qb), view(kb), view(kb), view(vb), view(vb))
    else:
        in_specs = [blk(cur), blk(cur), blk(cur)]
        args = (view(qb), view(kb), view(vb))
    o, lse = pl.pallas_call(
        functools.partial(_attn_kernel, use_prev=use_prev, j_axis=2),
        grid=(B, dil, nb),
        in_specs=in_specs,
        out_specs=[blk(cur),
                   pl.BlockSpec((None, None, nw, ATT_HEADS), lambda b, r, j: (b, r, j, 0))],
        out_shape=[jax.ShapeDtypeStruct((B, n, dil * W), BF16),
                   jax.ShapeDtypeStruct((B, dil, n, ATT_HEADS), F32)],
        compiler_params=_params("arbitrary", "arbitrary", "arbitrary"),
        name=f"attn_d{dil}",
    )(*args)
    lse = jnp.swapaxes(lse, 1, 2).reshape(B, S, ATT_HEADS)
    return o.reshape(B, S, W), lse


def _combine_kernel(o1_ref, o2_ref, o3_ref, l1_ref, l2_ref, l3_ref, e_ref, att_ref):
    l1, l2, l3 = l1_ref[...], l2_ref[...], l3_ref[...]
    m = jnp.maximum(jnp.maximum(l1, l2), l3)
    e1, e2, e3 = jnp.exp(l1 - m), jnp.exp(l2 - m), jnp.exp(l3 - m)
    inv = 1.0 / (e1 + e2 + e3)
    e = e_ref[...]
    acc = _dot_exact_rhs(e1 * inv, e) * o1_ref[...].astype(F32)
    acc = acc + _dot_exact_rhs(e2 * inv, e) * o2_ref[...].astype(F32)
    acc = acc + _dot_exact_rhs(e3 * inv, e) * o3_ref[...].astype(F32)
    att_ref[...] = acc.astype(att_ref.dtype)


def _head_expand():
    e = np.zeros((ATT_HEADS, ATT_W), np.float32)
    for h in range(ATT_HEADS):
        e[h, h * HEAD_DIM:(h + 1) * HEAD_DIM] = 1.0
    return jnp.asarray(e, BF16)


def _combine(os_, lses, tm):
    B, S, W = os_[0].shape
    ospec = pl.BlockSpec((None, tm, W), lambda b, i: (b, i, 0))
    lspec = pl.BlockSpec((None, tm, ATT_HEADS), lambda b, i: (b, i, 0))
    return pl.pallas_call(
        _combine_kernel,
        grid=(B, S // tm),
        in_specs=[ospec] * 3 + [lspec] * 3 + [pl.BlockSpec((ATT_HEADS, W), lambda b, i: (0, 0))],
        out_specs=ospec,
        out_shape=jax.ShapeDtypeStruct((B, S, W), BF16),
        compiler_params=_params("arbitrary", "arbitrary"),
        name="combine",
    )(*os_, *lses, _head_expand())


def _sample_counts(T, WB):
    i = np.arange(T)
    c = np.arange(WB)
    delta = WB + i[None, :] - c[:, None]
    cnt_c = np.zeros((WB, T), np.float32)
    diff = i[None, :] - i[:, None]
    cnt_n = np.zeros((T, T), np.float32)
    for win, dil in DIL_BRANCHES:
        cnt_c += ((delta > 0) & (delta % dil == 0) & (delta <= win)).astype(np.float32)
        cnt_n += ((diff >= 0) & (diff % dil == 0) & (diff <= win)).astype(np.float32)
    return cnt_c, cnt_n


def _sample_attn_kernel(q_ref, kn_ref, vn_ref, kt_ref, vt_ref, cc_ref, cn_ref, o_ref):
    cc, cn = cc_ref[...], cn_ref[...]
    outs = []
    for h in range(ATT_HEADS):
        sl = slice(HEAD_DIM * h, HEAD_DIM * (h + 1))
        qh = q_ref[:, sl]
        s_c = jnp.where(cc > 0, _dot(qh.astype(BF16), kt_ref[h].astype(BF16)), NEG_BIG)
        s_n = jnp.where(cn > 0, _dot_nt(qh, kn_ref[:, sl]), NEG_BIG)
        m = jnp.maximum(jnp.max(s_c, axis=-1, keepdims=True), jnp.max(s_n, axis=-1, keepdims=True))
        p_c = cc * jnp.exp(s_c - m)
        p_n = cn * jnp.exp(s_n - m)
        l = jnp.sum(p_c, axis=-1, keepdims=True) + jnp.sum(p_n, axis=-1, keepdims=True)
        acc = _dot_nt(p_c.astype(BF16), vt_ref[h].astype(BF16)) + _dot(p_n, vn_ref[:, sl])
        outs.append(acc / l)
    o_ref[...] = jnp.concatenate(outs, axis=1).astype(o_ref.dtype)


def _sample_attn(q, k_new, v_new, cache_kt, cache_vt):
    DB, T, W = q.shape
    WB = cache_kt.shape[-1]
    cnt_c, cnt_n = _sample_counts(T, WB)
    cc = jnp.asarray(cnt_c.T)
    cn = jnp.asarray(cnt_n.T)
    new_spec = pl.BlockSpec((None, T, W), lambda b: (b, 0, 0))
    cache_spec = pl.BlockSpec((None, ATT_HEADS, HEAD_DIM, WB), lambda b: (b, 0, 0, 0))
    const = lambda arr: pl.BlockSpec(arr.shape, lambda b: (0, 0))
    return pl.pallas_call(
        _sample_attn_kernel,
        grid=(DB,),
        in_specs=[new_spec, new_spec, new_spec, cache_spec, cache_spec, const(cc), const(cn)],
        out_specs=new_spec,
        out_shape=jax.ShapeDtypeStruct((DB, T, W), BF16),
        compiler_params=_params("arbitrary"),
        name="sample_attn",
    )(q, k_new, v_new, cache_kt, cache_vt, cc, cn)


def _rwkv_prologue(p_ref, shift_ref, prev_scr, shout_ref, mu_ref, w0_ref, w2_ref, a0_ref, a2_ref, g2_ref,
                   kk_ref, ka_ref, rk_ref, e):
    C = p_ref.shape[0]
    P = p_ref[...]
    row = lax.broadcasted_iota(jnp.int32, (C, 1), 0)
    p_prev = jnp.where(row == 0, prev_scr[...], pltpu.roll(P, 1, 0))
    pm = P + (p_prev - P) * mu_ref[...]
    prev_scr[...] = P[C - 1:C, :]
    shout_ref[...] = P[C - 1:C, :]

    W = RWKV_W
    r, k, v = pm[:, 0:W], pm[:, W:2 * W], pm[:, 2 * W:3 * W]
    lora_wa = pm[:, 3 * W:3 * W + DECAY_LORA + AAA_LORA]
    gl = pm[:, 3 * W + DECAY_LORA + AAA_LORA:]
    wpre = w0_ref[...] + _dot(jnp.tanh(lora_wa).astype(BF16), w2_ref[...])
    neg = -wpre
    softplus = jnp.maximum(neg, 0.0) + jnp.log(1.0 + jnp.exp(-jnp.abs(neg)))
    logw = -jnp.exp(-softplus - 0.5)
    a = jax.nn.sigmoid(a0_ref[...] + _dot(lora_wa.astype(BF16), a2_ref[...]))
    g = _dot(jax.nn.sigmoid(gl).astype(BF16), g2_ref[...])
    kkr = k * kk_ref[...]
    nrm = jnp.sqrt(_dot_exact_rhs(kkr * kkr, e))
    kk = kkr / jnp.maximum(nrm, 1e-12)
    kmod = k * (1.0 + (a - 1.0) * ka_ref[...])
    bonus = _dot_exact_rhs(r * kmod * rk_ref[...], e) * v

    ti = lax.broadcasted_iota(jnp.int32, (C, C), 0)
    si = lax.broadcasted_iota(jnp.int32, (C, C), 1)
    tri_b = jnp.where(ti >= si, 1.0, 0.0).astype(BF16)
    lw_h = logw.astype(BF16)
    lw_r = logw - lw_h.astype(F32)
    lw_m = lw_r.astype(BF16)
    lw_l = (lw_r - lw_m.astype(F32)).astype(BF16)
    L = _dot(tri_b, lw_h) + (_dot(tri_b, lw_m) + _dot(tri_b, lw_l))
    e_l = jnp.exp(L)
    e_nl = jnp.exp(-L)
    a_t = -kk * jnp.exp(L - logw)
    r_t = r * e_l
    b_t = kk * a * e_nl
    k_t = kmod * e_nl
    w_c = e_l[C - 1:C, :]
    return v, a_t, r_t, b_t, k_t, w_c, bonus, g


def _rwkv_epilogue(o, bonus, g, e, lg_ref, lb_ref, rw_ref):
    mean = _dot_exact_rhs(o, e) * (1.0 / HEAD_DIM)
    d = o - mean
    var = _dot_exact_rhs(d * d, e) * (1.0 / HEAD_DIM)
    on = d * lax.rsqrt(var + LNX_EPS) * lg_ref[...] + lb_ref[...]
    rw_ref[...] = ((on + bonus) * g).astype(rw_ref.dtype)


def _rwkv_pair_kernel(p_ref, shift_ref, s0_ref, mu_ref, w0_ref, w2_ref, a0_ref, a2_ref, g2_ref,
                      kk_ref, ka_ref, rk_ref, lg_ref, lb_ref, e_ref,
                      rw_ref, sout_ref, shout_ref, s_scr, prev_scr, *, n_double):
    C = p_ref.shape[0]
    n_pairs = RWKV_HEADS // 2
    PW = 2 * HEAD_DIM

    @pl.when(pl.program_id(1) == 0)
    def _():
        s_scr[...] = s0_ref[...]
        prev_scr[...] = shift_ref[...]

    e = e_ref[...]
    v, a_t, r_t, b_t, k_t, w_c, bonus, g = _rwkv_prologue(
        p_ref, shift_ref, prev_scr, shout_ref, mu_ref, w0_ref, w2_ref, a0_ref, a2_ref, g2_ref,
        kk_ref, ka_ref, rk_ref, e)

    lane = lax.broadcasted_iota(jnp.int32, (C, PW), 1)
    ti = lax.broadcasted_iota(jnp.int32, (C, PW), 0)
    si = lane & (HEAD_DIM - 1)
    first_head = lane < HEAD_DIM
    tri_incl = ti >= si
    tri_strict = ti > si
    eye = jnp.where(ti == si, 1.0, 0.0)
    rows2 = lax.broadcasted_iota(jnp.int32, (2 * C, PW), 0)
    lanes2 = lax.broadcasted_iota(jnp.int32, (2 * C, PW), 1)
    same_head = (rows2 < C) == (lanes2 < HEAD_DIM)

    def bd(x):
        zero = jnp.zeros_like(x)
        return jnp.concatenate([jnp.where(first_head, x, zero), jnp.where(first_head, zero, x)], axis=0)

    def dot3_bd(x, y):
        xh, xl = _split2(x)
        yh, yl = _split2(y)
        ybh = bd(yh)
        return (_dot(jnp.concatenate([xh, xl], axis=1), jnp.concatenate([ybh, ybh], axis=0))
                + _dot(xh, bd(yl)))

    pairs = range(n_pairs)
    sl = [slice(PW * p, PW * (p + 1)) for p in pairs]
    ar = [jnp.concatenate([a_t[:, sl[p]], r_t[:, sl[p]]], axis=0).astype(BF16) for p in pairs]
    bk = [jnp.concatenate([b_t[:, sl[p]], k_t[:, sl[p]]], axis=0).astype(BF16) for p in pairs]
    bk_bd = [jnp.concatenate([bd(bk[p][0:C]), bd(bk[p][C:2 * C])], axis=0) for p in pairs]
    gm = [_dot_nt(ar[p], bk_bd[p]) for p in pairs]
    m_ab = [jnp.where(tri_strict, gm[p][0:C, 0:PW], 0.0) for p in pairs]
    m_ak = [jnp.where(tri_strict, gm[p][0:C, PW:2 * PW], 0.0).astype(BF16) for p in pairs]
    n_r = [jnp.concatenate([jnp.where(tri_incl, gm[p][C:2 * C, 0:PW], 0.0),
                            jnp.where(tri_incl, gm[p][C:2 * C, PW:2 * PW], 0.0)], axis=1).astype(BF16)
           for p in pairs]
    t_inv = [eye + m_ab[p] for p in pairs]
    mp = m_ab
    for _ in range(n_double):
        mp = [dot3_bd(mp[p], mp[p]) for p in pairs]
        t_inv = [t_inv[p] + dot3_bd(t_inv[p], mp[p]) for p in pairs]
    v_bd = [bd(v[:, sl[p]].astype(BF16)) for p in pairs]
    mv = [_dot(m_ak[p], v_bd[p]) for p in pairs]
    s0 = [s_scr[p] for p in pairs]
    ars = [_dot_nt(ar[p], s0[p].astype(BF16)) for p in pairs]
    x = [ars[p][0:C] + mv[p] for p in pairs]
    u = [dot3_bd(t_inv[p], x[p]) for p in pairs]
    u_b = [u[p].astype(BF16) for p in pairs]
    o = [ars[p][C:2 * C] + _dot(n_r[p], jnp.concatenate([bd(u_b[p]), v_bd[p]], axis=0)) for p in pairs]
    for p in pairs:
        uv = jnp.concatenate([u_b[p], v[:, sl[p]].astype(BF16)], axis=0)
        upd = jnp.where(same_head, _dot_tn(uv, bk[p]), 0.0)
        s_new = (s0[p] + upd) * w_c[:, sl[p]]
        s_scr[p] = s_new
        sout_ref[p] = s_new
    _rwkv_epilogue(jnp.concatenate(o, axis=1), bonus, g, e, lg_ref, lb_ref, rw_ref)


def _rwkv(P, shift0, s0, lw, C):
    B, T, _ = P.shape
    assert T % C == 0 and C == HEAD_DIM, "the pair-packed chunk kernel needs chunk length == head dim"
    n_double = max(int(np.log2(C)) - 1, 0)
    W = RWKV_W
    e = np.zeros((W, W), np.float32)
    for h in range(RWKV_HEADS):
        e[h * HEAD_DIM:(h + 1) * HEAD_DIM, h * HEAD_DIM:(h + 1) * HEAD_DIM] = 1.0
    zpad = jnp.zeros((DECAY_LORA, W), F32)
    w2p = jnp.concatenate([lw["w2"], jnp.zeros((AAA_LORA, W), F32)], axis=0).astype(BF16)
    a2p = jnp.concatenate([zpad, lw["a2"]], axis=0).astype(BF16)
    row = lambda t: t.reshape(1, -1)
    consts = [row(lw["mu"]), row(lw["w0"]), w2p, row(lw["a0"]), a2p, lw["g2"].astype(BF16),
              row(lw["k_k"]), row(lw["k_a"]), row(lw["r_k"]), row(lw["lnx_g"]), row(lw["lnx_b"]),
              jnp.asarray(e, BF16)]
    sp = s0.reshape(B, RWKV_HEADS // 2, 2, HEAD_DIM, HEAD_DIM)
    z = jnp.zeros_like(sp[:, :, 0])
    s0 = jnp.concatenate([jnp.concatenate([sp[:, :, 0], z], axis=-1),
                          jnp.concatenate([z, sp[:, :, 1]], axis=-1)], axis=-2)
    st_shape = s0.shape[1:]
    const = lambda arr: pl.BlockSpec(arr.shape, lambda b, c: (0, 0))
    st_spec = pl.BlockSpec((None,) + st_shape, lambda b, c: (b, 0, 0, 0))
    sh_spec = pl.BlockSpec((None, 1, RWKV_IN), lambda b, c: (b, 0, 0))
    rw, s_out, sh_out = pl.pallas_call(
        functools.partial(_rwkv_pair_kernel, n_double=n_double),
        grid=(B, T // C),
        in_specs=[pl.BlockSpec((None, C, RWKV_IN), lambda b, c: (b, c, 0)), sh_spec, st_spec]
                 + [const(t) for t in consts],
        out_specs=[pl.BlockSpec((None, C, W), lambda b, c: (b, c, 0)), st_spec, sh_spec],
        out_shape=[jax.ShapeDtypeStruct((B, T, W), BF16),
                   jax.ShapeDtypeStruct((B,) + st_shape, F32),
                   jax.ShapeDtypeStruct((B, 1, RWKV_IN), F32)],
        scratch_shapes=[pltpu.VMEM(st_shape, F32),
                        pltpu.VMEM((1, RWKV_IN), F32)],
        compiler_params=_params("arbitrary", "arbitrary"),
        name=f"rwkv_c{C}",
    )(P, shift0.reshape(B, 1, RWKV_IN), s0, *consts)
    s_out = jnp.stack([s_out[:, :, :HEAD_DIM, :HEAD_DIM], s_out[:, :, HEAD_DIM:, HEAD_DIM:]], axis=2)
    s_out = s_out.reshape(B, RWKV_HEADS, HEAD_DIM, HEAD_DIM)
    return rw, s_out, sh_out.reshape(B, RWKV_IN)


def _rwkv_scan_kernel(r_ref, k_ref, v_ref, lo_ref, sr_ref, sk_ref, sv_ref, slo_ref, s_ref,
                      mur_ref, muk_ref, muv_ref, mulo_ref, w0_ref, w2t_ref, a0_ref, a2t_ref, g2t_ref,
                      kk_ref, ka_ref, rk_ref, lg_ref, lb_ref,
                      rw_ref, sout_ref,
                      w_scr, nkk_scr, bb_scr, km_scr, rr_scr, vv_scr, bonus_scr, g_scr, o_scr):
    T = r_ref.shape[0]

    def shifted(ref, s0_ref, mu_ref, t):
        cur = ref[t]
        prev = s0_ref[...] if t == 0 else ref[t - 1]
        return cur + (prev - cur) * mu_ref[...]

    for t in range(T):
        r = shifted(r_ref, sr_ref, mur_ref, t)
        k = shifted(k_ref, sk_ref, muk_ref, t)
        v = shifted(v_ref, sv_ref, muv_ref, t)
        lo = shifted(lo_ref, slo_ref, mulo_ref, t)
        wl, al, gl = lo[0:DECAY_LORA], lo[DECAY_LORA:DECAY_LORA + AAA_LORA], lo[DECAY_LORA + AAA_LORA:]
        neg = -(w0_ref[...] + _dot(w2t_ref[...], jnp.tanh(wl).astype(BF16)))
        softplus = jnp.maximum(neg, 0.0) + jnp.log(1.0 + jnp.exp(-jnp.abs(neg)))
        w = jnp.exp(-jnp.exp(-softplus - 0.5))
        a = jax.nn.sigmoid(a0_ref[...] + _dot(a2t_ref[...], al.astype(BF16)))
        g_scr[t] = _dot(g2t_ref[...], jax.nn.sigmoid(gl).astype(BF16))
        kkr = k * kk_ref[...]
        kk = kkr / jnp.maximum(jnp.sqrt(jnp.sum(kkr * kkr, axis=0, keepdims=True)), 1e-12)
        kmod = k * (1.0 + (a - 1.0) * ka_ref[...])
        bonus_scr[t] = jnp.sum(r * kmod * rk_ref[...], axis=0, keepdims=True) * v
        w_scr[t] = w
        nkk_scr[t] = -kk
        bb_scr[t] = kk * a
        km_scr[t] = kmod
        rr_scr[t] = r
        vv_scr[t] = v

    def row(i, carry):
        s = s_ref[i]
        for t in range(T):
            sa = jnp.sum(s * nkk_scr[t], axis=0, keepdims=True)
            s = s * w_scr[t] + sa * bb_scr[t] + vv_scr[t, pl.ds(i, 1), :] * km_scr[t]
            o_scr[t, pl.ds(i, 1), :] = jnp.sum(s * rr_scr[t], axis=0, keepdims=True)
        sout_ref[i] = s
        return carry

    lax.fori_loop(0, HEAD_DIM, row, 0)
    o = o_scr[...]
    mean = jnp.mean(o, axis=1, keepdims=True)
    d = o - mean
    var = jnp.mean(d * d, axis=1, keepdims=True)
    on = d * lax.rsqrt(var + LNX_EPS) * lg_ref[...] + lb_ref[...]
    rw_ref[...] = ((on + bonus_scr[...]) * g_scr[...]).astype(rw_ref.dtype)


def _rwkv_scan(pt, shift0, s0t, lw):
    T, _, DB = pt.shape
    W, HD = RWKV_W, HEAD_DIM
    lora_w = DECAY_LORA + AAA_LORA + GATE_LORA
    assert (3 * W) % lora_w == 0
    col = lambda t: jnp.broadcast_to(t.reshape(-1, 1), (t.size, DB))
    shift_t = shift0.T
    mu_c = col(lw["mu"])
    head = lambda base: (lambda h: (base + h, 0))
    pt_blk = lambda base: pl.BlockSpec((T, HD, DB), lambda h: (0, base + h, 0))
    blk2 = lambda base: pl.BlockSpec((HD, DB), head(base))
    lo_pt = pl.BlockSpec((T, lora_w, DB), lambda h: (0, 3 * W // lora_w, 0))
    lo_2d = pl.BlockSpec((lora_w, DB), lambda h: (3 * W // lora_w, 0))
    nh = W // HD
    st_spec = pl.BlockSpec((None, HD, HD, DB), lambda h: (h, 0, 0, 0))
    wt_spec = lambda n: pl.BlockSpec((HD, n), lambda h: (h, 0))
    scr = lambda: pltpu.VMEM((T, HD, DB), F32)
    return pl.pallas_call(
        _rwkv_scan_kernel,
        grid=(nh,),
        in_specs=[pt_blk(0), pt_blk(nh), pt_blk(2 * nh), lo_pt,
                  blk2(0), blk2(nh), blk2(2 * nh), lo_2d, st_spec,
                  blk2(0), blk2(nh), blk2(2 * nh), lo_2d,
                  blk2(0), wt_spec(DECAY_LORA), blk2(0), wt_spec(AAA_LORA), wt_spec(GATE_LORA),
                  blk2(0), blk2(0), blk2(0), blk2(0), blk2(0)],
        out_specs=[pl.BlockSpec((T, HD, DB), lambda h: (0, h, 0)), st_spec],
        out_shape=[jax.ShapeDtypeStruct((T, W, DB), F32), jax.ShapeDtypeStruct(s0t.shape, F32)],
        scratch_shapes=[scr() for _ in range(9)],
        compiler_params=_params("arbitrary"),
        name="rwkv_scan",
    )(pt, pt, pt, pt, shift_t, shift_t, shift_t, shift_t, s0t,
      mu_c, mu_c, mu_c, mu_c, col(lw["w0"]), lw["w2"].T.astype(BF16), col(lw["a0"]), lw["a2"].T.astype(BF16),
      lw["g2"].T.astype(BF16), col(lw["k_k"]), col(lw["k_a"]), col(lw["r_k"]), col(lw["lnx_g"]), col(lw["lnx_b"]))


def _mlp_kernel(att_ref, rw_ref, x_ref, gt1_ref, sc2_ref, sh2_ref, gt2_ref, g2_ref, gf_ref,
                wo_ref, w1_ref, w2_ref, y_ref, *, final_norm, ff_chunk):
    half = wo_ref.shape[0] // 2
    mix = _dot(att_ref[...], wo_ref[0:half, :]) + _dot(rw_ref[...], wo_ref[half:, :])
    x1 = x_ref[...] + gt1_ref[...] * mix
    ms = jnp.mean(x1 * x1, axis=-1, keepdims=True)
    h2 = x1 * lax.rsqrt(ms + NORM_EPS) * g2_ref[...]
    h2 = (h2 * (1.0 + sc2_ref[...]) + sh2_ref[...]).astype(BF16)
    acc = jnp.zeros_like(x1)
    for c in range(w1_ref.shape[1] // ff_chunk):
        hid = _dot(h2, w1_ref[:, c * ff_chunk:(c + 1) * ff_chunk])
        hid = jnp.square(jnp.maximum(hid, 0.0)).astype(BF16)
        acc = acc + _dot(hid, w2_ref[c * ff_chunk:(c + 1) * ff_chunk, :])
    x2 = x1 + gt2_ref[...] * acc
    if final_norm:
        ms2 = jnp.mean(x2 * x2, axis=-1, keepdims=True)
        x2 = x2 * lax.rsqrt(ms2 + NORM_EPS) * gf_ref[...]
    y_ref[...] = x2


def _mlp(att, rw, x, gt1, sc2, sh2, gt2, g2, gf, wo_b, w1_b, w2_b, tm, per_row_mod, final_norm):
    G, R, D = x.shape
    dff = w1_b.shape[1]
    row_spec = lambda w: pl.BlockSpec((None, tm, w), lambda b, i: (b, i, 0))
    if per_row_mod:
        mod_spec = pl.BlockSpec((None, tm, D), lambda b, i: (b, i, 0))
    else:
        mod_spec = pl.BlockSpec((None, 1, D), lambda b, i: (b, 0, 0))
    const = lambda shape: pl.BlockSpec(shape, lambda b, i: (0, 0), pipeline_mode=pl.Buffered(1))
    return pl.pallas_call(
        functools.partial(_mlp_kernel, final_norm=final_norm, ff_chunk=1024),
        grid=(G, R // tm),
        in_specs=[row_spec(att.shape[-1]), row_spec(rw.shape[-1]), row_spec(D),
                  mod_spec, mod_spec, mod_spec, mod_spec, const((1, D)), const((1, D)),
                  const(wo_b.shape), const((D, dff)), const((dff, D))],
        out_specs=row_spec(D),
        out_shape=jax.ShapeDtypeStruct((G, R, D), F32),
        compiler_params=_params("arbitrary", "arbitrary"),
        name="mlp",
    )(att, rw, x, gt1, sc2, sh2, gt2, g2.reshape(1, D), gf.reshape(1, D), wo_b, w1_b, w2_b)


def kernel(x_prompt, x_sample, cache_k, cache_v, state_wkv, state_shift, c_prompt, c_sample, norm1_g, norm2_g, w_ada, b_ada, w_in, w_out, mu, w0, w2, a0, a2, g2, k_k, k_a, r_k, lnx_g, lnx_b, w_ff1, w_ff2, normf_g):
    B, S, D = x_prompt.shape
    DB, T, _ = x_sample.shape
    depth = w_in.shape[0]
    dt = x_prompt.dtype
    tabs_p = _rope_tables(jnp.arange(S))
    tabs_s = tuple(jnp.tile(t, (DB, 1)) for t in _rope_tables(PAST_LEN + jnp.arange(T)))
    tabs_t = tuple(jnp.repeat(t, DB, axis=0) for t in _rope_tables(PAST_LEN + jnp.arange(T)))
    c_all = jnp.concatenate([c_prompt, c_sample], axis=0)
    hp = x_prompt
    hs = x_sample.reshape(1, DB * T, D)
    outs = [[] for _ in range(8)]
    tm_s = min(ROW_TILE, DB * T)
    for l in range(depth):
        last = l == depth - 1
        mod = _ada(c_all, w_ada[l], b_ada[l])
        mod_p = [m.reshape(B, 1, D) for m in jnp.split(mod[:B], 6, axis=-1)]
        mod_s = [jnp.repeat(m, T, axis=0).reshape(1, DB * T, D) for m in jnp.split(mod[B:], 6, axis=-1)]
        mod_t = [m.reshape(1, DB, D) for m in jnp.split(mod[B:], 6, axis=-1)]
        w_in_b, wo_b = w_in[l].astype(BF16), w_out[l].astype(BF16)
        w1_b, w2_b = w_ff1[l].astype(BF16), w_ff2[l].astype(BF16)
        lw = dict(mu=mu[l], w0=w0[l], w2=w2[l], a0=a0[l], a2=a2[l], g2=g2[l], k_k=k_k[l], k_a=k_a[l],
                  r_k=r_k[l], lnx_g=lnx_g[l], lnx_b=lnx_b[l])

        sh1, sc1, gt1, sh2, sc2, gt2 = mod_p
        q, kb, vb, P, kt, vt = _inproj(hp, sc1, sh1, norm1_g[l], w_in_b, tabs_p, ROW_TILE, "batch", False, True)
        branches = [_attn_branch(q, kb, vb, dil) for _, dil in DIL_BRANCHES]
        att = _combine([b[0] for b in branches], [b[1] for b in branches], ROW_TILE)
        rw, s_p, shift_p = _rwkv(P, jnp.zeros((B, RWKV_IN), F32),
                                 jnp.zeros((B, RWKV_HEADS, HEAD_DIM, HEAD_DIM), F32), lw, RWKV_CHUNK)
        hp = _mlp(att, rw, hp, gt1, sc2, sh2, gt2, norm2_g[l], normf_g, wo_b, w1_b, w2_b, ROW_TILE, False, last)
        keep = min(max(w for w, _ in DIL_BRANCHES), S)
        win = lambda t: jnp.transpose(t.reshape(B, ATT_HEADS, HEAD_DIM, S)[..., S - keep:], (0, 3, 1, 2))
        outs[0].append(win(kt))
        outs[1].append(win(vt))
        outs[2].append(s_p)
        outs[3].append(shift_p)

        sh1, sc1, gt1, sh2, sc2, gt2 = mod_s
        q, _, _, P, k, v = _inproj(hs, sc1, sh1, norm1_g[l], w_in_b, tabs_s, tm_s, "row", True, False)
        hs_t = jnp.transpose(hs.reshape(DB, T, D), (1, 0, 2)).reshape(1, T * DB, D)
        *_, kt, vt, pt = _inproj(hs_t, mod_t[1], mod_t[0], norm1_g[l], w_in_b, tabs_t, DB, "tile", False, True,
                                 nq=T, p_t=True)
        att = _sample_attn(q.reshape(DB, T, ATT_W).astype(F32), k.reshape(DB, T, ATT_W), v.reshape(DB, T, ATT_W),
                           jnp.transpose(cache_k[l], (0, 2, 3, 1)), jnp.transpose(cache_v[l], (0, 2, 3, 1)))
        rw_t, s_t = _rwkv_scan(pt.reshape(T, RWKV_IN, DB), state_shift[l],
                               jnp.transpose(state_wkv[l], (1, 2, 3, 0)), lw)
        rw = jnp.transpose(rw_t, (2, 0, 1)).astype(BF16)
        s_s = jnp.transpose(s_t, (3, 0, 1, 2))
        shift_s = P.reshape(DB, T, RWKV_IN)[:, T - 1]
        hs = _mlp(att.reshape(1, DB * T, ATT_W), rw.reshape(1, DB * T, RWKV_W), hs, gt1, sc2, sh2, gt2,
                  norm2_g[l], normf_g, wo_b, w1_b, w2_b, tm_s, True, last)
        new = lambda t: jnp.transpose(t.reshape(T, ATT_HEADS, HEAD_DIM, DB), (3, 0, 1, 2))
        outs[4].append(new(kt))
        outs[5].append(new(vt))
        outs[6].append(s_s)
        outs[7].append(shift_s)
    stack = lambda xs: jnp.stack(xs).astype(dt)
    return (hp.astype(dt), hs.reshape(DB, T, D).astype(x_sample.dtype), *[stack(o) for o in outs])
```

```python
import functools

import numpy as np
import jax
import jax.numpy as jnp
from jax import lax
from jax.experimental import pallas as pl
from jax.experimental.pallas import tpu as pltpu

F32 = jnp.float32
BF16 = jnp.bfloat16

HEAD_DIM = 64
ATT_HEADS = 8
ATT_W = ATT_HEADS * HEAD_DIM
RWKV_HEADS = 8
RWKV_W = RWKV_HEADS * HEAD_DIM
DIL_BRANCHES = ((128, 1), (512, 4), (2048, 16))
ROT_DIM = HEAD_DIM // 4
ROPE_THETA = 500000.0
DECAY_LORA = 64
AAA_LORA = 64
GATE_LORA = 128
RWKV_IN = 3 * RWKV_W + DECAY_LORA + AAA_LORA + GATE_LORA
NORM_EPS = 1e-6
LNX_EPS = 64e-5
PAST_LEN = 8192
ATT_BLOCK = 128
NEG_BIG = -1e30
VMEM_LIMIT = 56 * 1024 * 1024
ROW_TILE = 512
RWKV_CHUNK = 64
ATT_GROUP = 4


def _dot(a, b):
    return jnp.dot(a, b, preferred_element_type=F32)


def _dot_nt(a, b):
    return lax.dot_general(a, b, (((1,), (1,)), ((), ())), preferred_element_type=F32)


def _dot_tn(a, b):
    return lax.dot_general(a, b, (((0,), (0,)), ((), ())), preferred_element_type=F32)


def _split2(x):
    hi = x.astype(BF16)
    lo = (x - hi.astype(F32)).astype(BF16)
    return hi, lo


def _dot3(x, y):
    xh, xl = _split2(x)
    yh, yl = _split2(y)
    return _dot(xh, yh) + (_dot(xh, yl) + _dot(xl, yh))


def _dot_exact_rhs(x, e):
    xh, xl = _split2(x)
    return _dot(xh, e) + _dot(xl, e)


def _params(*sem):
    return pltpu.CompilerParams(dimension_semantics=sem, vmem_limit_bytes=VMEM_LIMIT)


def _ada_kernel(c_ref, w_ref, b_ref, o_ref):
    c = c_ref[...]
    s = c * jax.nn.sigmoid(c)
    o_ref[...] = _dot(s.astype(BF16), w_ref[...].astype(BF16)) + b_ref[...]


def _ada(c, w_ada, b_ada):
    n, d = c.shape
    cols = w_ada.shape[1]
    tn = 512
    return pl.pallas_call(
        _ada_kernel,
        grid=(cols // tn,),
        in_specs=[pl.BlockSpec((n, d), lambda j: (0, 0)),
                  pl.BlockSpec((d, tn), lambda j: (0, j)),
                  pl.BlockSpec((1, tn), lambda j: (0, j))],
        out_specs=pl.BlockSpec((n, tn), lambda j: (0, j)),
        out_shape=jax.ShapeDtypeStruct((n, cols), F32),
        compiler_params=_params("arbitrary"),
        name="ada",
    )(c, w_ada, b_ada.reshape(1, cols))


def _rope(t, cos, s1, s2):
    parts = []
    for c in range(ATT_W // 128):
        tc = t[:, 128 * c:128 * (c + 1)]
        parts.append(tc * cos + pltpu.roll(tc, 128 - ROT_DIM // 2, 1) * s1 + pltpu.roll(tc, ROT_DIM // 2, 1) * s2)
    return jnp.concatenate(parts, axis=1)


def _inproj_kernel(x_ref, sc_ref, sh_ref, g_ref, w_ref, cos_ref, s1_ref, s2_ref, *out_refs,
                   transposed, p_t):
    q_ref, k_ref, v_ref, p_ref = out_refs[:4]
    rest = list(out_refs[4:])
    x = x_ref[...]
    ms = jnp.mean(x * x, axis=-1, keepdims=True)
    h = x * lax.rsqrt(ms + NORM_EPS) * g_ref[...]
    h = (h * (1.0 + sc_ref[...]) + sh_ref[...]).astype(BF16)
    cos, s1, s2 = cos_ref[...], s1_ref[...], s2_ref[...]
    q = _rope(_dot(h, w_ref[:, 0:ATT_W]), cos, s1, s2)
    q_ref[...] = (q * HEAD_DIM ** -0.5).astype(q_ref.dtype)
    k = _rope(_dot(h, w_ref[:, ATT_W:2 * ATT_W]), cos, s1, s2)
    v = _dot(h, w_ref[:, 2 * ATT_W:3 * ATT_W])
    k_ref[...] = k
    v_ref[...] = v
    p = _dot(h, w_ref[:, 3 * ATT_W:])
    p_ref[...] = p
    if transposed:
        rest.pop(0)[...] = k.T
        rest.pop(0)[...] = v.T
    if p_t:
        rest.pop(0)[...] = p.T


def _inproj(x, sc, sh, g, w_in_b, tabs, tm, mod_mode, transposed, nq=1, p_t=False):
    G, R, D = x.shape
    nin = w_in_b.shape[1]
    row_spec = lambda w: pl.BlockSpec((None, tm, w), lambda b, i: (b, i, 0))
    mod_spec = {"batch": pl.BlockSpec((None, 1, D), lambda b, i: (b, 0, 0)),
                "row": pl.BlockSpec((None, tm, D), lambda b, i: (b, i, 0)),
                "tile": pl.BlockSpec((None, tm, D), lambda b, i: (b, 0, 0))}[mod_mode]
    tab_spec = pl.BlockSpec((tm, 128), lambda b, i: (i, 0))
    const = lambda shape: pl.BlockSpec(shape, lambda b, i: (0, 0))
    out_specs = [row_spec(ATT_W)] * 3 + [row_spec(RWKV_IN)]
    out_shape = [jax.ShapeDtypeStruct((G, R, ATT_W), F32)] * 3 + [jax.ShapeDtypeStruct((G, R, RWKV_IN), F32)]
    if transposed:
        spq = (R // nq) // tm
        t_spec = pl.BlockSpec((None, None, ATT_W, tm), lambda b, i: (b, i // spq, 0, i % spq))
        out_specs += [t_spec] * 2
        out_shape += [jax.ShapeDtypeStruct((G, nq, ATT_W, R // nq), F32)] * 2
        if p_t:
            out_specs.append(pl.BlockSpec((None, None, RWKV_IN, tm), lambda b, i: (b, i // spq, 0, i % spq)))
            out_shape.append(jax.ShapeDtypeStruct((G, nq, RWKV_IN, R // nq), F32))
    return pl.pallas_call(
        functools.partial(_inproj_kernel, transposed=transposed, p_t=p_t),
        grid=(G, R // tm),
        in_specs=[row_spec(D), mod_spec, mod_spec, const((1, D)), const((D, nin)),
                  tab_spec, tab_spec, tab_spec],
        out_specs=out_specs,
        out_shape=out_shape,
        compiler_params=_params("arbitrary", "arbitrary"),
        name="inproj",
    )(x, sc, sh, g.reshape(1, D), w_in_b, *tabs)


def _rope_tables(pos):
    half = ROT_DIM // 2
    inv = ROPE_THETA ** (-jnp.arange(half, dtype=F32) * (2.0 / ROT_DIM))
    ang = pos.astype(F32)[:, None] * inv[None, :]
    cos, sin = jnp.cos(ang), jnp.sin(ang)
    n = pos.shape[0]
    ones = jnp.ones((n, HEAD_DIM - ROT_DIM), F32)
    zeros = jnp.zeros((n, HEAD_DIM - ROT_DIM), F32)
    zh = jnp.zeros((n, half), F32)
    ctab = jnp.concatenate([cos, cos, ones], axis=1)
    s1 = jnp.concatenate([-sin, zh, zeros], axis=1)
    s2 = jnp.concatenate([zh, sin, zeros], axis=1)
    return tuple(jnp.tile(t, (1, 128 // HEAD_DIM)) for t in (ctab, s1, s2))


def _attn_fused_kernel(q_ref, k_ref, v_ref, o_ref, acc_scr, m_scr, l_scr, *, dilations, group):
    nw, HD = ATT_BLOCK, HEAD_DIM
    S = q_ref.shape[0]
    n_tiles = S // nw
    lane = lax.broadcasted_iota(jnp.int32, (nw, 2 * HD), 1)
    first_head = lane < HD
    q_idx = lax.broadcasted_iota(jnp.int32, (nw, 2 * HD), 0)
    n_col = nw // HD
    key_idx = [HD * jc + (lane & (HD - 1)) for jc in range(n_col)]
    cur_ok = [q_idx >= key_idx[jc] for jc in range(n_col)]

    def bd(x):
        zero = jnp.zeros_like(x)
        xa, xb = jnp.where(first_head, x, zero), jnp.where(first_head, zero, x)
        parts = []
        for jc in range(n_col):
            parts += [xa[HD * jc:HD * (jc + 1)], xb[HD * jc:HD * (jc + 1)]]
        return jnp.concatenate(parts, axis=0)

    def half_reduce(x, op, fill):
        ra = op(jnp.where(first_head, x, fill), axis=-1, keepdims=True)
        rb = op(jnp.where(first_head, fill, x), axis=-1, keepdims=True)
        return jnp.where(first_head, ra, rb)

    def steps(tiles, use_prev, init, final):
        n = range(len(tiles))
        qs = [q_ref[tiles[t][0], :].astype(BF16) for t in n]
        cols, vbd = [], []
        for t in n:
            rows, prev_rows, has_prev = tiles[t]
            s = _dot_nt(qs[t], bd(k_ref[rows, :].astype(BF16)))
            c = [jnp.where(cur_ok[jc], s[:, 2 * HD * jc:2 * HD * (jc + 1)], NEG_BIG) for jc in range(n_col)]
            vb = [bd(v_ref[rows, :].astype(BF16))]
            if use_prev:
                sp = _dot_nt(qs[t], bd(k_ref[prev_rows, :].astype(BF16)))
                thresh = jnp.where(has_prev, 0, nw)
                c += [jnp.where((key_idx[jc] - q_idx) >= thresh, sp[:, 2 * HD * jc:2 * HD * (jc + 1)], NEG_BIG)
                      for jc in range(n_col)]
                vb.append(bd(v_ref[prev_rows, :].astype(BF16)))
            cols.append(c)
            vbd.append(vb)
        m_new, m_old, p, l_new = [], [], [], []
        for t in n:
            m = half_reduce(functools.reduce(jnp.maximum, cols[t]), jnp.max, NEG_BIG)
            if not init:
                m_old.append(m_scr[tiles[t][0], :])
                m = jnp.maximum(m, m_old[t])
            m_new.append(m)
            p.append([jnp.exp(c - m) for c in cols[t]])
            l_new.append(half_reduce(functools.reduce(jnp.add, p[t]), jnp.sum, 0.0))
        acc = []
        for t in n:
            a = _dot(jnp.concatenate(p[t][0:n_col], axis=1).astype(BF16), vbd[t][0])
            if use_prev:
                a = a + _dot(jnp.concatenate(p[t][n_col:], axis=1).astype(BF16), vbd[t][1])
            acc.append(a)
        for t in n:
            rows = tiles[t][0]
            a, l = acc[t], l_new[t]
            if not init:
                alpha = jnp.exp(m_old[t] - m_new[t])
                a = a + acc_scr[rows, :] * alpha
                l = l + l_scr[rows, :] * alpha
            if final:
                o_ref[rows, :] = (a / l).astype(o_ref.dtype)
            else:
                acc_scr[rows, :] = a
                m_scr[rows, :] = m_new[t]
                l_scr[rows, :] = l

    n_br = len(dilations)
    for bi, dil in enumerate(dilations):
        init, final = bi == 0, bi == n_br - 1
        use_prev = S // (dil * nw) > 1

        def group_body(g, carry, dil=dil, use_prev=use_prev, init=init, final=final):
            tiles = []
            for i in range(group):
                t = g * group + i
                r, j = lax.rem(t, dil), lax.div(t, dil)
                start = r + j * (dil * nw)
                prev = r + jnp.maximum(j - 1, 0) * (dil * nw)
                if dil > 1:
                    tiles.append((pl.ds(start, nw, stride=dil), pl.ds(prev, nw, stride=dil), j > 0))
                else:
                    tiles.append((pl.ds(pl.multiple_of(start, nw), nw), pl.ds(pl.multiple_of(prev, nw), nw), j > 0))
            steps(tiles, use_prev, init, final)
            return carry

        lax.fori_loop(0, n_tiles // group, group_body, 0)


def _attn_fused(q, k, v):
    B, S, W = q.shape
    nw = ATT_BLOCK
    dilations = tuple(sorted((d for _, d in DIL_BRANCHES), reverse=True))
    for win, d in DIL_BRANCHES:
        assert win == d * nw and S % (d * nw) == 0
    assert (S // nw) % ATT_GROUP == 0
    PW = 2 * HEAD_DIM
    spec = pl.BlockSpec((None, S, PW), lambda b, p: (b, 0, p))
    return pl.pallas_call(
        functools.partial(_attn_fused_kernel, dilations=dilations, group=ATT_GROUP),
        grid=(B, W // PW),
        in_specs=[spec, spec, spec],
        out_specs=spec,
        out_shape=jax.ShapeDtypeStruct((B, S, W), BF16),
        scratch_shapes=[pltpu.VMEM((S, PW), F32) for _ in range(3)],
        compiler_params=_params("arbitrary", "arbitrary"),
        name="attn_fused",
    )(q, k, v)


def _sample_counts(T, WB):
    i = np.arange(T)
    c = np.arange(WB)
    delta = WB + i[None, :] - c[:, None]
    cnt_c = np.zeros((WB, T), np.float32)
    diff = i[None, :] - i[:, None]
    cnt_n = np.zeros((T, T), np.float32)
    for win, dil in DIL_BRANCHES:
        cnt_c += ((delta > 0) & (delta % dil == 0) & (delta <= win)).astype(np.float32)
        cnt_n += ((diff >= 0) & (diff % dil == 0) & (diff <= win)).astype(np.float32)
    return cnt_c, cnt_n


def _sample_attn_kernel(q_ref, kn_ref, vn_ref, kt_ref, vt_ref, cc_ref, cn_ref, o_ref):
    cc, cn = cc_ref[...], cn_ref[...]
    outs = []
    for h in range(ATT_HEADS):
        sl = slice(HEAD_DIM * h, HEAD_DIM * (h + 1))
        qh = q_ref[:, sl]
        s_c = jnp.where(cc > 0, _dot(qh.astype(BF16), kt_ref[h].astype(BF16)), NEG_BIG)
        s_n = jnp.where(cn > 0, _dot_nt(qh, kn_ref[:, sl]), NEG_BIG)
        m = jnp.maximum(jnp.max(s_c, axis=-1, keepdims=True), jnp.max(s_n, axis=-1, keepdims=True))
        p_c = cc * jnp.exp(s_c - m)
        p_n = cn * jnp.exp(s_n - m)
        l = jnp.sum(p_c, axis=-1, keepdims=True) + jnp.sum(p_n, axis=-1, keepdims=True)
        acc = _dot_nt(p_c.astype(BF16), vt_ref[h].astype(BF16)) + _dot(p_n, vn_ref[:, sl])
        outs.append(acc / l)
    o_ref[...] = jnp.concatenate(outs, axis=1).astype(o_ref.dtype)


def _sample_attn(q, k_new, v_new, cache_kt, cache_vt):
    DB, T, W = q.shape
    WB = cache_kt.shape[-1]
    cnt_c, cnt_n = _sample_counts(T, WB)
    cc = jnp.asarray(cnt_c.T)
    cn = jnp.asarray(cnt_n.T)
    new_spec = pl.BlockSpec((None, T, W), lambda b: (b, 0, 0))
    cache_spec = pl.BlockSpec((None, ATT_HEADS, HEAD_DIM, WB), lambda b: (b, 0, 0, 0))
    const = lambda arr: pl.BlockSpec(arr.shape, lambda b: (0, 0))
    return pl.pallas_call(
        _sample_attn_kernel,
        grid=(DB,),
        in_specs=[new_spec, new_spec, new_spec, cache_spec, cache_spec, const(cc), const(cn)],
        out_specs=new_spec,
        out_shape=jax.ShapeDtypeStruct((DB, T, W), BF16),
        compiler_params=_params("arbitrary"),
        name="sample_attn",
    )(q, k_new, v_new, cache_kt, cache_vt, cc, cn)


def _rwkv_prologue(p_ref, shift_ref, prev_scr, shout_ref, mu_ref, w0_ref, w2_ref, a0_ref, a2_ref, g2_ref,
                   kk_ref, ka_ref, rk_ref, e):
    C = p_ref.shape[0]
    P = p_ref[...]
    row = lax.broadcasted_iota(jnp.int32, (C, 1), 0)
    p_prev = jnp.where(row == 0, prev_scr[...], pltpu.roll(P, 1, 0))
    pm = P + (p_prev - P) * mu_ref[...]
    prev_scr[...] = P[C - 1:C, :]
    shout_ref[...] = P[C - 1:C, :]

    W = RWKV_W
    r, k, v = pm[:, 0:W], pm[:, W:2 * W], pm[:, 2 * W:3 * W]
    lora_wa = pm[:, 3 * W:3 * W + DECAY_LORA + AAA_LORA]
    gl = pm[:, 3 * W + DECAY_LORA + AAA_LORA:]
    wpre = w0_ref[...] + _dot(jnp.tanh(lora_wa).astype(BF16), w2_ref[...])
    neg = -wpre
    softplus = jnp.maximum(neg, 0.0) + jnp.log(1.0 + jnp.exp(-jnp.abs(neg)))
    logw = -jnp.exp(-softplus - 0.5)
    a = jax.nn.sigmoid(a0_ref[...] + _dot(lora_wa.astype(BF16), a2_ref[...]))
    g = _dot(jax.nn.sigmoid(gl).astype(BF16), g2_ref[...])
    kkr = k * kk_ref[...]
    nrm = jnp.sqrt(_dot_exact_rhs(kkr * kkr, e))
    kk = kkr / jnp.maximum(nrm, 1e-12)
    kmod = k * (1.0 + (a - 1.0) * ka_ref[...])
    bonus = _dot_exact_rhs(r * kmod * rk_ref[...], e) * v

    ti = lax.broadcasted_iota(jnp.int32, (C, C), 0)
    si = lax.broadcasted_iota(jnp.int32, (C, C), 1)
    tri_b = jnp.where(ti >= si, 1.0, 0.0).astype(BF16)
    lw_h = logw.astype(BF16)
    lw_r = logw - lw_h.astype(F32)
    lw_m = lw_r.astype(BF16)
    lw_l = (lw_r - lw_m.astype(F32)).astype(BF16)
    L = _dot(tri_b, lw_h) + (_dot(tri_b, lw_m) + _dot(tri_b, lw_l))
    e_l = jnp.exp(L)
    e_nl = jnp.exp(-L)
    a_t = -kk * jnp.exp(L - logw)
    r_t = r * e_l
    b_t = kk * a * e_nl
    k_t = kmod * e_nl
    w_c = e_l[C - 1:C, :]
    return v, a_t, r_t, b_t, k_t, w_c, bonus, g


def _rwkv_epilogue(o, bonus, g, e, lg_ref, lb_ref, rw_ref):
    mean = _dot_exact_rhs(o, e) * (1.0 / HEAD_DIM)
    d = o - mean
    var = _dot_exact_rhs(d * d, e) * (1.0 / HEAD_DIM)
    on = d * lax.rsqrt(var + LNX_EPS) * lg_ref[...] + lb_ref[...]
    rw_ref[...] = ((on + bonus) * g).astype(rw_ref.dtype)


def _rwkv_pair_kernel(p_ref, shift_ref, s0_ref, mu_ref, w0_ref, w2_ref, a0_ref, a2_ref, g2_ref,
                      kk_ref, ka_ref, rk_ref, lg_ref, lb_ref, e_ref,
                      rw_ref, sout_ref, shout_ref, s_scr, prev_scr, *, n_double):
    C = p_ref.shape[0]
    n_pairs = RWKV_HEADS // 2
    PW = 2 * HEAD_DIM

    @pl.when(pl.program_id(1) == 0)
    def _():
        s_scr[...] = s0_ref[...]
        prev_scr[...] = shift_ref[...]

    e = e_ref[...]
    v, a_t, r_t, b_t, k_t, w_c, bonus, g = _rwkv_prologue(
        p_ref, shift_ref, prev_scr, shout_ref, mu_ref, w0_ref, w2_ref, a0_ref, a2_ref, g2_ref,
        kk_ref, ka_ref, rk_ref, e)

    lane = lax.broadcasted_iota(jnp.int32, (C, PW), 1)
    ti = lax.broadcasted_iota(jnp.int32, (C, PW), 0)
    si = lane & (HEAD_DIM - 1)
    first_head = lane < HEAD_DIM
    tri_incl = ti >= si
    tri_strict = ti > si
    eye = jnp.where(ti == si, 1.0, 0.0)
    rows2 = lax.broadcasted_iota(jnp.int32, (2 * C, PW), 0)
    lanes2 = lax.broadcasted_iota(jnp.int32, (2 * C, PW), 1)
    same_head = (rows2 < C) == (lanes2 < HEAD_DIM)

    def bd(x):
        zero = jnp.zeros_like(x)
        return jnp.concatenate([jnp.where(first_head, x, zero), jnp.where(first_head, zero, x)], axis=0)

    def dot3_bd(x, y):
        xh, xl = _split2(x)
        yh, yl = _split2(y)
        ybh = bd(yh)
        return (_dot(jnp.concatenate([xh, xl], axis=1), jnp.concatenate([ybh, ybh], axis=0))
                + _dot(xh, bd(yl)))

    pairs = range(n_pairs)
    sl = [slice(PW * p, PW * (p + 1)) for p in pairs]
    ar = [jnp.concatenate([a_t[:, sl[p]], r_t[:, sl[p]]], axis=0).astype(BF16) for p in pairs]
    bk = [jnp.concatenate([b_t[:, sl[p]], k_t[:, sl[p]]], axis=0).astype(BF16) for p in pairs]
    bk_bd = [jnp.concatenate([bd(bk[p][0:C]), bd(bk[p][C:2 * C])], axis=0) for p in pairs]
    gm = [_dot_nt(ar[p], bk_bd[p]) for p in pairs]
    m_ab = [jnp.where(tri_strict, gm[p][0:C, 0:PW], 0.0) for p in pairs]
    m_ak = [jnp.where(tri_strict, gm[p][0:C, PW:2 * PW], 0.0).astype(BF16) for p in pairs]
    n_r = [jnp.concatenate([jnp.where(tri_incl, gm[p][C:2 * C, 0:PW], 0.0),
                            jnp.where(tri_incl, gm[p][C:2 * C, PW:2 * PW], 0.0)], axis=1).astype(BF16)
           for p in pairs]
    t_inv = [eye + m_ab[p] for p in pairs]
    mp = m_ab
    for _ in range(n_double):
        mp = [dot3_bd(mp[p], mp[p]) for p in pairs]
        t_inv = [t_inv[p] + dot3_bd(t_inv[p], mp[p]) for p in pairs]
    v_bd = [bd(v[:, sl[p]].astype(BF16)) for p in pairs]
    mv = [_dot(m_ak[p], v_bd[p]) for p in pairs]
    s0 = [s_scr[p] for p in pairs]
    ars = [_dot_nt(ar[p], s0[p].astype(BF16)) for p in pairs]
    x = [ars[p][0:C] + mv[p] for p in pairs]
    u = [dot3_bd(t_inv[p], x[p]) for p in pairs]
    u_b = [u[p].astype(BF16) for p in pairs]
    o = [ars[p][C:2 * C] + _dot(n_r[p], jnp.concatenate([bd(u_b[p]), v_bd[p]], axis=0)) for p in pairs]
    for p in pairs:
        uv = jnp.concatenate([u_b[p], v[:, sl[p]].astype(BF16)], axis=0)
        upd = jnp.where(same_head, _dot_tn(uv, bk[p]), 0.0)
        s_new = (s0[p] + upd) * w_c[:, sl[p]]
        s_scr[p] = s_new
        sout_ref[p] = s_new
    _rwkv_epilogue(jnp.concatenate(o, axis=1), bonus, g, e, lg_ref, lb_ref, rw_ref)


def _rwkv(P, shift0, s0, lw, C):
    B, T, _ = P.shape
    assert T % C == 0 and C == HEAD_DIM, "the pair-packed chunk kernel needs chunk length == head dim"
    n_double = max(int(np.log2(C)) - 1, 0)
    W = RWKV_W
    e = np.zeros((W, W), np.float32)
    for h in range(RWKV_HEADS):
        e[h * HEAD_DIM:(h + 1) * HEAD_DIM, h * HEAD_DIM:(h + 1) * HEAD_DIM] = 1.0
    zpad = jnp.zeros((DECAY_LORA, W), F32)
    w2p = jnp.concatenate([lw["w2"], jnp.zeros((AAA_LORA, W), F32)], axis=0).astype(BF16)
    a2p = jnp.concatenate([zpad, lw["a2"]], axis=0).astype(BF16)
    row = lambda t: t.reshape(1, -1)
    consts = [row(lw["mu"]), row(lw["w0"]), w2p, row(lw["a0"]), a2p, lw["g2"].astype(BF16),
              row(lw["k_k"]), row(lw["k_a"]), row(lw["r_k"]), row(lw["lnx_g"]), row(lw["lnx_b"]),
              jnp.asarray(e, BF16)]
    sp = s0.reshape(B, RWKV_HEADS // 2, 2, HEAD_DIM, HEAD_DIM)
    z = jnp.zeros_like(sp[:, :, 0])
    s0 = jnp.concatenate([jnp.concatenate([sp[:, :, 0], z], axis=-1),
                          jnp.concatenate([z, sp[:, :, 1]], axis=-1)], axis=-2)
    st_shape = s0.shape[1:]
    const = lambda arr: pl.BlockSpec(arr.shape, lambda b, c: (0, 0))
    st_spec = pl.BlockSpec((None,) + st_shape, lambda b, c: (b, 0, 0, 0))
    sh_spec = pl.BlockSpec((None, 1, RWKV_IN), lambda b, c: (b, 0, 0))
    rw, s_out, sh_out = pl.pallas_call(
        functools.partial(_rwkv_pair_kernel, n_double=n_double),
        grid=(B, T // C),
        in_specs=[pl.BlockSpec((None, C, RWKV_IN), lambda b, c: (b, c, 0)), sh_spec, st_spec]
                 + [const(t) for t in consts],
        out_specs=[pl.BlockSpec((None, C, W), lambda b, c: (b, c, 0)), st_spec, sh_spec],
        out_shape=[jax.ShapeDtypeStruct((B, T, W), BF16),
                   jax.ShapeDtypeStruct((B,) + st_shape, F32),
                   jax.ShapeDtypeStruct((B, 1, RWKV_IN), F32)],
        scratch_shapes=[pltpu.VMEM(st_shape, F32),
                        pltpu.VMEM((1, RWKV_IN), F32)],
        compiler_params=_params("arbitrary", "arbitrary"),
        name=f"rwkv_c{C}",
    )(P, shift0.reshape(B, 1, RWKV_IN), s0, *consts)
    s_out = jnp.stack([s_out[:, :, :HEAD_DIM, :HEAD_DIM], s_out[:, :, HEAD_DIM:, HEAD_DIM:]], axis=2)
    s_out = s_out.reshape(B, RWKV_HEADS, HEAD_DIM, HEAD_DIM)
    return rw, s_out, sh_out.reshape(B, RWKV_IN)


def _rwkv_scan_kernel(r_ref, k_ref, v_ref, lo_ref, sr_ref, sk_ref, sv_ref, slo_ref, s_ref,
                      mur_ref, muk_ref, muv_ref, mulo_ref, w0_ref, w2t_ref, a0_ref, a2t_ref, g2t_ref,
                      kk_ref, ka_ref, rk_ref, lg_ref, lb_ref,
                      rw_ref, sout_ref,
                      w_scr, nkk_scr, bb_scr, km_scr, rr_scr, vv_scr, bonus_scr, g_scr, o_scr):
    T = r_ref.shape[0]

    def shifted(ref, s0_ref, mu_ref, t):
        cur = ref[t]
        prev = s0_ref[...] if t == 0 else ref[t - 1]
        return cur + (prev - cur) * mu_ref[...]

    for t in range(T):
        r = shifted(r_ref, sr_ref, mur_ref, t)
        k = shifted(k_ref, sk_ref, muk_ref, t)
        v = shifted(v_ref, sv_ref, muv_ref, t)
        lo = shifted(lo_ref, slo_ref, mulo_ref, t)
        wl, al, gl = lo[0:DECAY_LORA], lo[DECAY_LORA:DECAY_LORA + AAA_LORA], lo[DECAY_LORA + AAA_LORA:]
        neg = -(w0_ref[...] + _dot(w2t_ref[...], jnp.tanh(wl).astype(BF16)))
        softplus = jnp.maximum(neg, 0.0) + jnp.log(1.0 + jnp.exp(-jnp.abs(neg)))
        w = jnp.exp(-jnp.exp(-softplus - 0.5))
        a = jax.nn.sigmoid(a0_ref[...] + _dot(a2t_ref[...], al.astype(BF16)))
        g_scr[t] = _dot(g2t_ref[...], jax.nn.sigmoid(gl).astype(BF16))
        kkr = k * kk_ref[...]
        kk = kkr / jnp.maximum(jnp.sqrt(jnp.sum(kkr * kkr, axis=0, keepdims=True)), 1e-12)
        kmod = k * (1.0 + (a - 1.0) * ka_ref[...])
        bonus_scr[t] = jnp.sum(r * kmod * rk_ref[...], axis=0, keepdims=True) * v
        w_scr[t] = w
        nkk_scr[t] = -kk
        bb_scr[t] = kk * a
        km_scr[t] = kmod
        rr_scr[t] = r
        vv_scr[t] = v

    def row(i, carry):
        s = s_ref[i]
        for t in range(T):
            sa = jnp.sum(s * nkk_scr[t], axis=0, keepdims=True)
            s = s * w_scr[t] + sa * bb_scr[t] + vv_scr[t, pl.ds(i, 1), :] * km_scr[t]
            o_scr[t, pl.ds(i, 1), :] = jnp.sum(s * rr_scr[t], axis=0, keepdims=True)
        sout_ref[i] = s
        return carry

    lax.fori_loop(0, HEAD_DIM, row, 0)
    o = o_scr[...]
    mean = jnp.mean(o, axis=1, keepdims=True)
    d = o - mean
    var = jnp.mean(d * d, axis=1, keepdims=True)
    on = d * lax.rsqrt(var + LNX_EPS) * lg_ref[...] + lb_ref[...]
    rw_ref[...] = ((on + bonus_scr[...]) * g_scr[...]).astype(rw_ref.dtype)


def _rwkv_scan(pt, shift0, s0t, lw):
    T, _, DB = pt.shape
    W, HD = RWKV_W, HEAD_DIM
    lora_w = DECAY_LORA + AAA_LORA + GATE_LORA
    assert (3 * W) % lora_w == 0
    col = lambda t: jnp.broadcast_to(t.reshape(-1, 1), (t.size, DB))
    shift_t = shift0.T
    mu_c = col(lw["mu"])
    head = lambda base: (lambda h: (base + h, 0))
    pt_blk = lambda base: pl.BlockSpec((T, HD, DB), lambda h: (0, base + h, 0))
    blk2 = lambda base: pl.BlockSpec((HD, DB), head(base))
    lo_pt = pl.BlockSpec((T, lora_w, DB), lambda h: (0, 3 * W // lora_w, 0))
    lo_2d = pl.BlockSpec((lora_w, DB), lambda h: (3 * W // lora_w, 0))
    nh = W // HD
    st_spec = pl.BlockSpec((None, HD, HD, DB), lambda h: (h, 0, 0, 0))
    wt_spec = lambda n: pl.BlockSpec((HD, n), lambda h: (h, 0))
    scr = lambda: pltpu.VMEM((T, HD, DB), F32)
    return pl.pallas_call(
        _rwkv_scan_kernel,
        grid=(nh,),
        in_specs=[pt_blk(0), pt_blk(nh), pt_blk(2 * nh), lo_pt,
                  blk2(0), blk2(nh), blk2(2 * nh), lo_2d, st_spec,
                  blk2(0), blk2(nh), blk2(2 * nh), lo_2d,
                  blk2(0), wt_spec(DECAY_LORA), blk2(0), wt_spec(AAA_LORA), wt_spec(GATE_LORA),
                  blk2(0), blk2(0), blk2(0), blk2(0), blk2(0)],
        out_specs=[pl.BlockSpec((T, HD, DB), lambda h: (0, h, 0)), st_spec],
        out_shape=[jax.ShapeDtypeStruct((T, W, DB), F32), jax.ShapeDtypeStruct(s0t.shape, F32)],
        scratch_shapes=[scr() for _ in range(9)],
        compiler_params=_params("arbitrary"),
        name="rwkv_scan",
    )(pt, pt, pt, pt, shift_t, shift_t, shift_t, shift_t, s0t,
      mu_c, mu_c, mu_c, mu_c, col(lw["w0"]), lw["w2"].T.astype(BF16), col(lw["a0"]), lw["a2"].T.astype(BF16),
      lw["g2"].T.astype(BF16), col(lw["k_k"]), col(lw["k_a"]), col(lw["r_k"]), col(lw["lnx_g"]), col(lw["lnx_b"]))


def _mlp_kernel(att_ref, rw_ref, x_ref, gt1_ref, sc2_ref, sh2_ref, gt2_ref, g2_ref, gf_ref,
                wo_ref, w1_ref, w2_ref, y_ref, *, final_norm, ff_chunk):
    half = wo_ref.shape[0] // 2
    mix = _dot(att_ref[...], wo_ref[0:half, :]) + _dot(rw_ref[...], wo_ref[half:, :])
    x1 = x_ref[...] + gt1_ref[...] * mix
    ms = jnp.mean(x1 * x1, axis=-1, keepdims=True)
    h2 = x1 * lax.rsqrt(ms + NORM_EPS) * g2_ref[...]
    h2 = (h2 * (1.0 + sc2_ref[...]) + sh2_ref[...]).astype(BF16)
    acc = jnp.zeros_like(x1)
    for c in range(w1_ref.shape[1] // ff_chunk):
        hid = _dot(h2, w1_ref[:, c * ff_chunk:(c + 1) * ff_chunk])
        hid = jnp.square(jnp.maximum(hid, 0.0)).astype(BF16)
        acc = acc + _dot(hid, w2_ref[c * ff_chunk:(c + 1) * ff_chunk, :])
    x2 = x1 + gt2_ref[...] * acc
    if final_norm:
        ms2 = jnp.mean(x2 * x2, axis=-1, keepdims=True)
        x2 = x2 * lax.rsqrt(ms2 + NORM_EPS) * gf_ref[...]
    y_ref[...] = x2


def _mlp(att, rw, x, gt1, sc2, sh2, gt2, g2, gf, wo_b, w1_b, w2_b, tm, per_row_mod, final_norm):
    G, R, D = x.shape
    dff = w1_b.shape[1]
    row_spec = lambda w: pl.BlockSpec((None, tm, w), lambda b, i: (b, i, 0))
    if per_row_mod:
        mod_spec = pl.BlockSpec((None, tm, D), lambda b, i: (b, i, 0))
    else:
        mod_spec = pl.BlockSpec((None, 1, D), lambda b, i: (b, 0, 0))
    const = lambda shape: pl.BlockSpec(shape, lambda b, i: (0, 0), pipeline_mode=pl.Buffered(1))
    return pl.pallas_call(
        functools.partial(_mlp_kernel, final_norm=final_norm, ff_chunk=1024),
        grid=(G, R // tm),
        in_specs=[row_spec(att.shape[-1]), row_spec(rw.shape[-1]), row_spec(D),
                  mod_spec, mod_spec, mod_spec, mod_spec, const((1, D)), const((1, D)),
                  const(wo_b.shape), const((D, dff)), const((dff, D))],
        out_specs=row_spec(D),
        out_shape=jax.ShapeDtypeStruct((G, R, D), F32),
        compiler_params=_params("arbitrary", "arbitrary"),
        name="mlp",
    )(att, rw, x, gt1, sc2, sh2, gt2, g2.reshape(1, D), gf.reshape(1, D), wo_b, w1_b, w2_b)


def kernel(x_prompt, x_sample, cache_k, cache_v, state_wkv, state_shift, c_prompt, c_sample, norm1_g, norm2_g, w_ada, b_ada, w_in, w_out, mu, w0, w2, a0, a2, g2, k_k, k_a, r_k, lnx_g, lnx_b, w_ff1, w_ff2, normf_g):
    B, S, D = x_prompt.shape
    DB, T, _ = x_sample.shape
    depth = w_in.shape[0]
    dt = x_prompt.dtype
    tabs_p = _rope_tables(jnp.arange(S))
    tabs_s = tuple(jnp.tile(t, (DB, 1)) for t in _rope_tables(PAST_LEN + jnp.arange(T)))
    tabs_t = tuple(jnp.repeat(t, DB, axis=0) for t in _rope_tables(PAST_LEN + jnp.arange(T)))
    c_all = jnp.concatenate([c_prompt, c_sample], axis=0)
    hp = x_prompt
    hs = x_sample.reshape(1, DB * T, D)
    outs = [[] for _ in range(8)]
    tm_s = min(ROW_TILE, DB * T)
    for l in range(depth):
        last = l == depth - 1
        mod = _ada(c_all, w_ada[l], b_ada[l])
        mod_p = [m.reshape(B, 1, D) for m in jnp.split(mod[:B], 6, axis=-1)]
        mod_s = [jnp.repeat(m, T, axis=0).reshape(1, DB * T, D) for m in jnp.split(mod[B:], 6, axis=-1)]
        mod_t = [m.reshape(1, DB, D) for m in jnp.split(mod[B:], 6, axis=-1)]
        w_in_b, wo_b = w_in[l].astype(BF16), w_out[l].astype(BF16)
        w1_b, w2_b = w_ff1[l].astype(BF16), w_ff2[l].astype(BF16)
        lw = dict(mu=mu[l], w0=w0[l], w2=w2[l], a0=a0[l], a2=a2[l], g2=g2[l], k_k=k_k[l], k_a=k_a[l],
                  r_k=r_k[l], lnx_g=lnx_g[l], lnx_b=lnx_b[l])

        sh1, sc1, gt1, sh2, sc2, gt2 = mod_p
        q, k, v, P, kt, vt = _inproj(hp, sc1, sh1, norm1_g[l], w_in_b, tabs_p, ROW_TILE, "batch", True)
        att = _attn_fused(q, k, v)
        rw, s_p, shift_p = _rwkv(P, jnp.zeros((B, RWKV_IN), F32),
                                 jnp.zeros((B, RWKV_HEADS, HEAD_DIM, HEAD_DIM), F32), lw, RWKV_CHUNK)
        hp = _mlp(att, rw, hp, gt1, sc2, sh2, gt2, norm2_g[l], normf_g, wo_b, w1_b, w2_b, ROW_TILE, False, last)
        keep = min(max(w for w, _ in DIL_BRANCHES), S)
        win = lambda t: jnp.transpose(t.reshape(B, ATT_HEADS, HEAD_DIM, S)[..., S - keep:], (0, 3, 1, 2))
        outs[0].append(win(kt))
        outs[1].append(win(vt))
        outs[2].append(s_p)
        outs[3].append(shift_p)

        sh1, sc1, gt1, sh2, sc2, gt2 = mod_s
        q, k, v, P = _inproj(hs, sc1, sh1, norm1_g[l], w_in_b, tabs_s, tm_s, "row", False)
        hs_t = jnp.transpose(hs.reshape(DB, T, D), (1, 0, 2)).reshape(1, T * DB, D)
        *_, kt, vt, pt = _inproj(hs_t, mod_t[1], mod_t[0], norm1_g[l], w_in_b, tabs_t, DB, "tile", True, nq=T, p_t=True)
        att = _sample_attn(q.reshape(DB, T, ATT_W), k.reshape(DB, T, ATT_W), v.reshape(DB, T, ATT_W),
                           jnp.transpose(cache_k[l], (0, 2, 3, 1)), jnp.transpose(cache_v[l], (0, 2, 3, 1)))
        rw_t, s_t = _rwkv_scan(pt.reshape(T, RWKV_IN, DB), state_shift[l],
                               jnp.transpose(state_wkv[l], (1, 2, 3, 0)), lw)
        rw = jnp.transpose(rw_t, (2, 0, 1)).astype(BF16)
        s_s = jnp.transpose(s_t, (3, 0, 1, 2))
        shift_s = P.reshape(DB, T, RWKV_IN)[:, T - 1]
        hs = _mlp(att.reshape(1, DB * T, ATT_W), rw.reshape(1, DB * T, RWKV_W), hs, gt1, sc2, sh2, gt2,
                  norm2_g[l], normf_g, wo_b, w1_b, w2_b, tm_s, True, last)
        new = lambda t: jnp.transpose(t.reshape(T, ATT_HEADS, HEAD_DIM, DB), (3, 0, 1, 2))
        outs[4].append(new(kt))
        outs[5].append(new(vt))
        outs[6].append(s_s)
        outs[7].append(shift_s)
    stack = lambda xs: jnp.stack(xs).astype(dt)
    return (hp.astype(dt), hs.reshape(DB, T, D).astype(x_sample.dtype), *[stack(o) for o in outs])
```

```python
import functools

import numpy as np
import jax
import jax.numpy as jnp
from jax import lax
from jax.experimental import pallas as pl
from jax.experimental.pallas import tpu as pltpu

F32 = jnp.float32
BF16 = jnp.bfloat16

HEAD_DIM = 64
ATT_HEADS = 8
ATT_W = ATT_HEADS * HEAD_DIM
RWKV_HEADS = 8
RWKV_W = RWKV_HEADS * HEAD_DIM
DIL_BRANCHES = ((128, 1), (512, 4), (2048, 16))
ROT_DIM = HEAD_DIM // 4
ROPE_THETA = 500000.0
DECAY_LORA = 64
AAA_LORA = 64
GATE_LORA = 128
RWKV_IN = 3 * RWKV_W + DECAY_LORA + AAA_LORA + GATE_LORA
NORM_EPS = 1e-6
LNX_EPS = 64e-5
PAST_LEN = 8192
ATT_BLOCK = 128
NEG_BIG = -1e30
VMEM_LIMIT = 56 * 1024 * 1024
ROW_TILE = 512
RWKV_CHUNK = 64
RWKV_CHUNKS_PER_STEP = 4
ATT_GROUP = 4


def _dot(a, b):
    return jnp.dot(a, b, preferred_element_type=F32)


def _dot_nt(a, b):
    return lax.dot_general(a, b, (((1,), (1,)), ((), ())), preferred_element_type=F32)


def _dot_tn(a, b):
    return lax.dot_general(a, b, (((0,), (0,)), ((), ())), preferred_element_type=F32)


def _split2(x):
    hi = x.astype(BF16)
    lo = (x - hi.astype(F32)).astype(BF16)
    return hi, lo


def _head_sums(x, e):
    C, W = x.shape
    n = W // e.shape[0]
    xh, xl = _split2(x)
    tiles = [t[:, e.shape[0] * p:e.shape[0] * (p + 1)] for t in (xh, xl) for p in range(n)]
    o = _dot(jnp.concatenate(tiles, axis=0), e)
    return jnp.concatenate([o[C * p:C * (p + 1)] + o[C * (n + p):C * (n + p + 1)] for p in range(n)], axis=1)


def _params(*sem):
    return pltpu.CompilerParams(dimension_semantics=sem, vmem_limit_bytes=VMEM_LIMIT)


def _ada_kernel(c_ref, w_ref, b_ref, o_ref):
    c = c_ref[...]
    s = c * jax.nn.sigmoid(c)
    o_ref[...] = _dot(s.astype(BF16), w_ref[...].astype(BF16)) + b_ref[...]


def _ada(c, w_ada, b_ada):
    n, d = c.shape
    cols = w_ada.shape[1]
    tn = 512
    return pl.pallas_call(
        _ada_kernel,
        grid=(cols // tn,),
        in_specs=[pl.BlockSpec((n, d), lambda j: (0, 0)),
                  pl.BlockSpec((d, tn), lambda j: (0, j)),
                  pl.BlockSpec((1, tn), lambda j: (0, j))],
        out_specs=pl.BlockSpec((n, tn), lambda j: (0, j)),
        out_shape=jax.ShapeDtypeStruct((n, cols), F32),
        compiler_params=_params("arbitrary"),
        name="ada",
    )(c, w_ada, b_ada.reshape(1, cols))


def _rope(t, cos, s1, s2):
    parts = []
    for c in range(ATT_W // 128):
        tc = t[:, 128 * c:128 * (c + 1)]
        parts.append(tc * cos + pltpu.roll(tc, 128 - ROT_DIM // 2, 1) * s1 + pltpu.roll(tc, ROT_DIM // 2, 1) * s2)
    return jnp.concatenate(parts, axis=1)


def _inproj_kernel(x_ref, sc_ref, sh_ref, g_ref, w_ref, cos_ref, s1_ref, s2_ref, *out_refs,
                   transposed, p_t):
    q_ref, k_ref, v_ref, p_ref = out_refs[:4]
    rest = list(out_refs[4:])
    x = x_ref[...]
    ms = jnp.mean(x * x, axis=-1, keepdims=True)
    h = x * lax.rsqrt(ms + NORM_EPS) * g_ref[...]
    h = (h * (1.0 + sc_ref[...]) + sh_ref[...]).astype(BF16)
    cos, s1, s2 = cos_ref[...], s1_ref[...], s2_ref[...]
    q = _rope(_dot(h, w_ref[:, 0:ATT_W]), cos, s1, s2)
    q_ref[...] = (q * HEAD_DIM ** -0.5).astype(q_ref.dtype)
    k = _rope(_dot(h, w_ref[:, ATT_W:2 * ATT_W]), cos, s1, s2)
    v = _dot(h, w_ref[:, 2 * ATT_W:3 * ATT_W])
    k_ref[...] = k
    v_ref[...] = v
    p = _dot(h, w_ref[:, 3 * ATT_W:])
    p_ref[...] = p
    if transposed:
        rest.pop(0)[...] = k.T
        rest.pop(0)[...] = v.T
    if p_t:
        rest.pop(0)[...] = p.T


def _inproj(x, sc, sh, g, w_in_b, tabs, tm, mod_mode, transposed, nq=1, p_t=False):
    G, R, D = x.shape
    nin = w_in_b.shape[1]
    row_spec = lambda w: pl.BlockSpec((None, tm, w), lambda b, i: (b, i, 0))
    mod_spec = {"batch": pl.BlockSpec((None, 1, D), lambda b, i: (b, 0, 0)),
                "row": pl.BlockSpec((None, tm, D), lambda b, i: (b, i, 0)),
                "tile": pl.BlockSpec((None, tm, D), lambda b, i: (b, 0, 0))}[mod_mode]
    tab_spec = pl.BlockSpec((tm, 128), lambda b, i: (i, 0))
    const = lambda shape: pl.BlockSpec(shape, lambda b, i: (0, 0))
    out_specs = [row_spec(ATT_W)] * 3 + [row_spec(RWKV_IN)]
    out_shape = [jax.ShapeDtypeStruct((G, R, ATT_W), F32)] * 3 + [jax.ShapeDtypeStruct((G, R, RWKV_IN), F32)]
    if transposed:
        spq = (R // nq) // tm
        t_spec = pl.BlockSpec((None, None, ATT_W, tm), lambda b, i: (b, i // spq, 0, i % spq))
        out_specs += [t_spec] * 2
        out_shape += [jax.ShapeDtypeStruct((G, nq, ATT_W, R // nq), F32)] * 2
        if p_t:
            out_specs.append(pl.BlockSpec((None, None, RWKV_IN, tm), lambda b, i: (b, i // spq, 0, i % spq)))
            out_shape.append(jax.ShapeDtypeStruct((G, nq, RWKV_IN, R // nq), F32))
    return pl.pallas_call(
        functools.partial(_inproj_kernel, transposed=transposed, p_t=p_t),
        grid=(G, R // tm),
        in_specs=[row_spec(D), mod_spec, mod_spec, const((1, D)), const((D, nin)),
                  tab_spec, tab_spec, tab_spec],
        out_specs=out_specs,
        out_shape=out_shape,
        compiler_params=_params("arbitrary", "arbitrary"),
        name="inproj",
    )(x, sc, sh, g.reshape(1, D), w_in_b, *tabs)


def _rope_tables(pos):
    half = ROT_DIM // 2
    inv = ROPE_THETA ** (-jnp.arange(half, dtype=F32) * (2.0 / ROT_DIM))
    ang = pos.astype(F32)[:, None] * inv[None, :]
    cos, sin = jnp.cos(ang), jnp.sin(ang)
    n = pos.shape[0]
    ones = jnp.ones((n, HEAD_DIM - ROT_DIM), F32)
    zeros = jnp.zeros((n, HEAD_DIM - ROT_DIM), F32)
    zh = jnp.zeros((n, half), F32)
    ctab = jnp.concatenate([cos, cos, ones], axis=1)
    s1 = jnp.concatenate([-sin, zh, zeros], axis=1)
    s2 = jnp.concatenate([zh, sin, zeros], axis=1)
    return tuple(jnp.tile(t, (1, 128 // HEAD_DIM)) for t in (ctab, s1, s2))


def _attn_fused_kernel(q_ref, k_ref, v_ref, o_ref, acc_scr, m_scr, l_scr, *, dilations, group):
    nw, HD = ATT_BLOCK, HEAD_DIM
    S = q_ref.shape[0]
    n_tiles = S // nw
    lane = lax.broadcasted_iota(jnp.int32, (nw, 2 * HD), 1)
    first_head = lane < HD
    q_idx = lax.broadcasted_iota(jnp.int32, (nw, 2 * HD), 0)
    n_col = nw // HD
    key_idx = [HD * jc + (lane & (HD - 1)) for jc in range(n_col)]
    cur_ok = [q_idx >= key_idx[jc] for jc in range(n_col)]

    def bd(x):
        zero = jnp.zeros_like(x)
        xa, xb = jnp.where(first_head, x, zero), jnp.where(first_head, zero, x)
        parts = []
        for jc in range(n_col):
            parts += [xa[HD * jc:HD * (jc + 1)], xb[HD * jc:HD * (jc + 1)]]
        return jnp.concatenate(parts, axis=0)

    def half_reduce(x, op, fill):
        ra = op(jnp.where(first_head, x, fill), axis=-1, keepdims=True)
        rb = op(jnp.where(first_head, fill, x), axis=-1, keepdims=True)
        return jnp.where(first_head, ra, rb)

    def steps(tiles, use_prev, init, final):
        n = range(len(tiles))
        qs = [q_ref[tiles[t][0], :].astype(BF16) for t in n]
        cols, vbd = [], []
        for t in n:
            rows, prev_rows, has_prev = tiles[t]
            s = _dot_nt(qs[t], bd(k_ref[rows, :].astype(BF16)))
            c = [jnp.where(cur_ok[jc], s[:, 2 * HD * jc:2 * HD * (jc + 1)], NEG_BIG) for jc in range(n_col)]
            vb = [bd(v_ref[rows, :].astype(BF16))]
            if use_prev:
                sp = _dot_nt(qs[t], bd(k_ref[prev_rows, :].astype(BF16)))
                thresh = jnp.where(has_prev, 0, nw)
                c += [jnp.where((key_idx[jc] - q_idx) >= thresh, sp[:, 2 * HD * jc:2 * HD * (jc + 1)], NEG_BIG)
                      for jc in range(n_col)]
                vb.append(bd(v_ref[prev_rows, :].astype(BF16)))
            cols.append(c)
            vbd.append(vb)
        m_new, m_old, p, l_new = [], [], [], []
        for t in n:
            m = half_reduce(functools.reduce(jnp.maximum, cols[t]), jnp.max, NEG_BIG)
            if not init:
                m_old.append(m_scr[tiles[t][0], :])
                m = jnp.maximum(m, m_old[t])
            m_new.append(m)
            p.append([jnp.exp(c - m) for c in cols[t]])
            l_new.append(half_reduce(functools.reduce(jnp.add, p[t]), jnp.sum, 0.0))
        acc = []
        for t in n:
            a = _dot(jnp.concatenate(p[t][0:n_col], axis=1).astype(BF16), vbd[t][0])
            if use_prev:
                a = a + _dot(jnp.concatenate(p[t][n_col:], axis=1).astype(BF16), vbd[t][1])
            acc.append(a)
        for t in n:
            rows = tiles[t][0]
            a, l = acc[t], l_new[t]
            if not init:
                alpha = jnp.exp(m_old[t] - m_new[t])
                a = a + acc_scr[rows, :] * alpha
                l = l + l_scr[rows, :] * alpha
            if final:
                o_ref[rows, :] = (a / l).astype(o_ref.dtype)
            else:
                acc_scr[rows, :] = a
                m_scr[rows, :] = m_new[t]
                l_scr[rows, :] = l

    n_br = len(dilations)
    for bi, dil in enumerate(dilations):
        init, final = bi == 0, bi == n_br - 1
        use_prev = S // (dil * nw) > 1

        def group_body(g, carry, dil=dil, use_prev=use_prev, init=init, final=final):
            tiles = []
            for i in range(group):
                t = g * group + i
                r, j = lax.rem(t, dil), lax.div(t, dil)
                start = r + j * (dil * nw)
                prev = r + jnp.maximum(j - 1, 0) * (dil * nw)
                if dil > 1:
                    tiles.append((pl.ds(start, nw, stride=dil), pl.ds(prev, nw, stride=dil), j > 0))
                else:
                    tiles.append((pl.ds(pl.multiple_of(start, nw), nw), pl.ds(pl.multiple_of(prev, nw), nw), j > 0))
            steps(tiles, use_prev, init, final)
            return carry

        lax.fori_loop(0, n_tiles // group, group_body, 0)


def _attn_fused(q, k, v):
    B, S, W = q.shape
    nw = ATT_BLOCK
    dilations = tuple(sorted((d for _, d in DIL_BRANCHES), reverse=True))
    for win, d in DIL_BRANCHES:
        assert win == d * nw and S % (d * nw) == 0
    assert (S // nw) % ATT_GROUP == 0
    PW = 2 * HEAD_DIM
    spec = pl.BlockSpec((None, S, PW), lambda b, p: (b, 0, p))
    return pl.pallas_call(
        functools.partial(_attn_fused_kernel, dilations=dilations, group=ATT_GROUP),
        grid=(B, W // PW),
        in_specs=[spec, spec, spec],
        out_specs=spec,
        out_shape=jax.ShapeDtypeStruct((B, S, W), BF16),
        scratch_shapes=[pltpu.VMEM((S, PW), F32) for _ in range(3)],
        compiler_params=_params("arbitrary", "arbitrary"),
        name="attn_fused",
    )(q, k, v)


def _sample_counts(T, WB):
    i = np.arange(T)
    c = np.arange(WB)
    delta = WB + i[None, :] - c[:, None]
    cnt_c = np.zeros((WB, T), np.float32)
    diff = i[None, :] - i[:, None]
    cnt_n = np.zeros((T, T), np.float32)
    for win, dil in DIL_BRANCHES:
        cnt_c += ((delta > 0) & (delta % dil == 0) & (delta <= win)).astype(np.float32)
        cnt_n += ((diff >= 0) & (diff % dil == 0) & (diff <= win)).astype(np.float32)
    return cnt_c, cnt_n


def _sample_attn_kernel(q_ref, kn_ref, vn_ref, kt_ref, vt_ref, cc_ref, cn_ref, o_ref):
    cc, cn = cc_ref[...], cn_ref[...]
    outs = []
    for h in range(ATT_HEADS):
        sl = slice(HEAD_DIM * h, HEAD_DIM * (h + 1))
        qh = q_ref[:, sl]
        s_c = jnp.where(cc > 0, _dot(qh.astype(BF16), kt_ref[h].astype(BF16)), NEG_BIG)
        s_n = jnp.where(cn > 0, _dot_nt(qh, kn_ref[:, sl]), NEG_BIG)
        m = jnp.maximum(jnp.max(s_c, axis=-1, keepdims=True), jnp.max(s_n, axis=-1, keepdims=True))
        p_c = cc * jnp.exp(s_c - m)
        p_n = cn * jnp.exp(s_n - m)
        l = jnp.sum(p_c, axis=-1, keepdims=True) + jnp.sum(p_n, axis=-1, keepdims=True)
        acc = _dot_nt(p_c.astype(BF16), vt_ref[h].astype(BF16)) + _dot(p_n, vn_ref[:, sl])
        outs.append(acc / l)
    o_ref[...] = jnp.concatenate(outs, axis=1).astype(o_ref.dtype)


def _sample_attn(q, k_new, v_new, cache_kt, cache_vt):
    DB, T, W = q.shape
    WB = cache_kt.shape[-1]
    cnt_c, cnt_n = _sample_counts(T, WB)
    cc = jnp.asarray(cnt_c.T)
    cn = jnp.asarray(cnt_n.T)
    new_spec = pl.BlockSpec((None, T, W), lambda b: (b, 0, 0))
    cache_spec = pl.BlockSpec((None, ATT_HEADS, HEAD_DIM, WB), lambda b: (b, 0, 0, 0))
    const = lambda arr: pl.BlockSpec(arr.shape, lambda b: (0, 0))
    return pl.pallas_call(
        _sample_attn_kernel,
        grid=(DB,),
        in_specs=[new_spec, new_spec, new_spec, cache_spec, cache_spec, const(cc), const(cn)],
        out_specs=new_spec,
        out_shape=jax.ShapeDtypeStruct((DB, T, W), BF16),
        compiler_params=_params("arbitrary"),
        name="sample_attn",
    )(q, k_new, v_new, cache_kt, cache_vt, cc, cn)


def _rwkv_prologue(p_ref, shift_ref, prev_scr, shout_ref, mu_ref, w0_ref, w2_ref, a0_ref, a2_ref, g2_ref,
                   kk_ref, ka_ref, rk_ref, e, chunk):
    C = p_ref.shape[0]
    P = p_ref[...]
    row = lax.broadcasted_iota(jnp.int32, (C, 1), 0)
    p_prev = jnp.where(row == 0, prev_scr[...], pltpu.roll(P, 1, 0))
    pm = P + (p_prev - P) * mu_ref[...]
    prev_scr[...] = P[C - 1:C, :]
    shout_ref[...] = P[C - 1:C, :]

    W = RWKV_W
    r, k, v = pm[:, 0:W], pm[:, W:2 * W], pm[:, 2 * W:3 * W]
    lora_wa = pm[:, 3 * W:3 * W + DECAY_LORA + AAA_LORA]
    gl = pm[:, 3 * W + DECAY_LORA + AAA_LORA:]
    wpre = w0_ref[...] + _dot(jnp.tanh(lora_wa).astype(BF16), w2_ref[...])
    neg = -wpre
    softplus = jnp.maximum(neg, 0.0) + jnp.log(1.0 + jnp.exp(-jnp.abs(neg)))
    logw = -jnp.exp(-softplus - 0.5)
    a = jax.nn.sigmoid(a0_ref[...] + _dot(lora_wa.astype(BF16), a2_ref[...]))
    g = _dot(jax.nn.sigmoid(gl).astype(BF16), g2_ref[...])
    kkr = k * kk_ref[...]
    nrm = jnp.sqrt(_head_sums(kkr * kkr, e))
    kk = kkr / jnp.maximum(nrm, 1e-12)
    kmod = k * (1.0 + (a - 1.0) * ka_ref[...])
    bonus = _head_sums(r * kmod * rk_ref[...], e) * v

    ti = lax.broadcasted_iota(jnp.int32, (C, C), 0)
    si = lax.broadcasted_iota(jnp.int32, (C, C), 1)
    tri_b = jnp.where((ti >= si) & (lax.div(ti, chunk) == lax.div(si, chunk)), 1.0, 0.0).astype(BF16)
    lw_h = logw.astype(BF16)
    lw_r = logw - lw_h.astype(F32)
    lw_m = lw_r.astype(BF16)
    lw_l = (lw_r - lw_m.astype(F32)).astype(BF16)
    L = _dot(tri_b, lw_h) + (_dot(tri_b, lw_m) + _dot(tri_b, lw_l))
    e_l = jnp.exp(L)
    e_nl = jnp.exp(-L)
    a_t = -kk * jnp.exp(L - logw)
    r_t = r * e_l
    b_t = kk * a * e_nl
    k_t = kmod * e_nl
    return v, a_t, r_t, b_t, k_t, e_l, bonus, g


def _rwkv_epilogue(o, bonus, g, e, lg_ref, lb_ref, rw_ref):
    mean = _head_sums(o, e) * (1.0 / HEAD_DIM)
    d = o - mean
    var = _head_sums(d * d, e) * (1.0 / HEAD_DIM)
    on = d * lax.rsqrt(var + LNX_EPS) * lg_ref[...] + lb_ref[...]
    rw_ref[...] = ((on + bonus) * g).astype(rw_ref.dtype)


def _rwkv_pair_kernel(p_ref, shift_ref, s0_ref, mu_ref, w0_ref, w2_ref, a0_ref, a2_ref, g2_ref,
                      kk_ref, ka_ref, rk_ref, lg_ref, lb_ref, e_ref,
                      rw_ref, sout_ref, shout_ref, s_scr, prev_scr, *, n_double):
    C = HEAD_DIM
    n_sub = p_ref.shape[0] // C
    n_pairs = RWKV_HEADS // 2
    PW = 2 * HEAD_DIM

    @pl.when(pl.program_id(1) == 0)
    def _():
        s_scr[...] = s0_ref[...]
        prev_scr[...] = shift_ref[...]

    e = e_ref[...]
    v, a_t, r_t, b_t, k_t, e_l, bonus, g = _rwkv_prologue(
        p_ref, shift_ref, prev_scr, shout_ref, mu_ref, w0_ref, w2_ref, a0_ref, a2_ref, g2_ref,
        kk_ref, ka_ref, rk_ref, e, C)

    lane = lax.broadcasted_iota(jnp.int32, (C, PW), 1)
    ti = lax.broadcasted_iota(jnp.int32, (C, PW), 0)
    si = lane & (HEAD_DIM - 1)
    first_head = lane < HEAD_DIM
    tri_incl = ti >= si
    tri_strict = ti > si
    eye = jnp.where(ti == si, 1.0, 0.0)
    rows2 = lax.broadcasted_iota(jnp.int32, (2 * C, PW), 0)
    lanes2 = lax.broadcasted_iota(jnp.int32, (2 * C, PW), 1)
    same_head = (rows2 < C) == (lanes2 < HEAD_DIM)

    def bd(x):
        zero = jnp.zeros_like(x)
        return jnp.concatenate([jnp.where(first_head, x, zero), jnp.where(first_head, zero, x)], axis=0)

    def dot_bd(x, y):
        return _dot(x.astype(BF16), bd(y.astype(BF16)))

    items = [(c, p) for c in range(n_sub) for p in range(n_pairs)]
    tile = lambda t, c, p: t[C * c:C * (c + 1), PW * p:PW * (p + 1)]
    ar = {i: jnp.concatenate([tile(a_t, *i), tile(r_t, *i)], axis=0).astype(BF16) for i in items}
    bk = {i: jnp.concatenate([tile(b_t, *i), tile(k_t, *i)], axis=0).astype(BF16) for i in items}
    gm = {i: _dot_nt(ar[i], jnp.concatenate([bd(bk[i][0:C]), bd(bk[i][C:2 * C])], axis=0)) for i in items}
    m_ab = {i: jnp.where(tri_strict, gm[i][0:C, 0:PW], 0.0) for i in items}
    m_ak = {i: jnp.where(tri_strict, gm[i][0:C, PW:2 * PW], 0.0).astype(BF16) for i in items}
    n_r = {i: jnp.concatenate([jnp.where(tri_incl, gm[i][C:2 * C, 0:PW], 0.0),
                               jnp.where(tri_incl, gm[i][C:2 * C, PW:2 * PW], 0.0)], axis=1).astype(BF16)
           for i in items}
    t_inv = {i: eye + m_ab[i] for i in items}
    mp = m_ab
    for _ in range(n_double):
        mp = {i: dot_bd(mp[i], mp[i]) for i in items}
        t_inv = {i: t_inv[i] + dot_bd(t_inv[i], mp[i]) for i in items}
    v_b = {i: tile(v, *i).astype(BF16) for i in items}
    v_bd = {i: bd(v_b[i]) for i in items}
    mv = {i: _dot(m_ak[i], v_bd[i]) for i in items}
    state = [s_scr[p] for p in range(n_pairs)]
    o_rows = []
    for c in range(n_sub):
        its = [(c, p) for p in range(n_pairs)]
        ars = {i: _dot_nt(ar[i], state[i[1]].astype(BF16)) for i in its}
        u_b = {i: dot_bd(t_inv[i], ars[i][0:C] + mv[i]).astype(BF16) for i in its}
        o = [ars[i][C:2 * C] + _dot(n_r[i], jnp.concatenate([bd(u_b[i]), v_bd[i]], axis=0)) for i in its]
        o_rows.append(jnp.concatenate(o, axis=1))
        for i in its:
            upd = jnp.where(same_head, _dot_tn(jnp.concatenate([u_b[i], v_b[i]], axis=0), bk[i]), 0.0)
            state[i[1]] = (state[i[1]] + upd) * e_l[C * (c + 1) - 1:C * (c + 1), PW * i[1]:PW * (i[1] + 1)]
    for p in range(n_pairs):
        s_scr[p] = state[p]
        sout_ref[p] = state[p]
    _rwkv_epilogue(jnp.concatenate(o_rows, axis=0), bonus, g, e, lg_ref, lb_ref, rw_ref)


def _rwkv(P, shift0, s0, lw, C):
    B, T, _ = P.shape
    assert C == HEAD_DIM, "the pair-packed chunk kernel needs chunk length == head dim"
    rows = C * RWKV_CHUNKS_PER_STEP
    assert T % rows == 0
    n_double = max(int(np.log2(C)) - 1, 0)
    W = RWKV_W
    e = np.kron(np.eye(2, dtype=np.float32), np.ones((HEAD_DIM, HEAD_DIM), np.float32))
    zpad = jnp.zeros((DECAY_LORA, W), F32)
    w2p = jnp.concatenate([lw["w2"], jnp.zeros((AAA_LORA, W), F32)], axis=0).astype(BF16)
    a2p = jnp.concatenate([zpad, lw["a2"]], axis=0).astype(BF16)
    row = lambda t: t.reshape(1, -1)
    consts = [row(lw["mu"]), row(lw["w0"]), w2p, row(lw["a0"]), a2p, lw["g2"].astype(BF16),
              row(lw["k_k"]), row(lw["k_a"]), row(lw["r_k"]), row(lw["lnx_g"]), row(lw["lnx_b"]),
              jnp.asarray(e, BF16)]
    sp = s0.reshape(B, RWKV_HEADS // 2, 2, HEAD_DIM, HEAD_DIM)
    z = jnp.zeros_like(sp[:, :, 0])
    s0 = jnp.concatenate([jnp.concatenate([sp[:, :, 0], z], axis=-1),
                          jnp.concatenate([z, sp[:, :, 1]], axis=-1)], axis=-2)
    st_shape = s0.shape[1:]
    const = lambda arr: pl.BlockSpec(arr.shape, lambda b, c: (0, 0))
    st_spec = pl.BlockSpec((None,) + st_shape, lambda b, c: (b, 0, 0, 0))
    sh_spec = pl.BlockSpec((None, 1, RWKV_IN), lambda b, c: (b, 0, 0))
    rw, s_out, sh_out = pl.pallas_call(
        functools.partial(_rwkv_pair_kernel, n_double=n_double),
        grid=(B, T // rows),
        in_specs=[pl.BlockSpec((None, rows, RWKV_IN), lambda b, c: (b, c, 0)), sh_spec, st_spec]
                 + [const(t) for t in consts],
        out_specs=[pl.BlockSpec((None, rows, W), lambda b, c: (b, c, 0)), st_spec, sh_spec],
        out_shape=[jax.ShapeDtypeStruct((B, T, W), BF16),
                   jax.ShapeDtypeStruct((B,) + st_shape, F32),
                   jax.ShapeDtypeStruct((B, 1, RWKV_IN), F32)],
        scratch_shapes=[pltpu.VMEM(st_shape, F32),
                        pltpu.VMEM((1, RWKV_IN), F32)],
        compiler_params=_params("arbitrary", "arbitrary"),
        name=f"rwkv_c{C}",
    )(P, shift0.reshape(B, 1, RWKV_IN), s0, *consts)
    s_out = jnp.stack([s_out[:, :, :HEAD_DIM, :HEAD_DIM], s_out[:, :, HEAD_DIM:, HEAD_DIM:]], axis=2)
    s_out = s_out.reshape(B, RWKV_HEADS, HEAD_DIM, HEAD_DIM)
    return rw, s_out, sh_out.reshape(B, RWKV_IN)


def _rwkv_scan_kernel(r_ref, k_ref, v_ref, lo_ref, sr_ref, sk_ref, sv_ref, slo_ref, s_ref,
                      mur_ref, muk_ref, muv_ref, mulo_ref, w0_ref, w2t_ref, a0_ref, a2t_ref, g2t_ref,
                      kk_ref, ka_ref, rk_ref, lg_ref, lb_ref,
                      rw_ref, sout_ref,
                      w_scr, nkk_scr, bb_scr, km_scr, rr_scr, vv_scr, bonus_scr, g_scr, o_scr):
    T = r_ref.shape[0]

    def shifted(ref, s0_ref, mu_ref, t):
        cur = ref[t]
        prev = s0_ref[...] if t == 0 else ref[t - 1]
        return cur + (prev - cur) * mu_ref[...]

    for t in range(T):
        r = shifted(r_ref, sr_ref, mur_ref, t)
        k = shifted(k_ref, sk_ref, muk_ref, t)
        v = shifted(v_ref, sv_ref, muv_ref, t)
        lo = shifted(lo_ref, slo_ref, mulo_ref, t)
        wl, al, gl = lo[0:DECAY_LORA], lo[DECAY_LORA:DECAY_LORA + AAA_LORA], lo[DECAY_LORA + AAA_LORA:]
        neg = -(w0_ref[...] + _dot(w2t_ref[...], jnp.tanh(wl).astype(BF16)))
        softplus = jnp.maximum(neg, 0.0) + jnp.log(1.0 + jnp.exp(-jnp.abs(neg)))
        w = jnp.exp(-jnp.exp(-softplus - 0.5))
        a = jax.nn.sigmoid(a0_ref[...] + _dot(a2t_ref[...], al.astype(BF16)))
        g_scr[t] = _dot(g2t_ref[...], jax.nn.sigmoid(gl).astype(BF16))
        kkr = k * kk_ref[...]
        kk = kkr / jnp.maximum(jnp.sqrt(jnp.sum(kkr * kkr, axis=0, keepdims=True)), 1e-12)
        kmod = k * (1.0 + (a - 1.0) * ka_ref[...])
        bonus_scr[t] = jnp.sum(r * kmod * rk_ref[...], axis=0, keepdims=True) * v
        w_scr[t] = w
        nkk_scr[t] = -kk
        bb_scr[t] = kk * a
        km_scr[t] = kmod
        rr_scr[t] = r
        vv_scr[t] = v

    def row(i, carry):
        s = s_ref[i]
        for t in range(T):
            sa = jnp.sum(s * nkk_scr[t], axis=0, keepdims=True)
            s = s * w_scr[t] + sa * bb_scr[t] + vv_scr[t, pl.ds(i, 1), :] * km_scr[t]
            o_scr[t, pl.ds(i, 1), :] = jnp.sum(s * rr_scr[t], axis=0, keepdims=True)
        sout_ref[i] = s
        return carry

    lax.fori_loop(0, HEAD_DIM, row, 0)
    o = o_scr[...]
    mean = jnp.mean(o, axis=1, keepdims=True)
    d = o - mean
    var = jnp.mean(d * d, axis=1, keepdims=True)
    on = d * lax.rsqrt(var + LNX_EPS) * lg_ref[...] + lb_ref[...]
    rw_ref[...] = ((on + bonus_scr[...]) * g_scr[...]).astype(rw_ref.dtype)


def _rwkv_scan(pt, shift0, s0t, lw):
    T, _, DB = pt.shape
    W, HD = RWKV_W, HEAD_DIM
    lora_w = DECAY_LORA + AAA_LORA + GATE_LORA
    assert (3 * W) % lora_w == 0
    col = lambda t: jnp.broadcast_to(t.reshape(-1, 1), (t.size, DB))
    shift_t = shift0.T
    mu_c = col(lw["mu"])
    head = lambda base: (lambda h: (base + h, 0))
    pt_blk = lambda base: pl.BlockSpec((T, HD, DB), lambda h: (0, base + h, 0))
    blk2 = lambda base: pl.BlockSpec((HD, DB), head(base))
    lo_pt = pl.BlockSpec((T, lora_w, DB), lambda h: (0, 3 * W // lora_w, 0))
    lo_2d = pl.BlockSpec((lora_w, DB), lambda h: (3 * W // lora_w, 0))
    nh = W // HD
    st_spec = pl.BlockSpec((None, HD, HD, DB), lambda h: (h, 0, 0, 0))
    wt_spec = lambda n: pl.BlockSpec((HD, n), lambda h: (h, 0))
    scr = lambda: pltpu.VMEM((T, HD, DB), F32)
    return pl.pallas_call(
        _rwkv_scan_kernel,
        grid=(nh,),
        in_specs=[pt_blk(0), pt_blk(nh), pt_blk(2 * nh), lo_pt,
                  blk2(0), blk2(nh), blk2(2 * nh), lo_2d, st_spec,
                  blk2(0), blk2(nh), blk2(2 * nh), lo_2d,
                  blk2(0), wt_spec(DECAY_LORA), blk2(0), wt_spec(AAA_LORA), wt_spec(GATE_LORA),
                  blk2(0), blk2(0), blk2(0), blk2(0), blk2(0)],
        out_specs=[pl.BlockSpec((T, HD, DB), lambda h: (0, h, 0)), st_spec],
        out_shape=[jax.ShapeDtypeStruct((T, W, DB), F32), jax.ShapeDtypeStruct(s0t.shape, F32)],
        scratch_shapes=[scr() for _ in range(9)],
        compiler_params=_params("arbitrary"),
        name="rwkv_scan",
    )(pt, pt, pt, pt, shift_t, shift_t, shift_t, shift_t, s0t,
      mu_c, mu_c, mu_c, mu_c, col(lw["w0"]), lw["w2"].T.astype(BF16), col(lw["a0"]), lw["a2"].T.astype(BF16),
      lw["g2"].T.astype(BF16), col(lw["k_k"]), col(lw["k_a"]), col(lw["r_k"]), col(lw["lnx_g"]), col(lw["lnx_b"]))


def _mlp_kernel(att_ref, rw_ref, x_ref, gt1_ref, sc2_ref, sh2_ref, gt2_ref, g2_ref, gf_ref,
                wo_ref, w1_ref, w2_ref, y_ref, *, final_norm, ff_chunk):
    half = wo_ref.shape[0] // 2
    mix = _dot(att_ref[...], wo_ref[0:half, :]) + _dot(rw_ref[...], wo_ref[half:, :])
    x1 = x_ref[...] + gt1_ref[...] * mix
    ms = jnp.mean(x1 * x1, axis=-1, keepdims=True)
    h2 = x1 * lax.rsqrt(ms + NORM_EPS) * g2_ref[...]
    h2 = (h2 * (1.0 + sc2_ref[...]) + sh2_ref[...]).astype(BF16)
    acc = jnp.zeros_like(x1)
    for c in range(w1_ref.shape[1] // ff_chunk):
        hid = _dot(h2, w1_ref[:, c * ff_chunk:(c + 1) * ff_chunk])
        hid = jnp.square(jnp.maximum(hid, 0.0)).astype(BF16)
        acc = acc + _dot(hid, w2_ref[c * ff_chunk:(c + 1) * ff_chunk, :])
    x2 = x1 + gt2_ref[...] * acc
    if final_norm:
        ms2 = jnp.mean(x2 * x2, axis=-1, keepdims=True)
        x2 = x2 * lax.rsqrt(ms2 + NORM_EPS) * gf_ref[...]
    y_ref[...] = x2


def _mlp(att, rw, x, gt1, sc2, sh2, gt2, g2, gf, wo_b, w1_b, w2_b, tm, per_row_mod, final_norm):
    G, R, D = x.shape
    dff = w1_b.shape[1]
    row_spec = lambda w: pl.BlockSpec((None, tm, w), lambda b, i: (b, i, 0))
    if per_row_mod:
        mod_spec = pl.BlockSpec((None, tm, D), lambda b, i: (b, i, 0))
    else:
        mod_spec = pl.BlockSpec((None, 1, D), lambda b, i: (b, 0, 0))
    const = lambda shape: pl.BlockSpec(shape, lambda b, i: (0, 0), pipeline_mode=pl.Buffered(1))
    return pl.pallas_call(
        functools.partial(_mlp_kernel, final_norm=final_norm, ff_chunk=1024),
        grid=(G, R // tm),
        in_specs=[row_spec(att.shape[-1]), row_spec(rw.shape[-1]), row_spec(D),
                  mod_spec, mod_spec, mod_spec, mod_spec, const((1, D)), const((1, D)),
                  const(wo_b.shape), const((D, dff)), const((dff, D))],
        out_specs=row_spec(D),
        out_shape=jax.ShapeDtypeStruct((G, R, D), F32),
        compiler_params=_params("arbitrary", "arbitrary"),
        name="mlp",
    )(att, rw, x, gt1, sc2, sh2, gt2, g2.reshape(1, D), gf.reshape(1, D), wo_b, w1_b, w2_b)


def kernel(x_prompt, x_sample, cache_k, cache_v, state_wkv, state_shift, c_prompt, c_sample, norm1_g, norm2_g, w_ada, b_ada, w_in, w_out, mu, w0, w2, a0, a2, g2, k_k, k_a, r_k, lnx_g, lnx_b, w_ff1, w_ff2, normf_g):
    B, S, D = x_prompt.shape
    DB, T, _ = x_sample.shape
    depth = w_in.shape[0]
    dt = x_prompt.dtype
    tabs_p = _rope_tables(jnp.arange(S))
    tabs_s = tuple(jnp.tile(t, (DB, 1)) for t in _rope_tables(PAST_LEN + jnp.arange(T)))
    tabs_t = tuple(jnp.repeat(t, DB, axis=0) for t in _rope_tables(PAST_LEN + jnp.arange(T)))
    c_all = jnp.concatenate([c_prompt, c_sample], axis=0)
    hp = x_prompt
    hs = x_sample.reshape(1, DB * T, D)
    outs = [[] for _ in range(8)]
    tm_s = min(ROW_TILE, DB * T)
    for l in range(depth):
        last = l == depth - 1
        mod = _ada(c_all, w_ada[l], b_ada[l])
        mod_p = [m.reshape(B, 1, D) for m in jnp.split(mod[:B], 6, axis=-1)]
        mod_s = [jnp.repeat(m, T, axis=0).reshape(1, DB * T, D) for m in jnp.split(mod[B:], 6, axis=-1)]
        mod_t = [m.reshape(1, DB, D) for m in jnp.split(mod[B:], 6, axis=-1)]
        w_in_b, wo_b = w_in[l].astype(BF16), w_out[l].astype(BF16)
        w1_b, w2_b = w_ff1[l].astype(BF16), w_ff2[l].astype(BF16)
        lw = dict(mu=mu[l], w0=w0[l], w2=w2[l], a0=a0[l], a2=a2[l], g2=g2[l], k_k=k_k[l], k_a=k_a[l],
                  r_k=r_k[l], lnx_g=lnx_g[l], lnx_b=lnx_b[l])

        sh1, sc1, gt1, sh2, sc2, gt2 = mod_p
        q, k, v, P, kt, vt = _inproj(hp, sc1, sh1, norm1_g[l], w_in_b, tabs_p, ROW_TILE, "batch", True)
        att = _attn_fused(q, k, v)
        rw, s_p, shift_p = _rwkv(P, jnp.zeros((B, RWKV_IN), F32),
                                 jnp.zeros((B, RWKV_HEADS, HEAD_DIM, HEAD_DIM), F32), lw, RWKV_CHUNK)
        hp = _mlp(att, rw, hp, gt1, sc2, sh2, gt2, norm2_g[l], normf_g, wo_b, w1_b, w2_b, ROW_TILE, False, last)
        keep = min(max(w for w, _ in DIL_BRANCHES), S)
        win = lambda t: jnp.transpose(t.reshape(B, ATT_HEADS, HEAD_DIM, S)[..., S - keep:], (0, 3, 1, 2))
        outs[0].append(win(kt))
        outs[1].append(win(vt))
        outs[2].append(s_p)
        outs[3].append(shift_p)

        sh1, sc1, gt1, sh2, sc2, gt2 = mod_s
        q, k, v, P = _inproj(hs, sc1, sh1, norm1_g[l], w_in_b, tabs_s, tm_s, "row", False)
        hs_t = jnp.transpose(hs.reshape(DB, T, D), (1, 0, 2)).reshape(1, T * DB, D)
        *_, kt, vt, pt = _inproj(hs_t, mod_t[1], mod_t[0], norm1_g[l], w_in_b, tabs_t, DB, "tile", True, nq=T, p_t=True)
        att = _sample_attn(q.reshape(DB, T, ATT_W), k.reshape(DB, T, ATT_W), v.reshape(DB, T, ATT_W),
                           jnp.transpose(cache_k[l], (0, 2, 3, 1)), jnp.transpose(cache_v[l], (0, 2, 3, 1)))
        rw_t, s_t = _rwkv_scan(pt.reshape(T, RWKV_IN, DB), state_shift[l],
                               jnp.transpose(state_wkv[l], (1, 2, 3, 0)), lw)
        rw = jnp.transpose(rw_t, (2, 0, 1)).astype(BF16)
        s_s = jnp.transpose(s_t, (3, 0, 1, 2))
        shift_s = P.reshape(DB, T, RWKV_IN)[:, T - 1]
        hs = _mlp(att.reshape(1, DB * T, ATT_W), rw.reshape(1, DB * T, RWKV_W), hs, gt1, sc2, sh2, gt2,
                  norm2_g[l], normf_g, wo_b, w1_b, w2_b, tm_s, True, last)
        new = lambda t: jnp.transpose(t.reshape(T, ATT_HEADS, HEAD_DIM, DB), (3, 0, 1, 2))
        outs[4].append(new(kt))
        outs[5].append(new(vt))
        outs[6].append(s_s)
        outs[7].append(shift_s)
    stack = lambda xs: jnp.stack(xs).astype(dt)
    return (hp.astype(dt), hs.reshape(DB, T, D).astype(x_sample.dtype), *[stack(o) for o in outs])
```

```python
import functools

import numpy as np
import jax
import jax.numpy as jnp
from jax import lax
from jax.experimental import pallas as pl
from jax.experimental.pallas import tpu as pltpu

F32 = jnp.float32
BF16 = jnp.bfloat16

HEAD_DIM = 64
ATT_HEADS = 8
ATT_W = ATT_HEADS * HEAD_DIM
RWKV_HEADS = 8
RWKV_W = RWKV_HEADS * HEAD_DIM
DIL_BRANCHES = ((128, 1), (512, 4), (2048, 16))
ROT_DIM = HEAD_DIM // 4
ROPE_THETA = 500000.0
DECAY_LORA = 64
AAA_LORA = 64
GATE_LORA = 128
RWKV_IN = 3 * RWKV_W + DECAY_LORA + AAA_LORA + GATE_LORA
NORM_EPS = 1e-6
LNX_EPS = 64e-5
PAST_LEN = 8192
ATT_BLOCK = 128
NEG_BIG = -1e30
LOG2_E = 1.4426950408889634
VMEM_LIMIT = 56 * 1024 * 1024
ROW_TILE = 512
RWKV_CHUNK = 64
RWKV_CHUNKS_PER_STEP = 4
SAMPLE_ATT_BATCH = 2
ATT_GROUP = 4


def _dot(a, b):
    return jnp.dot(a, b, preferred_element_type=F32)


def _dot_nt(a, b):
    return lax.dot_general(a, b, (((1,), (1,)), ((), ())), preferred_element_type=F32)


def _dot_tn(a, b):
    return lax.dot_general(a, b, (((0,), (0,)), ((), ())), preferred_element_type=F32)


def _split2(x):
    hi = x.astype(BF16)
    lo = (x - hi.astype(F32)).astype(BF16)
    return hi, lo


def _head_sums(x, e):
    C, W = x.shape
    n = W // e.shape[0]
    xh, xl = _split2(x)
    tiles = [t[:, e.shape[0] * p:e.shape[0] * (p + 1)] for t in (xh, xl) for p in range(n)]
    o = _dot(jnp.concatenate(tiles, axis=0), e)
    return jnp.concatenate([o[C * p:C * (p + 1)] + o[C * (n + p):C * (n + p + 1)] for p in range(n)], axis=1)


def _params(*sem):
    return pltpu.CompilerParams(dimension_semantics=sem, vmem_limit_bytes=VMEM_LIMIT)


def _ada_kernel(c_ref, w_ref, b_ref, o_ref):
    c = c_ref[...]
    s = c * jax.nn.sigmoid(c)
    o_ref[...] = _dot(s.astype(BF16), w_ref[...].astype(BF16)) + b_ref[...]


def _ada(c, w_ada, b_ada):
    n, d = c.shape
    cols = w_ada.shape[1]
    tn = 512
    return pl.pallas_call(
        _ada_kernel,
        grid=(cols // tn,),
        in_specs=[pl.BlockSpec((n, d), lambda j: (0, 0)),
                  pl.BlockSpec((d, tn), lambda j: (0, j)),
                  pl.BlockSpec((1, tn), lambda j: (0, j))],
        out_specs=pl.BlockSpec((n, tn), lambda j: (0, j)),
        out_shape=jax.ShapeDtypeStruct((n, cols), F32),
        compiler_params=_params("arbitrary"),
        name="ada",
    )(c, w_ada, b_ada.reshape(1, cols))


def _rope(t, cos, s1, s2):
    parts = []
    for c in range(ATT_W // 128):
        tc = t[:, 128 * c:128 * (c + 1)]
        parts.append(tc * cos + pltpu.roll(tc, 128 - ROT_DIM // 2, 1) * s1 + pltpu.roll(tc, ROT_DIM // 2, 1) * s2)
    return jnp.concatenate(parts, axis=1)


def _inproj_kernel(x_ref, sc_ref, sh_ref, g_ref, w_ref, cos_ref, s1_ref, s2_ref, *out_refs,
                   transposed, p_t):
    q_ref, k_ref, v_ref, p_ref = out_refs[:4]
    rest = list(out_refs[4:])
    x = x_ref[...]
    ms = jnp.mean(x * x, axis=-1, keepdims=True)
    h = x * lax.rsqrt(ms + NORM_EPS) * g_ref[...]
    h = (h * (1.0 + sc_ref[...]) + sh_ref[...]).astype(BF16)
    cos, s1, s2 = cos_ref[...], s1_ref[...], s2_ref[...]
    q = _rope(_dot(h, w_ref[:, 0:ATT_W]), cos, s1, s2)
    q_ref[...] = (q * HEAD_DIM ** -0.5).astype(q_ref.dtype)
    k = _rope(_dot(h, w_ref[:, ATT_W:2 * ATT_W]), cos, s1, s2)
    v = _dot(h, w_ref[:, 2 * ATT_W:3 * ATT_W])
    k_ref[...] = k
    v_ref[...] = v
    p = _dot(h, w_ref[:, 3 * ATT_W:])
    p_ref[...] = p
    if transposed:
        rest.pop(0)[...] = k.T
        rest.pop(0)[...] = v.T
    if p_t:
        rest.pop(0)[...] = p.T


def _inproj(x, sc, sh, g, w_in_b, tabs, tm, mod_mode, transposed, nq=1, p_t=False):
    G, R, D = x.shape
    nin = w_in_b.shape[1]
    row_spec = lambda w: pl.BlockSpec((None, tm, w), lambda b, i: (b, i, 0))
    mod_spec = {"batch": pl.BlockSpec((None, 1, D), lambda b, i: (b, 0, 0)),
                "row": pl.BlockSpec((None, tm, D), lambda b, i: (b, i, 0)),
                "tile": pl.BlockSpec((None, tm, D), lambda b, i: (b, 0, 0))}[mod_mode]
    tab_spec = pl.BlockSpec((tm, 128), lambda b, i: (i, 0))
    const = lambda shape: pl.BlockSpec(shape, lambda b, i: (0, 0))
    out_specs = [row_spec(ATT_W)] * 3 + [row_spec(RWKV_IN)]
    out_shape = [jax.ShapeDtypeStruct((G, R, ATT_W), F32)] * 3 + [jax.ShapeDtypeStruct((G, R, RWKV_IN), F32)]
    if transposed:
        spq = (R // nq) // tm
        t_spec = pl.BlockSpec((None, None, ATT_W, tm), lambda b, i: (b, i // spq, 0, i % spq))
        out_specs += [t_spec] * 2
        out_shape += [jax.ShapeDtypeStruct((G, nq, ATT_W, R // nq), F32)] * 2
        if p_t:
            out_specs.append(pl.BlockSpec((None, None, RWKV_IN, tm), lambda b, i: (b, i // spq, 0, i % spq)))
            out_shape.append(jax.ShapeDtypeStruct((G, nq, RWKV_IN, R // nq), F32))
    return pl.pallas_call(
        functools.partial(_inproj_kernel, transposed=transposed, p_t=p_t),
        grid=(G, R // tm),
        in_specs=[row_spec(D), mod_spec, mod_spec, const((1, D)), const((D, nin)),
                  tab_spec, tab_spec, tab_spec],
        out_specs=out_specs,
        out_shape=out_shape,
        compiler_params=_params("arbitrary", "arbitrary"),
        name="inproj",
    )(x, sc, sh, g.reshape(1, D), w_in_b, *tabs)


def _rope_tables(pos):
    half = ROT_DIM // 2
    inv = ROPE_THETA ** (-jnp.arange(half, dtype=F32) * (2.0 / ROT_DIM))
    ang = pos.astype(F32)[:, None] * inv[None, :]
    cos, sin = jnp.cos(ang), jnp.sin(ang)
    n = pos.shape[0]
    ones = jnp.ones((n, HEAD_DIM - ROT_DIM), F32)
    zeros = jnp.zeros((n, HEAD_DIM - ROT_DIM), F32)
    zh = jnp.zeros((n, half), F32)
    ctab = jnp.concatenate([cos, cos, ones], axis=1)
    s1 = jnp.concatenate([-sin, zh, zeros], axis=1)
    s2 = jnp.concatenate([zh, sin, zeros], axis=1)
    return tuple(jnp.tile(t, (1, 128 // HEAD_DIM)) for t in (ctab, s1, s2))


def _attn_fused_kernel(q_ref, k_ref, v_ref, o_ref, acc_scr, m_scr, l_scr, *, dilations, group):
    nw, HD = ATT_BLOCK, HEAD_DIM
    S = q_ref.shape[0]
    n_tiles = S // nw
    lane = lax.broadcasted_iota(jnp.int32, (nw, 2 * HD), 1)
    first_head = lane < HD
    q_idx = lax.broadcasted_iota(jnp.int32, (nw, 2 * HD), 0)
    n_col = nw // HD
    key_idx = [HD * jc + (lane & (HD - 1)) for jc in range(n_col)]
    cur_ok = [q_idx >= key_idx[jc] for jc in range(n_col)]

    def bd(x):
        zero = jnp.zeros_like(x)
        xa, xb = jnp.where(first_head, x, zero), jnp.where(first_head, zero, x)
        parts = []
        for jc in range(n_col):
            parts += [xa[HD * jc:HD * (jc + 1)], xb[HD * jc:HD * (jc + 1)]]
        return jnp.concatenate(parts, axis=0)

    bd_row = lax.broadcasted_iota(jnp.int32, (2 * nw, 2 * HD), 0)
    bd_lane = lax.broadcasted_iota(jnp.int32, (2 * nw, 2 * HD), 1)
    ones_bd = jnp.where(((bd_row // HD) % 2 == 0) == (bd_lane < HD), 1.0, 0.0).astype(BF16)

    def with_ones(v):
        return jnp.concatenate([bd(v.astype(BF16)), ones_bd], axis=1)

    def half_reduce(x, op, fill):
        ra = op(jnp.where(first_head, x, fill), axis=-1, keepdims=True)
        rb = op(jnp.where(first_head, fill, x), axis=-1, keepdims=True)
        return jnp.where(first_head, ra, rb)

    def steps(tiles, use_prev, init, final):
        n = range(len(tiles))
        qs = [(q_ref[tiles[t][0], :] * LOG2_E).astype(BF16) for t in n]
        cols, vbd = [], []
        for t in n:
            rows, prev_rows, has_prev = tiles[t]
            s = _dot_nt(qs[t], bd(k_ref[rows, :].astype(BF16)))
            c = [jnp.where(cur_ok[jc], s[:, 2 * HD * jc:2 * HD * (jc + 1)], NEG_BIG) for jc in range(n_col)]
            vb = [with_ones(v_ref[rows, :])]
            if use_prev:
                sp = _dot_nt(qs[t], bd(k_ref[prev_rows, :].astype(BF16)))
                thresh = jnp.where(has_prev, 0, nw)
                c += [jnp.where((key_idx[jc] - q_idx) >= thresh, sp[:, 2 * HD * jc:2 * HD * (jc + 1)], NEG_BIG)
                      for jc in range(n_col)]
                vb.append(with_ones(v_ref[prev_rows, :]))
            cols.append(c)
            vbd.append(vb)
        m_new, m_old, p = [], [], []
        for t in n:
            m = half_reduce(functools.reduce(jnp.maximum, cols[t]), jnp.max, NEG_BIG)
            if not init:
                m_old.append(m_scr[tiles[t][0], :])
                m = jnp.maximum(m, m_old[t])
            m_new.append(m)
            p.append([jnp.exp2(c - m) for c in cols[t]])
        acc = []
        for t in n:
            a = _dot(jnp.concatenate(p[t][0:n_col], axis=1).astype(BF16), vbd[t][0])
            if use_prev:
                a = a + _dot(jnp.concatenate(p[t][n_col:], axis=1).astype(BF16), vbd[t][1])
            acc.append(a)
        for t in n:
            rows = tiles[t][0]
            a, l = acc[t][:, 0:2 * HD], acc[t][:, 2 * HD:]
            if not init:
                alpha = jnp.exp2(m_old[t] - m_new[t])
                a = a + acc_scr[rows, :] * alpha
                l = l + l_scr[rows, :] * alpha
            if final:
                o_ref[rows, :] = (a / l).astype(o_ref.dtype)
            else:
                acc_scr[rows, :] = a
                m_scr[rows, :] = m_new[t]
                l_scr[rows, :] = l

    n_br = len(dilations)
    for bi, dil in enumerate(dilations):
        init, final = bi == 0, bi == n_br - 1
        use_prev = S // (dil * nw) > 1

        def group_body(g, carry, dil=dil, use_prev=use_prev, init=init, final=final):
            tiles = []
            for i in range(group):
                t = g * group + i
                r, j = lax.rem(t, dil), lax.div(t, dil)
                start = r + j * (dil * nw)
                prev = r + jnp.maximum(j - 1, 0) * (dil * nw)
                if dil > 1:
                    tiles.append((pl.ds(start, nw, stride=dil), pl.ds(prev, nw, stride=dil), j > 0))
                else:
                    tiles.append((pl.ds(pl.multiple_of(start, nw), nw), pl.ds(pl.multiple_of(prev, nw), nw), j > 0))
            steps(tiles, use_prev, init, final)
            return carry

        lax.fori_loop(0, n_tiles // group, group_body, 0, unroll=True)


def _attn_fused(q, k, v):
    B, S, W = q.shape
    nw = ATT_BLOCK
    dilations = tuple(sorted((d for _, d in DIL_BRANCHES), reverse=True))
    for win, d in DIL_BRANCHES:
        assert win == d * nw and S % (d * nw) == 0
    assert (S // nw) % ATT_GROUP == 0
    PW = 2 * HEAD_DIM
    spec = pl.BlockSpec((None, S, PW), lambda b, p: (b, 0, p))
    return pl.pallas_call(
        functools.partial(_attn_fused_kernel, dilations=dilations, group=ATT_GROUP),
        grid=(B, W // PW),
        in_specs=[spec, spec, spec],
        out_specs=spec,
        out_shape=jax.ShapeDtypeStruct((B, S, W), BF16),
        scratch_shapes=[pltpu.VMEM((S, PW), F32) for _ in range(3)],
        compiler_params=_params("arbitrary", "arbitrary"),
        name="attn_fused",
    )(q, k, v)


def _sample_counts(T, WB):
    i = np.arange(T)
    c = np.arange(WB)
    delta = WB + i[None, :] - c[:, None]
    cnt_c = np.zeros((WB, T), np.float32)
    diff = i[None, :] - i[:, None]
    cnt_n = np.zeros((T, T), np.float32)
    for win, dil in DIL_BRANCHES:
        cnt_c += ((delta > 0) & (delta % dil == 0) & (delta <= win)).astype(np.float32)
        cnt_n += ((diff >= 0) & (diff % dil == 0) & (diff <= win)).astype(np.float32)
    return cnt_c, cnt_n


def _sample_attn_kernel(q_ref, kn_ref, vn_ref, kt_ref, vt_ref, cc_ref, cn_ref, o_ref):
    cc, cn = cc_ref[...], cn_ref[...]
    nb = q_ref.shape[0]
    items = [(b, h) for b in range(nb) for h in range(ATT_HEADS)]
    sl = lambda h: slice(HEAD_DIM * h, HEAD_DIM * (h + 1))
    qh = {(b, h): q_ref[b, :, sl(h)] for b, h in items}
    s_c = {i: jnp.where(cc > 0, _dot(qh[i].astype(BF16), kt_ref[i[0], i[1]].astype(BF16)), NEG_BIG)
           for i in items}
    s_n = {i: jnp.where(cn > 0, _dot_nt(qh[i], kn_ref[i[0], :, sl(i[1])]), NEG_BIG) for i in items}
    p_c, p_n, l = {}, {}, {}
    for i in items:
        m = jnp.maximum(jnp.max(s_c[i], axis=-1, keepdims=True), jnp.max(s_n[i], axis=-1, keepdims=True))
        p_c[i] = cc * jnp.exp(s_c[i] - m)
        p_n[i] = cn * jnp.exp(s_n[i] - m)
        l[i] = jnp.sum(p_c[i], axis=-1, keepdims=True) + jnp.sum(p_n[i], axis=-1, keepdims=True)
    acc = {i: _dot_nt(p_c[i].astype(BF16), vt_ref[i[0], i[1]].astype(BF16)) + _dot(p_n[i], vn_ref[i[0], :, sl(i[1])])
           for i in items}
    for b in range(nb):
        o_ref[b] = jnp.concatenate([acc[(b, h)] / l[(b, h)] for h in range(ATT_HEADS)], axis=1).astype(o_ref.dtype)


def _sample_attn(q, k_new, v_new, cache_kt, cache_vt):
    DB, T, W = q.shape
    WB = cache_kt.shape[-1]
    cnt_c, cnt_n = _sample_counts(T, WB)
    cc = jnp.asarray(cnt_c.T)
    cn = jnp.asarray(cnt_n.T)
    nb = SAMPLE_ATT_BATCH if DB % SAMPLE_ATT_BATCH == 0 else 1
    new_spec = pl.BlockSpec((nb, T, W), lambda b: (b, 0, 0))
    cache_spec = pl.BlockSpec((nb, ATT_HEADS, HEAD_DIM, WB), lambda b: (b, 0, 0, 0))
    const = lambda arr: pl.BlockSpec(arr.shape, lambda b: (0, 0))
    return pl.pallas_call(
        _sample_attn_kernel,
        grid=(DB // nb,),
        in_specs=[new_spec, new_spec, new_spec, cache_spec, cache_spec, const(cc), const(cn)],
        out_specs=new_spec,
        out_shape=jax.ShapeDtypeStruct((DB, T, W), BF16),
        compiler_params=_params("arbitrary"),
        name="sample_attn",
    )(q, k_new, v_new, cache_kt, cache_vt, cc, cn)


def _rwkv_prologue(p_ref, shift_ref, prev_scr, shout_ref, mu_ref, w0_ref, w2_ref, a0_ref, a2_ref, g2_ref,
                   kk_ref, ka_ref, rk_ref, e, chunk):
    C = p_ref.shape[0]
    P = p_ref[...]
    row = lax.broadcasted_iota(jnp.int32, (C, 1), 0)
    p_prev = jnp.where(row == 0, prev_scr[...], pltpu.roll(P, 1, 0))
    pm = P + (p_prev - P) * mu_ref[...]
    prev_scr[...] = P[C - 1:C, :]
    shout_ref[...] = P[C - 1:C, :]

    W = RWKV_W
    r, k, v = pm[:, 0:W], pm[:, W:2 * W], pm[:, 2 * W:3 * W]
    lora_wa = pm[:, 3 * W:3 * W + DECAY_LORA + AAA_LORA]
    gl = pm[:, 3 * W + DECAY_LORA + AAA_LORA:]
    wpre = w0_ref[...] + _dot(jnp.tanh(lora_wa).astype(BF16), w2_ref[...])
    neg = -wpre
    softplus = jnp.maximum(neg, 0.0) + jnp.log(1.0 + jnp.exp(-jnp.abs(neg)))
    logw = -jnp.exp(-softplus - 0.5)
    a = jax.nn.sigmoid(a0_ref[...] + _dot(lora_wa.astype(BF16), a2_ref[...]))
    g = _dot(jax.nn.sigmoid(gl).astype(BF16), g2_ref[...])
    kkr = k * kk_ref[...]
    nrm = jnp.sqrt(_head_sums(kkr * kkr, e))
    kk = kkr / jnp.maximum(nrm, 1e-12)
    kmod = k * (1.0 + (a - 1.0) * ka_ref[...])
    bonus = _head_sums(r * kmod * rk_ref[...], e) * v

    ti = lax.broadcasted_iota(jnp.int32, (C, C), 0)
    si = lax.broadcasted_iota(jnp.int32, (C, C), 1)
    tri_b = jnp.where((ti >= si) & (lax.div(ti, chunk) == lax.div(si, chunk)), 1.0, 0.0).astype(BF16)
    lw_h = logw.astype(BF16)
    lw_r = logw - lw_h.astype(F32)
    lw_m = lw_r.astype(BF16)
    lw_l = (lw_r - lw_m.astype(F32)).astype(BF16)
    L = _dot(tri_b, lw_h) + (_dot(tri_b, lw_m) + _dot(tri_b, lw_l))
    e_l = jnp.exp(L)
    e_nl = jnp.exp(-L)
    a_t = -kk * jnp.exp(L - logw)
    r_t = r * e_l
    b_t = kk * a * e_nl
    k_t = kmod * e_nl
    return v, a_t, r_t, b_t, k_t, e_l, bonus, g


def _rwkv_epilogue(o, bonus, g, e, lg_ref, lb_ref, rw_ref):
    mean = _head_sums(o, e) * (1.0 / HEAD_DIM)
    d = o - mean
    var = _head_sums(d * d, e) * (1.0 / HEAD_DIM)
    on = d * lax.rsqrt(var + LNX_EPS) * lg_ref[...] + lb_ref[...]
    rw_ref[...] = ((on + bonus) * g).astype(rw_ref.dtype)


def _rwkv_pair_kernel(p_ref, shift_ref, s0_ref, mu_ref, w0_ref, w2_ref, a0_ref, a2_ref, g2_ref,
                      kk_ref, ka_ref, rk_ref, lg_ref, lb_ref, e_ref,
                      rw_ref, sout_ref, shout_ref, s_scr, prev_scr, *, n_double):
    C = HEAD_DIM
    n_sub = p_ref.shape[0] // C
    n_pairs = RWKV_HEADS // 2
    PW = 2 * HEAD_DIM

    @pl.when(pl.program_id(1) == 0)
    def _():
        s_scr[...] = s0_ref[...]
        prev_scr[...] = shift_ref[...]

    e = e_ref[...]
    v, a_t, r_t, b_t, k_t, e_l, bonus, g = _rwkv_prologue(
        p_ref, shift_ref, prev_scr, shout_ref, mu_ref, w0_ref, w2_ref, a0_ref, a2_ref, g2_ref,
        kk_ref, ka_ref, rk_ref, e, C)

    lane = lax.broadcasted_iota(jnp.int32, (C, PW), 1)
    ti = lax.broadcasted_iota(jnp.int32, (C, PW), 0)
    si = lane & (HEAD_DIM - 1)
    first_head = lane < HEAD_DIM
    tri_incl = ti >= si
    tri_strict = ti > si
    eye = jnp.where(ti == si, 1.0, 0.0)
    rows2 = lax.broadcasted_iota(jnp.int32, (2 * C, PW), 0)
    lanes2 = lax.broadcasted_iota(jnp.int32, (2 * C, PW), 1)
    same_head = (rows2 < C) == (lanes2 < HEAD_DIM)

    def bd(x):
        zero = jnp.zeros_like(x)
        return jnp.concatenate([jnp.where(first_head, x, zero), jnp.where(first_head, zero, x)], axis=0)

    def dot_bd(x, y):
        return _dot(x.astype(BF16), bd(y.astype(BF16)))

    items = [(c, p) for c in range(n_sub) for p in range(n_pairs)]
    tile = lambda t, c, p: t[C * c:C * (c + 1), PW * p:PW * (p + 1)]
    ar = {i: jnp.concatenate([tile(a_t, *i), tile(r_t, *i)], axis=0).astype(BF16) for i in items}
    bk = {i: jnp.concatenate([tile(b_t, *i), tile(k_t, *i)], axis=0).astype(BF16) for i in items}
    gm = {i: _dot_nt(ar[i], jnp.concatenate([bd(bk[i][0:C]), bd(bk[i][C:2 * C])], axis=0)) for i in items}
    m_ab = {i: jnp.where(tri_strict, gm[i][0:C, 0:PW], 0.0) for i in items}
    m_ak = {i: jnp.where(tri_strict, gm[i][0:C, PW:2 * PW], 0.0).astype(BF16) for i in items}
    n_r = {i: jnp.concatenate([jnp.where(tri_incl, gm[i][C:2 * C, 0:PW], 0.0),
                               jnp.where(tri_incl, gm[i][C:2 * C, PW:2 * PW], 0.0)], axis=1).astype(BF16)
           for i in items}
    t_inv = {i: eye + m_ab[i] for i in items}
    mp = m_ab
    for _ in range(n_double):
        mp = {i: dot_bd(mp[i], mp[i]) for i in items}
        t_inv = {i: t_inv[i] + dot_bd(t_inv[i], mp[i]) for i in items}
    v_b = {i: tile(v, *i).astype(BF16) for i in items}
    v_bd = {i: bd(v_b[i]) for i in items}
    mv = {i: _dot(m_ak[i], v_bd[i]) for i in items}
    state = [s_scr[p] for p in range(n_pairs)]
    o_rows = []
    for c in range(n_sub):
        its = [(c, p) for p in range(n_pairs)]
        ars = {i: _dot_nt(ar[i], state[i[1]].astype(BF16)) for i in its}
        u_b = {i: dot_bd(t_inv[i], ars[i][0:C] + mv[i]).astype(BF16) for i in its}
        o = [ars[i][C:2 * C] + _dot(n_r[i], jnp.concatenate([bd(u_b[i]), v_bd[i]], axis=0)) for i in its]
        o_rows.append(jnp.concatenate(o, axis=1))
        for i in its:
            upd = jnp.where(same_head, _dot_tn(jnp.concatenate([u_b[i], v_b[i]], axis=0), bk[i]), 0.0)
            state[i[1]] = (state[i[1]] + upd) * e_l[C * (c + 1) - 1:C * (c + 1), PW * i[1]:PW * (i[1] + 1)]
    for p in range(n_pairs):
        s_scr[p] = state[p]
        sout_ref[p] = state[p]
    _rwkv_epilogue(jnp.concatenate(o_rows, axis=0), bonus, g, e, lg_ref, lb_ref, rw_ref)


def _rwkv(P, shift0, s0, lw, C):
    B, T, _ = P.shape
    assert C == HEAD_DIM, "the pair-packed chunk kernel needs chunk length == head dim"
    rows = C * RWKV_CHUNKS_PER_STEP
    assert T % rows == 0
    n_double = max(int(np.log2(C)) - 1, 0)
    W = RWKV_W
    e = np.kron(np.eye(2, dtype=np.float32), np.ones((HEAD_DIM, HEAD_DIM), np.float32))
    zpad = jnp.zeros((DECAY_LORA, W), F32)
    w2p = jnp.concatenate([lw["w2"], jnp.zeros((AAA_LORA, W), F32)], axis=0).astype(BF16)
    a2p = jnp.concatenate([zpad, lw["a2"]], axis=0).astype(BF16)
    row = lambda t: t.reshape(1, -1)
    consts = [row(lw["mu"]), row(lw["w0"]), w2p, row(lw["a0"]), a2p, lw["g2"].astype(BF16),
              row(lw["k_k"]), row(lw["k_a"]), row(lw["r_k"]), row(lw["lnx_g"]), row(lw["lnx_b"]),
              jnp.asarray(e, BF16)]
    sp = s0.reshape(B, RWKV_HEADS // 2, 2, HEAD_DIM, HEAD_DIM)
    z = jnp.zeros_like(sp[:, :, 0])
    s0 = jnp.concatenate([jnp.concatenate([sp[:, :, 0], z], axis=-1),
                          jnp.concatenate([z, sp[:, :, 1]], axis=-1)], axis=-2)
    st_shape = s0.shape[1:]
    const = lambda arr: pl.BlockSpec(arr.shape, lambda b, c: (0, 0))
    st_spec = pl.BlockSpec((None,) + st_shape, lambda b, c: (b, 0, 0, 0))
    sh_spec = pl.BlockSpec((None, 1, RWKV_IN), lambda b, c: (b, 0, 0))
    rw, s_out, sh_out = pl.pallas_call(
        functools.partial(_rwkv_pair_kernel, n_double=n_double),
        grid=(B, T // rows),
        in_specs=[pl.BlockSpec((None, rows, RWKV_IN), lambda b, c: (b, c, 0)), sh_spec, st_spec]
                 + [const(t) for t in consts],
        out_specs=[pl.BlockSpec((None, rows, W), lambda b, c: (b, c, 0)), st_spec, sh_spec],
        out_shape=[jax.ShapeDtypeStruct((B, T, W), BF16),
                   jax.ShapeDtypeStruct((B,) + st_shape, F32),
                   jax.ShapeDtypeStruct((B, 1, RWKV_IN), F32)],
        scratch_shapes=[pltpu.VMEM(st_shape, F32),
                        pltpu.VMEM((1, RWKV_IN), F32)],
        compiler_params=_params("arbitrary", "arbitrary"),
        name=f"rwkv_c{C}",
    )(P, shift0.reshape(B, 1, RWKV_IN), s0, *consts)
    s_out = jnp.stack([s_out[:, :, :HEAD_DIM, :HEAD_DIM], s_out[:, :, HEAD_DIM:, HEAD_DIM:]], axis=2)
    s_out = s_out.reshape(B, RWKV_HEADS, HEAD_DIM, HEAD_DIM)
    return rw, s_out, sh_out.reshape(B, RWKV_IN)


def _rwkv_scan_kernel(r_ref, k_ref, v_ref, lo_ref, sr_ref, sk_ref, sv_ref, slo_ref, s_ref,
                      mur_ref, muk_ref, muv_ref, mulo_ref, w0_ref, w2t_ref, a0_ref, a2t_ref, g2t_ref,
                      kk_ref, ka_ref, rk_ref, lg_ref, lb_ref,
                      rw_ref, sout_ref,
                      w_scr, nkk_scr, bb_scr, km_scr, rr_scr, vv_scr, bonus_scr, g_scr, o_scr):
    T = r_ref.shape[0]

    def shifted(ref, s0_ref, mu_ref, t):
        cur = ref[t]
        prev = s0_ref[...] if t == 0 else ref[t - 1]
        return cur + (prev - cur) * mu_ref[...]

    for t in range(T):
        r = shifted(r_ref, sr_ref, mur_ref, t)
        k = shifted(k_ref, sk_ref, muk_ref, t)
        v = shifted(v_ref, sv_ref, muv_ref, t)
        lo = shifted(lo_ref, slo_ref, mulo_ref, t)
        wl, al, gl = lo[0:DECAY_LORA], lo[DECAY_LORA:DECAY_LORA + AAA_LORA], lo[DECAY_LORA + AAA_LORA:]
        neg = -(w0_ref[...] + _dot(w2t_ref[...], jnp.tanh(wl).astype(BF16)))
        softplus = jnp.maximum(neg, 0.0) + jnp.log(1.0 + jnp.exp(-jnp.abs(neg)))
        w = jnp.exp(-jnp.exp(-softplus - 0.5))
        a = jax.nn.sigmoid(a0_ref[...] + _dot(a2t_ref[...], al.astype(BF16)))
        g_scr[t] = _dot(g2t_ref[...], jax.nn.sigmoid(gl).astype(BF16))
        kkr = k * kk_ref[...]
        kk = kkr / jnp.maximum(jnp.sqrt(jnp.sum(kkr * kkr, axis=0, keepdims=True)), 1e-12)
        kmod = k * (1.0 + (a - 1.0) * ka_ref[...])
        bonus_scr[t] = jnp.sum(r * kmod * rk_ref[...], axis=0, keepdims=True) * v
        w_scr[t] = w
        nkk_scr[t] = -kk
        bb_scr[t] = kk * a
        km_scr[t] = kmod
        rr_scr[t] = r
        vv_scr[t] = v

    def row(i, carry):
        s = s_ref[i]
        for t in range(T):
            sa = jnp.sum(s * nkk_scr[t], axis=0, keepdims=True)
            s = s * w_scr[t] + sa * bb_scr[t] + vv_scr[t, pl.ds(i, 1), :] * km_scr[t]
            o_scr[t, pl.ds(i, 1), :] = jnp.sum(s * rr_scr[t], axis=0, keepdims=True)
        sout_ref[i] = s
        return carry

    lax.fori_loop(0, HEAD_DIM, row, 0)
    o = o_scr[...]
    mean = jnp.mean(o, axis=1, keepdims=True)
    d = o - mean
    var = jnp.mean(d * d, axis=1, keepdims=True)
    on = d * lax.rsqrt(var + LNX_EPS) * lg_ref[...] + lb_ref[...]
    rw_ref[...] = ((on + bonus_scr[...]) * g_scr[...]).astype(rw_ref.dtype)


def _rwkv_scan(pt, shift0, s0t, lw):
    T, _, DB = pt.shape
    W, HD = RWKV_W, HEAD_DIM
    lora_w = DECAY_LORA + AAA_LORA + GATE_LORA
    assert (3 * W) % lora_w == 0
    col = lambda t: jnp.broadcast_to(t.reshape(-1, 1), (t.size, DB))
    shift_t = shift0.T
    mu_c = col(lw["mu"])
    head = lambda base: (lambda h: (base + h, 0))
    pt_blk = lambda base: pl.BlockSpec((T, HD, DB), lambda h: (0, base + h, 0))
    blk2 = lambda base: pl.BlockSpec((HD, DB), head(base))
    lo_pt = pl.BlockSpec((T, lora_w, DB), lambda h: (0, 3 * W // lora_w, 0))
    lo_2d = pl.BlockSpec((lora_w, DB), lambda h: (3 * W // lora_w, 0))
    nh = W // HD
    st_spec = pl.BlockSpec((None, HD, HD, DB), lambda h: (h, 0, 0, 0))
    wt_spec = lambda n: pl.BlockSpec((HD, n), lambda h: (h, 0))
    scr = lambda: pltpu.VMEM((T, HD, DB), F32)
    return pl.pallas_call(
        _rwkv_scan_kernel,
        grid=(nh,),
        in_specs=[pt_blk(0), pt_blk(nh), pt_blk(2 * nh), lo_pt,
                  blk2(0), blk2(nh), blk2(2 * nh), lo_2d, st_spec,
                  blk2(0), blk2(nh), blk2(2 * nh), lo_2d,
                  blk2(0), wt_spec(DECAY_LORA), blk2(0), wt_spec(AAA_LORA), wt_spec(GATE_LORA),
                  blk2(0), blk2(0), blk2(0), blk2(0), blk2(0)],
        out_specs=[pl.BlockSpec((T, HD, DB), lambda h: (0, h, 0)), st_spec],
        out_shape=[jax.ShapeDtypeStruct((T, W, DB), F32), jax.ShapeDtypeStruct(s0t.shape, F32)],
        scratch_shapes=[scr() for _ in range(9)],
        compiler_params=_params("arbitrary"),
        name="rwkv_scan",
    )(pt, pt, pt, pt, shift_t, shift_t, shift_t, shift_t, s0t,
      mu_c, mu_c, mu_c, mu_c, col(lw["w0"]), lw["w2"].T.astype(BF16), col(lw["a0"]), lw["a2"].T.astype(BF16),
      lw["g2"].T.astype(BF16), col(lw["k_k"]), col(lw["k_a"]), col(lw["r_k"]), col(lw["lnx_g"]), col(lw["lnx_b"]))


def _mlp_kernel(att_ref, rw_ref, x_ref, gt1_ref, sc2_ref, sh2_ref, gt2_ref, g2_ref, gf_ref,
                wo_ref, w1_ref, w2_ref, y_ref, *, final_norm, ff_chunk):
    half = wo_ref.shape[0] // 2
    mix = _dot(att_ref[...], wo_ref[0:half, :]) + _dot(rw_ref[...], wo_ref[half:, :])
    x1 = x_ref[...] + gt1_ref[...] * mix
    ms = jnp.mean(x1 * x1, axis=-1, keepdims=True)
    h2 = x1 * lax.rsqrt(ms + NORM_EPS) * g2_ref[...]
    h2 = (h2 * (1.0 + sc2_ref[...]) + sh2_ref[...]).astype(BF16)
    acc = jnp.zeros_like(x1)
    for c in range(w1_ref.shape[1] // ff_chunk):
        hid = _dot(h2, w1_ref[:, c * ff_chunk:(c + 1) * ff_chunk])
        hid = jnp.square(jnp.maximum(hid, 0.0)).astype(BF16)
        acc = acc + _dot(hid, w2_ref[c * ff_chunk:(c + 1) * ff_chunk, :])
    x2 = x1 + gt2_ref[...] * acc
    if final_norm:
        ms2 = jnp.mean(x2 * x2, axis=-1, keepdims=True)
        x2 = x2 * lax.rsqrt(ms2 + NORM_EPS) * gf_ref[...]
    y_ref[...] = x2


def _mlp(att, rw, x, gt1, sc2, sh2, gt2, g2, gf, wo_b, w1_b, w2_b, tm, per_row_mod, final_norm):
    G, R, D = x.shape
    dff = w1_b.shape[1]
    row_spec = lambda w: pl.BlockSpec((None, tm, w), lambda b, i: (b, i, 0))
    if per_row_mod:
        mod_spec = pl.BlockSpec((None, tm, D), lambda b, i: (b, i, 0))
    else:
        mod_spec = pl.BlockSpec((None, 1, D), lambda b, i: (b, 0, 0))
    const = lambda shape: pl.BlockSpec(shape, lambda b, i: (0, 0), pipeline_mode=pl.Buffered(1))
    return pl.pallas_call(
        functools.partial(_mlp_kernel, final_norm=final_norm, ff_chunk=1024),
        grid=(G, R // tm),
        in_specs=[row_spec(att.shape[-1]), row_spec(rw.shape[-1]), row_spec(D),
                  mod_spec, mod_spec, mod_spec, mod_spec, const((1, D)), const((1, D)),
                  const(wo_b.shape), const((D, dff)), const((dff, D))],
        out_specs=row_spec(D),
        out_shape=jax.ShapeDtypeStruct((G, R, D), F32),
        compiler_params=_params("arbitrary", "arbitrary"),
        name="mlp",
    )(att, rw, x, gt1, sc2, sh2, gt2, g2.reshape(1, D), gf.reshape(1, D), wo_b, w1_b, w2_b)


def kernel(x_prompt, x_sample, cache_k, cache_v, state_wkv, state_shift, c_prompt, c_sample, norm1_g, norm2_g, w_ada, b_ada, w_in, w_out, mu, w0, w2, a0, a2, g2, k_k, k_a, r_k, lnx_g, lnx_b, w_ff1, w_ff2, normf_g):
    B, S, D = x_prompt.shape
    DB, T, _ = x_sample.shape
    depth = w_in.shape[0]
    dt = x_prompt.dtype
    tabs_p = _rope_tables(jnp.arange(S))
    tabs_s = tuple(jnp.tile(t, (DB, 1)) for t in _rope_tables(PAST_LEN + jnp.arange(T)))
    tabs_t = tuple(jnp.repeat(t, DB, axis=0) for t in _rope_tables(PAST_LEN + jnp.arange(T)))
    c_all = jnp.concatenate([c_prompt, c_sample], axis=0)
    hp = x_prompt
    hs = x_sample.reshape(1, DB * T, D)
    outs = [[] for _ in range(8)]
    tm_s = min(ROW_TILE, DB * T)
    for l in range(depth):
        last = l == depth - 1
        mod = _ada(c_all, w_ada[l], b_ada[l])
        mod_p = [m.reshape(B, 1, D) for m in jnp.split(mod[:B], 6, axis=-1)]
        mod_s = [jnp.repeat(m, T, axis=0).reshape(1, DB * T, D) for m in jnp.split(mod[B:], 6, axis=-1)]
        mod_t = [m.reshape(1, DB, D) for m in jnp.split(mod[B:], 6, axis=-1)]
        w_in_b, wo_b = w_in[l].astype(BF16), w_out[l].astype(BF16)
        w1_b, w2_b = w_ff1[l].astype(BF16), w_ff2[l].astype(BF16)
        lw = dict(mu=mu[l], w0=w0[l], w2=w2[l], a0=a0[l], a2=a2[l], g2=g2[l], k_k=k_k[l], k_a=k_a[l],
                  r_k=r_k[l], lnx_g=lnx_g[l], lnx_b=lnx_b[l])

        sh1, sc1, gt1, sh2, sc2, gt2 = mod_p
        q, k, v, P, kt, vt = _inproj(hp, sc1, sh1, norm1_g[l], w_in_b, tabs_p, ROW_TILE, "batch", True)
        att = _attn_fused(q, k, v)
        rw, s_p, shift_p = _rwkv(P, jnp.zeros((B, RWKV_IN), F32),
                                 jnp.zeros((B, RWKV_HEADS, HEAD_DIM, HEAD_DIM), F32), lw, RWKV_CHUNK)
        hp = _mlp(att, rw, hp, gt1, sc2, sh2, gt2, norm2_g[l], normf_g, wo_b, w1_b, w2_b, ROW_TILE, False, last)
        keep = min(max(w for w, _ in DIL_BRANCHES), S)
        win = lambda t: jnp.transpose(t.reshape(B, ATT_HEADS, HEAD_DIM, S)[..., S - keep:], (0, 3, 1, 2))
        outs[0].append(win(kt))
        outs[1].append(win(vt))
        outs[2].append(s_p)
        outs[3].append(shift_p)

        sh1, sc1, gt1, sh2, sc2, gt2 = mod_s
        q, k, v, P = _inproj(hs, sc1, sh1, norm1_g[l], w_in_b, tabs_s, tm_s, "row", False)
        hs_t = jnp.transpose(hs.reshape(DB, T, D), (1, 0, 2)).reshape(1, T * DB, D)
        *_, kt, vt, pt = _inproj(hs_t, mod_t[1], mod_t[0], norm1_g[l], w_in_b, tabs_t, DB, "tile", True, nq=T, p_t=True)
        att = _sample_attn(q.reshape(DB, T, ATT_W), k.reshape(DB, T, ATT_W), v.reshape(DB, T, ATT_W),
                           jnp.transpose(cache_k[l], (0, 2, 3, 1)), jnp.transpose(cache_v[l], (0, 2, 3, 1)))
        rw_t, s_t = _rwkv_scan(pt.reshape(T, RWKV_IN, DB), state_shift[l],
                               jnp.transpose(state_wkv[l], (1, 2, 3, 0)), lw)
        rw = jnp.transpose(rw_t, (2, 0, 1)).astype(BF16)
        s_s = jnp.transpose(s_t, (3, 0, 1, 2))
        shift_s = P.reshape(DB, T, RWKV_IN)[:, T - 1]
        hs = _mlp(att.reshape(1, DB * T, ATT_W), rw.reshape(1, DB * T, RWKV_W), hs, gt1, sc2, sh2, gt2,
                  norm2_g[l], normf_g, wo_b, w1_b, w2_b, tm_s, True, last)
        new = lambda t: jnp.transpose(t.reshape(T, ATT_HEADS, HEAD_DIM, DB), (3, 0, 1, 2))
        outs[4].append(new(kt))
        outs[5].append(new(vt))
        outs[6].append(s_s)
        outs[7].append(shift_s)
    stack = lambda xs: jnp.stack(xs).astype(dt)
    return (hp.astype(dt), hs.reshape(DB, T, D).astype(x_sample.dtype), *[stack(o) for o in outs])
```

```python
import functools

import numpy as np
import jax
import jax.numpy as jnp
from jax import lax
from jax.experimental import pallas as pl
from jax.experimental.pallas import tpu as pltpu

F32 = jnp.float32
BF16 = jnp.bfloat16

HEAD_DIM = 64
ATT_HEADS = 8
ATT_W = ATT_HEADS * HEAD_DIM
RWKV_HEADS = 8
RWKV_W = RWKV_HEADS * HEAD_DIM
DIL_BRANCHES = ((128, 1), (512, 4), (2048, 16))
ROT_DIM = HEAD_DIM // 4
ROPE_THETA = 500000.0
DECAY_LORA = 64
AAA_LORA = 64
GATE_LORA = 128
RWKV_IN = 3 * RWKV_W + DECAY_LORA + AAA_LORA + GATE_LORA
NORM_EPS = 1e-6
LNX_EPS = 64e-5
PAST_LEN = 8192
ATT_BLOCK = 128
NEG_BIG = -1e30
LOG2_E = 1.4426950408889634
VMEM_LIMIT = 56 * 1024 * 1024
ROW_TILE = 512
RWKV_CHUNK = 64
RWKV_CHUNKS_PER_STEP = 4
ATT_GROUP = 4


def _dot(a, b):
    return jnp.dot(a, b, preferred_element_type=F32)


def _dot_nt(a, b):
    return lax.dot_general(a, b, (((1,), (1,)), ((), ())), preferred_element_type=F32)


def _dot_tn(a, b):
    return lax.dot_general(a, b, (((0,), (0,)), ((), ())), preferred_element_type=F32)


def _split2(x):
    hi = x.astype(BF16)
    lo = (x - hi.astype(F32)).astype(BF16)
    return hi, lo


def _head_sums(x, e):
    C, W = x.shape
    n = W // e.shape[0]
    xh, xl = _split2(x)
    tiles = [t[:, e.shape[0] * p:e.shape[0] * (p + 1)] for t in (xh, xl) for p in range(n)]
    o = _dot(jnp.concatenate(tiles, axis=0), e)
    return jnp.concatenate([o[C * p:C * (p + 1)] + o[C * (n + p):C * (n + p + 1)] for p in range(n)], axis=1)


def _params(*sem):
    return pltpu.CompilerParams(dimension_semantics=sem, vmem_limit_bytes=VMEM_LIMIT)


def _ada_kernel(c_ref, w_ref, b_ref, o_ref):
    c = c_ref[...]
    s = c * jax.nn.sigmoid(c)
    o_ref[...] = _dot(s.astype(BF16), w_ref[...].astype(BF16)) + b_ref[...]


def _ada(c, w_ada, b_ada):
    n, d = c.shape
    cols = w_ada.shape[1]
    tn = 512
    return pl.pallas_call(
        _ada_kernel,
        grid=(cols // tn,),
        in_specs=[pl.BlockSpec((n, d), lambda j: (0, 0)),
                  pl.BlockSpec((d, tn), lambda j: (0, j)),
                  pl.BlockSpec((1, tn), lambda j: (0, j))],
        out_specs=pl.BlockSpec((n, tn), lambda j: (0, j)),
        out_shape=jax.ShapeDtypeStruct((n, cols), F32),
        compiler_params=_params("arbitrary"),
        name="ada",
    )(c, w_ada, b_ada.reshape(1, cols))


def _rope(t, cos, s1, s2):
    parts = []
    for c in range(ATT_W // 128):
        tc = t[:, 128 * c:128 * (c + 1)]
        parts.append(tc * cos + pltpu.roll(tc, 128 - ROT_DIM // 2, 1) * s1 + pltpu.roll(tc, ROT_DIM // 2, 1) * s2)
    return jnp.concatenate(parts, axis=1)


def _inproj_kernel(x_ref, sc_ref, sh_ref, g_ref, w_ref, cos_ref, s1_ref, s2_ref, *out_refs,
                   transposed, p_t):
    q_ref, k_ref, v_ref, p_ref = out_refs[:4]
    rest = list(out_refs[4:])
    x = x_ref[...]
    ms = jnp.mean(x * x, axis=-1, keepdims=True)
    h = x * lax.rsqrt(ms + NORM_EPS) * g_ref[...]
    h = (h * (1.0 + sc_ref[...]) + sh_ref[...]).astype(BF16)
    cos, s1, s2 = cos_ref[...], s1_ref[...], s2_ref[...]
    q = _rope(_dot(h, w_ref[:, 0:ATT_W]), cos, s1, s2)
    q_ref[...] = (q * HEAD_DIM ** -0.5).astype(q_ref.dtype)
    k = _rope(_dot(h, w_ref[:, ATT_W:2 * ATT_W]), cos, s1, s2)
    v = _dot(h, w_ref[:, 2 * ATT_W:3 * ATT_W])
    k_ref[...] = k
    v_ref[...] = v
    p = _dot(h, w_ref[:, 3 * ATT_W:])
    p_ref[...] = p
    if transposed:
        rest.pop(0)[...] = k.T
        rest.pop(0)[...] = v.T
    if p_t:
        rest.pop(0)[...] = p.T


def _inproj(x, sc, sh, g, w_in_b, tabs, tm, mod_mode, transposed, nq=1, p_t=False):
    G, R, D = x.shape
    nin = w_in_b.shape[1]
    row_spec = lambda w: pl.BlockSpec((None, tm, w), lambda b, i: (b, i, 0))
    mod_spec = {"batch": pl.BlockSpec((None, 1, D), lambda b, i: (b, 0, 0)),
                "row": pl.BlockSpec((None, tm, D), lambda b, i: (b, i, 0)),
                "tile": pl.BlockSpec((None, tm, D), lambda b, i: (b, 0, 0))}[mod_mode]
    tab_spec = pl.BlockSpec((tm, 128), lambda b, i: (i, 0))
    const = lambda shape: pl.BlockSpec(shape, lambda b, i: (0, 0))
    out_specs = [row_spec(ATT_W)] * 3 + [row_spec(RWKV_IN)]
    out_shape = [jax.ShapeDtypeStruct((G, R, ATT_W), F32)] * 3 + [jax.ShapeDtypeStruct((G, R, RWKV_IN), F32)]
    if transposed:
        spq = (R // nq) // tm
        t_spec = pl.BlockSpec((None, None, ATT_W, tm), lambda b, i: (b, i // spq, 0, i % spq))
        out_specs += [t_spec] * 2
        out_shape += [jax.ShapeDtypeStruct((G, nq, ATT_W, R // nq), F32)] * 2
        if p_t:
            out_specs.append(pl.BlockSpec((None, None, RWKV_IN, tm), lambda b, i: (b, i // spq, 0, i % spq)))
            out_shape.append(jax.ShapeDtypeStruct((G, nq, RWKV_IN, R // nq), F32))
    return pl.pallas_call(
        functools.partial(_inproj_kernel, transposed=transposed, p_t=p_t),
        grid=(G, R // tm),
        in_specs=[row_spec(D), mod_spec, mod_spec, const((1, D)), const((D, nin)),
                  tab_spec, tab_spec, tab_spec],
        out_specs=out_specs,
        out_shape=out_shape,
        compiler_params=_params("arbitrary", "arbitrary"),
        name="inproj",
    )(x, sc, sh, g.reshape(1, D), w_in_b, *tabs)


def _rope_tables(pos):
    half = ROT_DIM // 2
    inv = ROPE_THETA ** (-jnp.arange(half, dtype=F32) * (2.0 / ROT_DIM))
    ang = pos.astype(F32)[:, None] * inv[None, :]
    cos, sin = jnp.cos(ang), jnp.sin(ang)
    n = pos.shape[0]
    ones = jnp.ones((n, HEAD_DIM - ROT_DIM), F32)
    zeros = jnp.zeros((n, HEAD_DIM - ROT_DIM), F32)
    zh = jnp.zeros((n, half), F32)
    ctab = jnp.concatenate([cos, cos, ones], axis=1)
    s1 = jnp.concatenate([-sin, zh, zeros], axis=1)
    s2 = jnp.concatenate([zh, sin, zeros], axis=1)
    return tuple(jnp.tile(t, (1, 128 // HEAD_DIM)) for t in (ctab, s1, s2))


def _attn_fused_kernel(q_ref, k_ref, v_ref, o_ref, acc_scr, m_scr, l_scr, *, dilations, group):
    nw, HD = ATT_BLOCK, HEAD_DIM
    S = q_ref.shape[0]
    n_tiles = S // nw
    lane = lax.broadcasted_iota(jnp.int32, (nw, 2 * HD), 1)
    first_head = lane < HD
    q_idx = lax.broadcasted_iota(jnp.int32, (nw, 2 * HD), 0)
    n_col = nw // HD
    key_idx = [HD * jc + (lane & (HD - 1)) for jc in range(n_col)]
    cur_ok = [q_idx >= key_idx[jc] for jc in range(n_col)]

    def bd(x):
        zero = jnp.zeros_like(x)
        xa, xb = jnp.where(first_head, x, zero), jnp.where(first_head, zero, x)
        parts = []
        for jc in range(n_col):
            parts += [xa[HD * jc:HD * (jc + 1)], xb[HD * jc:HD * (jc + 1)]]
        return jnp.concatenate(parts, axis=0)

    bd_row = lax.broadcasted_iota(jnp.int32, (2 * nw, 2 * HD), 0)
    bd_lane = lax.broadcasted_iota(jnp.int32, (2 * nw, 2 * HD), 1)
    ones_bd = jnp.where(((bd_row // HD) % 2 == 0) == (bd_lane < HD), 1.0, 0.0).astype(BF16)

    def with_ones(v):
        return jnp.concatenate([bd(v.astype(BF16)), ones_bd], axis=1)

    def half_reduce(x, op, fill):
        ra = op(jnp.where(first_head, x, fill), axis=-1, keepdims=True)
        rb = op(jnp.where(first_head, fill, x), axis=-1, keepdims=True)
        return jnp.where(first_head, ra, rb)

    def steps(tiles, use_prev, init, final):
        n = range(len(tiles))
        qs = [(q_ref[tiles[t][0], :] * LOG2_E).astype(BF16) for t in n]
        cols, vbd = [], []
        for t in n:
            rows, prev_rows, has_prev = tiles[t]
            s = _dot_nt(qs[t], bd(k_ref[rows, :].astype(BF16)))
            c = [jnp.where(cur_ok[jc], s[:, 2 * HD * jc:2 * HD * (jc + 1)], NEG_BIG) for jc in range(n_col)]
            vb = [with_ones(v_ref[rows, :])]
            if use_prev:
                sp = _dot_nt(qs[t], bd(k_ref[prev_rows, :].astype(BF16)))
                thresh = jnp.where(has_prev, 0, nw)
                c += [jnp.where((key_idx[jc] - q_idx) >= thresh, sp[:, 2 * HD * jc:2 * HD * (jc + 1)], NEG_BIG)
                      for jc in range(n_col)]
                vb.append(with_ones(v_ref[prev_rows, :]))
            cols.append(c)
            vbd.append(vb)
        m_new, m_old, p = [], [], []
        for t in n:
            m = half_reduce(functools.reduce(jnp.maximum, cols[t]), jnp.max, NEG_BIG)
            if not init:
                m_old.append(m_scr[tiles[t][0], :])
                m = jnp.maximum(m, m_old[t])
            m_new.append(m)
            p.append([jnp.exp2(c - m) for c in cols[t]])
        acc = []
        for t in n:
            a = _dot(jnp.concatenate(p[t][0:n_col], axis=1).astype(BF16), vbd[t][0])
            if use_prev:
                a = a + _dot(jnp.concatenate(p[t][n_col:], axis=1).astype(BF16), vbd[t][1])
            acc.append(a)
        for t in n:
            rows = tiles[t][0]
            a, l = acc[t][:, 0:2 * HD], acc[t][:, 2 * HD:]
            if not init:
                alpha = jnp.exp2(m_old[t] - m_new[t])
                a = a + acc_scr[rows, :] * alpha
                l = l + l_scr[rows, :] * alpha
            if final:
                o_ref[rows, :] = (a / l).astype(o_ref.dtype)
            else:
                acc_scr[rows, :] = a
                m_scr[rows, :] = m_new[t]
                l_scr[rows, :] = l

    n_br = len(dilations)
    for bi, dil in enumerate(dilations):
        init, final = bi == 0, bi == n_br - 1
        use_prev = S // (dil * nw) > 1

        def group_body(g, carry, dil=dil, use_prev=use_prev, init=init, final=final):
            tiles = []
            for i in range(group):
                t = g * group + i
                r, j = lax.rem(t, dil), lax.div(t, dil)
                start = r + j * (dil * nw)
                prev = r + jnp.maximum(j - 1, 0) * (dil * nw)
                if dil > 1:
                    tiles.append((pl.ds(start, nw, stride=dil), pl.ds(prev, nw, stride=dil), j > 0))
                else:
                    tiles.append((pl.ds(pl.multiple_of(start, nw), nw), pl.ds(pl.multiple_of(prev, nw), nw), j > 0))
            steps(tiles, use_prev, init, final)
            return carry

        lax.fori_loop(0, n_tiles // group, group_body, 0, unroll=True)


def _attn_fused(q, k, v):
    B, S, W = q.shape
    nw = ATT_BLOCK
    dilations = tuple(sorted((d for _, d in DIL_BRANCHES), reverse=True))
    for win, d in DIL_BRANCHES:
        assert win == d * nw and S % (d * nw) == 0
    assert (S // nw) % ATT_GROUP == 0
    PW = 2 * HEAD_DIM
    spec = pl.BlockSpec((None, S, PW), lambda b, p: (b, 0, p))
    return pl.pallas_call(
        functools.partial(_attn_fused_kernel, dilations=dilations, group=ATT_GROUP),
        grid=(B, W // PW),
        in_specs=[spec, spec, spec],
        out_specs=spec,
        out_shape=jax.ShapeDtypeStruct((B, S, W), BF16),
        scratch_shapes=[pltpu.VMEM((S, PW), F32) for _ in range(3)],
        compiler_params=_params("arbitrary", "arbitrary"),
        name="attn_fused",
    )(q, k, v)


def _sample_counts(T, WB):
    i = np.arange(T)
    c = np.arange(WB)
    delta = WB + i[None, :] - c[:, None]
    cnt_c = np.zeros((WB, T), np.float32)
    diff = i[None, :] - i[:, None]
    cnt_n = np.zeros((T, T), np.float32)
    for win, dil in DIL_BRANCHES:
        cnt_c += ((delta > 0) & (delta % dil == 0) & (delta <= win)).astype(np.float32)
        cnt_n += ((diff >= 0) & (diff % dil == 0) & (diff <= win)).astype(np.float32)
    return cnt_c, cnt_n


def _sample_attn_kernel(q_ref, kn_ref, vn_ref, kt_ref, vt_ref, cc_ref, cn_ref, o_ref):
    cc, cn = cc_ref[...], cn_ref[...]
    nb = q_ref.shape[0]
    items = [(b, h) for b in range(nb) for h in range(ATT_HEADS)]
    sl = lambda h: slice(HEAD_DIM * h, HEAD_DIM * (h + 1))
    qh = {(b, h): q_ref[b, :, sl(h)] for b, h in items}
    s_c = {i: jnp.where(cc > 0, _dot(qh[i].astype(BF16), kt_ref[i[0], i[1]].astype(BF16)), NEG_BIG)
           for i in items}
    s_n = {i: jnp.where(cn > 0, _dot_nt(qh[i], kn_ref[i[0], :, sl(i[1])]), NEG_BIG) for i in items}
    p_c, p_n, l = {}, {}, {}
    for i in items:
        m = jnp.maximum(jnp.max(s_c[i], axis=-1, keepdims=True), jnp.max(s_n[i], axis=-1, keepdims=True))
        p_c[i] = cc * jnp.exp(s_c[i] - m)
        p_n[i] = cn * jnp.exp(s_n[i] - m)
        l[i] = jnp.sum(p_c[i], axis=-1, keepdims=True) + jnp.sum(p_n[i], axis=-1, keepdims=True)
    acc = {i: _dot_nt(p_c[i].astype(BF16), vt_ref[i[0], i[1]].astype(BF16)) + _dot(p_n[i], vn_ref[i[0], :, sl(i[1])])
           for i in items}
    for b in range(nb):
        o_ref[b] = jnp.concatenate([acc[(b, h)] / l[(b, h)] for h in range(ATT_HEADS)], axis=1).astype(o_ref.dtype)


def _sample_attn_operands(q, k_new, v_new, cache_kt, cache_vt, n_steps, step_index):
    DB, T, W = q.shape
    WB = cache_kt.shape[-1]
    assert DB % n_steps == 0, "the requests must split evenly over the host call's grid steps"
    nb = DB // n_steps
    cnt_c, cnt_n = _sample_counts(T, WB)
    cc = jnp.asarray(cnt_c.T)
    cn = jnp.asarray(cnt_n.T)
    new_spec = pl.BlockSpec((nb, T, W), lambda *g: (step_index(*g), 0, 0))
    cache_spec = pl.BlockSpec((nb, ATT_HEADS, HEAD_DIM, WB), lambda *g: (step_index(*g), 0, 0, 0))
    const = lambda arr: pl.BlockSpec(arr.shape, lambda *g: (0, 0))
    args = (q, k_new, v_new, cache_kt, cache_vt, cc, cn)
    in_specs = [new_spec, new_spec, new_spec, cache_spec, cache_spec, const(cc), const(cn)]
    return args, in_specs, new_spec, jax.ShapeDtypeStruct((DB, T, W), BF16)


def _rwkv_prologue(p_ref, shift_ref, prev_scr, shout_ref, mu_ref, w0_ref, w2_ref, a0_ref, a2_ref, g2_ref,
                   kk_ref, ka_ref, rk_ref, e, chunk):
    C = p_ref.shape[0]
    P = p_ref[...]
    row = lax.broadcasted_iota(jnp.int32, (C, 1), 0)
    p_prev = jnp.where(row == 0, prev_scr[...], pltpu.roll(P, 1, 0))
    pm = P + (p_prev - P) * mu_ref[...]
    prev_scr[...] = P[C - 1:C, :]
    shout_ref[...] = P[C - 1:C, :]

    W = RWKV_W
    r, k, v = pm[:, 0:W], pm[:, W:2 * W], pm[:, 2 * W:3 * W]
    lora_wa = pm[:, 3 * W:3 * W + DECAY_LORA + AAA_LORA]
    gl = pm[:, 3 * W + DECAY_LORA + AAA_LORA:]
    wpre = w0_ref[...] + _dot(jnp.tanh(lora_wa).astype(BF16), w2_ref[...])
    neg = -wpre
    softplus = jnp.maximum(neg, 0.0) + jnp.log(1.0 + jnp.exp(-jnp.abs(neg)))
    logw = -jnp.exp(-softplus - 0.5)
    a = jax.nn.sigmoid(a0_ref[...] + _dot(lora_wa.astype(BF16), a2_ref[...]))
    g = _dot(jax.nn.sigmoid(gl).astype(BF16), g2_ref[...])
    kkr = k * kk_ref[...]
    nrm = jnp.sqrt(_head_sums(kkr * kkr, e))
    kk = kkr / jnp.maximum(nrm, 1e-12)
    kmod = k * (1.0 + (a - 1.0) * ka_ref[...])
    bonus = _head_sums(r * kmod * rk_ref[...], e) * v

    ti = lax.broadcasted_iota(jnp.int32, (C, C), 0)
    si = lax.broadcasted_iota(jnp.int32, (C, C), 1)
    tri_b = jnp.where((ti >= si) & (lax.div(ti, chunk) == lax.div(si, chunk)), 1.0, 0.0).astype(BF16)
    lw_h = logw.astype(BF16)
    lw_r = logw - lw_h.astype(F32)
    lw_m = lw_r.astype(BF16)
    lw_l = (lw_r - lw_m.astype(F32)).astype(BF16)
    L = _dot(tri_b, lw_h) + (_dot(tri_b, lw_m) + _dot(tri_b, lw_l))
    e_l = jnp.exp(L)
    e_nl = jnp.exp(-L)
    a_t = -kk * jnp.exp(L - logw)
    r_t = r * e_l
    b_t = kk * a * e_nl
    k_t = kmod * e_nl
    return v, a_t, r_t, b_t, k_t, e_l, bonus, g


def _rwkv_epilogue(o, bonus, g, e, lg_ref, lb_ref, rw_ref):
    mean = _head_sums(o, e) * (1.0 / HEAD_DIM)
    d = o - mean
    var = _head_sums(d * d, e) * (1.0 / HEAD_DIM)
    on = d * lax.rsqrt(var + LNX_EPS) * lg_ref[...] + lb_ref[...]
    rw_ref[...] = ((on + bonus) * g).astype(rw_ref.dtype)


def _rwkv_pair_kernel(*refs, n_double, n_side_in):
    (p_ref, shift_ref, s0_ref, mu_ref, w0_ref, w2_ref, a0_ref, a2_ref, g2_ref,
     kk_ref, ka_ref, rk_ref, lg_ref, lb_ref, e_ref) = refs[:15]
    side_in = refs[15:15 + n_side_in]
    rw_ref, sout_ref, shout_ref, side_out, s_scr, prev_scr = refs[15 + n_side_in:]
    _sample_attn_kernel(*side_in, side_out)

    C = HEAD_DIM
    n_sub = p_ref.shape[0] // C
    n_pairs = RWKV_HEADS // 2
    PW = 2 * HEAD_DIM

    @pl.when(pl.program_id(1) == 0)
    def _():
        s_scr[...] = s0_ref[...]
        prev_scr[...] = shift_ref[...]

    e = e_ref[...]
    v, a_t, r_t, b_t, k_t, e_l, bonus, g = _rwkv_prologue(
        p_ref, shift_ref, prev_scr, shout_ref, mu_ref, w0_ref, w2_ref, a0_ref, a2_ref, g2_ref,
        kk_ref, ka_ref, rk_ref, e, C)

    lane = lax.broadcasted_iota(jnp.int32, (C, PW), 1)
    ti = lax.broadcasted_iota(jnp.int32, (C, PW), 0)
    si = lane & (HEAD_DIM - 1)
    first_head = lane < HEAD_DIM
    tri_incl = ti >= si
    tri_strict = ti > si
    eye = jnp.where(ti == si, 1.0, 0.0)
    rows2 = lax.broadcasted_iota(jnp.int32, (2 * C, PW), 0)
    lanes2 = lax.broadcasted_iota(jnp.int32, (2 * C, PW), 1)
    same_head = (rows2 < C) == (lanes2 < HEAD_DIM)

    def bd(x):
        zero = jnp.zeros_like(x)
        return jnp.concatenate([jnp.where(first_head, x, zero), jnp.where(first_head, zero, x)], axis=0)

    def dot_bd(x, y):
        return _dot(x.astype(BF16), bd(y.astype(BF16)))

    items = [(c, p) for c in range(n_sub) for p in range(n_pairs)]
    tile = lambda t, c, p: t[C * c:C * (c + 1), PW * p:PW * (p + 1)]
    ar = {i: jnp.concatenate([tile(a_t, *i), tile(r_t, *i)], axis=0).astype(BF16) for i in items}
    bk = {i: jnp.concatenate([tile(b_t, *i), tile(k_t, *i)], axis=0).astype(BF16) for i in items}
    gm = {i: _dot_nt(ar[i], jnp.concatenate([bd(bk[i][0:C]), bd(bk[i][C:2 * C])], axis=0)) for i in items}
    m_ab = {i: jnp.where(tri_strict, gm[i][0:C, 0:PW], 0.0) for i in items}
    m_ak = {i: jnp.where(tri_strict, gm[i][0:C, PW:2 * PW], 0.0).astype(BF16) for i in items}
    n_r = {i: jnp.concatenate([jnp.where(tri_incl, gm[i][C:2 * C, 0:PW], 0.0),
                               jnp.where(tri_incl, gm[i][C:2 * C, PW:2 * PW], 0.0)], axis=1).astype(BF16)
           for i in items}
    t_inv = {i: eye + m_ab[i] for i in items}
    mp = m_ab
    for _ in range(n_double):
        mp = {i: dot_bd(mp[i], mp[i]) for i in items}
        t_inv = {i: t_inv[i] + dot_bd(t_inv[i], mp[i]) for i in items}
    v_b = {i: tile(v, *i).astype(BF16) for i in items}
    v_bd = {i: bd(v_b[i]) for i in items}
    mv = {i: _dot(m_ak[i], v_bd[i]) for i in items}
    state = [s_scr[p] for p in range(n_pairs)]
    o_rows = []
    for c in range(n_sub):
        its = [(c, p) for p in range(n_pairs)]
        ars = {i: _dot_nt(ar[i], state[i[1]].astype(BF16)) for i in its}
        u_b = {i: dot_bd(t_inv[i], ars[i][0:C] + mv[i]).astype(BF16) for i in its}
        o = [ars[i][C:2 * C] + _dot(n_r[i], jnp.concatenate([bd(u_b[i]), v_bd[i]], axis=0)) for i in its]
        o_rows.append(jnp.concatenate(o, axis=1))
        for i in its:
            upd = jnp.where(same_head, _dot_tn(jnp.concatenate([u_b[i], v_b[i]], axis=0), bk[i]), 0.0)
            state[i[1]] = (state[i[1]] + upd) * e_l[C * (c + 1) - 1:C * (c + 1), PW * i[1]:PW * (i[1] + 1)]
    for p in range(n_pairs):
        s_scr[p] = state[p]
        sout_ref[p] = state[p]
    _rwkv_epilogue(jnp.concatenate(o_rows, axis=0), bonus, g, e, lg_ref, lb_ref, rw_ref)


def _rwkv(P, shift0, s0, lw, C, sample_attn_inputs):
    B, T, _ = P.shape
    assert C == HEAD_DIM, "the pair-packed chunk kernel needs chunk length == head dim"
    rows = C * RWKV_CHUNKS_PER_STEP
    assert T % rows == 0
    n_double = max(int(np.log2(C)) - 1, 0)
    W = RWKV_W
    e = np.kron(np.eye(2, dtype=np.float32), np.ones((HEAD_DIM, HEAD_DIM), np.float32))
    zpad = jnp.zeros((DECAY_LORA, W), F32)
    w2p = jnp.concatenate([lw["w2"], jnp.zeros((AAA_LORA, W), F32)], axis=0).astype(BF16)
    a2p = jnp.concatenate([zpad, lw["a2"]], axis=0).astype(BF16)
    row = lambda t: t.reshape(1, -1)
    consts = [row(lw["mu"]), row(lw["w0"]), w2p, row(lw["a0"]), a2p, lw["g2"].astype(BF16),
              row(lw["k_k"]), row(lw["k_a"]), row(lw["r_k"]), row(lw["lnx_g"]), row(lw["lnx_b"]),
              jnp.asarray(e, BF16)]
    sp = s0.reshape(B, RWKV_HEADS // 2, 2, HEAD_DIM, HEAD_DIM)
    z = jnp.zeros_like(sp[:, :, 0])
    s0 = jnp.concatenate([jnp.concatenate([sp[:, :, 0], z], axis=-1),
                          jnp.concatenate([z, sp[:, :, 1]], axis=-1)], axis=-2)
    st_shape = s0.shape[1:]
    const = lambda arr: pl.BlockSpec(arr.shape, lambda b, c: (0, 0))
    st_spec = pl.BlockSpec((None,) + st_shape, lambda b, c: (b, 0, 0, 0))
    sh_spec = pl.BlockSpec((None, 1, RWKV_IN), lambda b, c: (b, 0, 0))
    n_c = T // rows
    side_args, side_specs, side_out_spec, side_out_shape = _sample_attn_operands(
        *sample_attn_inputs, B * n_c, lambda b, c: b * n_c + c)
    rw, s_out, sh_out, att_side = pl.pallas_call(
        functools.partial(_rwkv_pair_kernel, n_double=n_double, n_side_in=len(side_args)),
        grid=(B, n_c),
        in_specs=[pl.BlockSpec((None, rows, RWKV_IN), lambda b, c: (b, c, 0)), sh_spec, st_spec]
                 + [const(t) for t in consts] + side_specs,
        out_specs=[pl.BlockSpec((None, rows, W), lambda b, c: (b, c, 0)), st_spec, sh_spec, side_out_spec],
        out_shape=[jax.ShapeDtypeStruct((B, T, W), BF16),
                   jax.ShapeDtypeStruct((B,) + st_shape, F32),
                   jax.ShapeDtypeStruct((B, 1, RWKV_IN), F32), side_out_shape],
        scratch_shapes=[pltpu.VMEM(st_shape, F32),
                        pltpu.VMEM((1, RWKV_IN), F32)],
        compiler_params=_params("arbitrary", "arbitrary"),
        name=f"rwkv_c{C}",
    )(P, shift0.reshape(B, 1, RWKV_IN), s0, *consts, *side_args)
    s_out = jnp.stack([s_out[:, :, :HEAD_DIM, :HEAD_DIM], s_out[:, :, HEAD_DIM:, HEAD_DIM:]], axis=2)
    s_out = s_out.reshape(B, RWKV_HEADS, HEAD_DIM, HEAD_DIM)
    return rw, s_out, sh_out.reshape(B, RWKV_IN), att_side


def _rwkv_scan_kernel(r_ref, k_ref, v_ref, lo_ref, sr_ref, sk_ref, sv_ref, slo_ref, s_ref,
                      mur_ref, muk_ref, muv_ref, mulo_ref, w0_ref, w2t_ref, a0_ref, a2t_ref, g2t_ref,
                      kk_ref, ka_ref, rk_ref, lg_ref, lb_ref,
                      rw_ref, sout_ref,
                      w_scr, nkk_scr, bb_scr, km_scr, rr_scr, vv_scr, bonus_scr, g_scr, o_scr):
    T = r_ref.shape[0]

    def shifted(ref, s0_ref, mu_ref, t):
        cur = ref[t]
        prev = s0_ref[...] if t == 0 else ref[t - 1]
        return cur + (prev - cur) * mu_ref[...]

    for t in range(T):
        r = shifted(r_ref, sr_ref, mur_ref, t)
        k = shifted(k_ref, sk_ref, muk_ref, t)
        v = shifted(v_ref, sv_ref, muv_ref, t)
        lo = shifted(lo_ref, slo_ref, mulo_ref, t)
        wl, al, gl = lo[0:DECAY_LORA], lo[DECAY_LORA:DECAY_LORA + AAA_LORA], lo[DECAY_LORA + AAA_LORA:]
        neg = -(w0_ref[...] + _dot(w2t_ref[...], jnp.tanh(wl).astype(BF16)))
        softplus = jnp.maximum(neg, 0.0) + jnp.log(1.0 + jnp.exp(-jnp.abs(neg)))
        w = jnp.exp(-jnp.exp(-softplus - 0.5))
        a = jax.nn.sigmoid(a0_ref[...] + _dot(a2t_ref[...], al.astype(BF16)))
        g_scr[t] = _dot(g2t_ref[...], jax.nn.sigmoid(gl).astype(BF16))
        kkr = k * kk_ref[...]
        kk = kkr / jnp.maximum(jnp.sqrt(jnp.sum(kkr * kkr, axis=0, keepdims=True)), 1e-12)
        kmod = k * (1.0 + (a - 1.0) * ka_ref[...])
        bonus_scr[t] = jnp.sum(r * kmod * rk_ref[...], axis=0, keepdims=True) * v
        w_scr[t] = w
        nkk_scr[t] = -kk
        bb_scr[t] = kk * a
        km_scr[t] = kmod
        rr_scr[t] = r
        vv_scr[t] = v

    def row(i, carry):
        s = s_ref[i]
        for t in range(T):
            sa = jnp.sum(s * nkk_scr[t], axis=0, keepdims=True)
            s = s * w_scr[t] + sa * bb_scr[t] + vv_scr[t, pl.ds(i, 1), :] * km_scr[t]
            o_scr[t, pl.ds(i, 1), :] = jnp.sum(s * rr_scr[t], axis=0, keepdims=True)
        sout_ref[i] = s
        return carry

    lax.fori_loop(0, HEAD_DIM, row, 0, unroll=4)
    o = o_scr[...]
    mean = jnp.mean(o, axis=1, keepdims=True)
    d = o - mean
    var = jnp.mean(d * d, axis=1, keepdims=True)
    on = d * lax.rsqrt(var + LNX_EPS) * lg_ref[...] + lb_ref[...]
    rw_ref[...] = ((on + bonus_scr[...]) * g_scr[...]).astype(rw_ref.dtype)


def _rwkv_scan(pt, shift0, s0t, lw):
    T, _, DB = pt.shape
    W, HD = RWKV_W, HEAD_DIM
    lora_w = DECAY_LORA + AAA_LORA + GATE_LORA
    assert (3 * W) % lora_w == 0
    col = lambda t: jnp.broadcast_to(t.reshape(-1, 1), (t.size, DB))
    shift_t = shift0.T
    mu_c = col(lw["mu"])
    head = lambda base: (lambda h: (base + h, 0))
    pt_blk = lambda base: pl.BlockSpec((T, HD, DB), lambda h: (0, base + h, 0))
    blk2 = lambda base: pl.BlockSpec((HD, DB), head(base))
    lo_pt = pl.BlockSpec((T, lora_w, DB), lambda h: (0, 3 * W // lora_w, 0))
    lo_2d = pl.BlockSpec((lora_w, DB), lambda h: (3 * W // lora_w, 0))
    nh = W // HD
    st_spec = pl.BlockSpec((None, HD, HD, DB), lambda h: (h, 0, 0, 0))
    wt_spec = lambda n: pl.BlockSpec((HD, n), lambda h: (h, 0))
    scr = lambda: pltpu.VMEM((T, HD, DB), F32)
    return pl.pallas_call(
        _rwkv_scan_kernel,
        grid=(nh,),
        in_specs=[pt_blk(0), pt_blk(nh), pt_blk(2 * nh), lo_pt,
                  blk2(0), blk2(nh), blk2(2 * nh), lo_2d, st_spec,
                  blk2(0), blk2(nh), blk2(2 * nh), lo_2d,
                  blk2(0), wt_spec(DECAY_LORA), blk2(0), wt_spec(AAA_LORA), wt_spec(GATE_LORA),
                  blk2(0), blk2(0), blk2(0), blk2(0), blk2(0)],
        out_specs=[pl.BlockSpec((T, HD, DB), lambda h: (0, h, 0)), st_spec],
        out_shape=[jax.ShapeDtypeStruct((T, W, DB), F32), jax.ShapeDtypeStruct(s0t.shape, F32)],
        scratch_shapes=[scr() for _ in range(9)],
        compiler_params=_params("arbitrary"),
        name="rwkv_scan",
    )(pt, pt, pt, pt, shift_t, shift_t, shift_t, shift_t, s0t,
      mu_c, mu_c, mu_c, mu_c, col(lw["w0"]), lw["w2"].T.astype(BF16), col(lw["a0"]), lw["a2"].T.astype(BF16),
      lw["g2"].T.astype(BF16), col(lw["k_k"]), col(lw["k_a"]), col(lw["r_k"]), col(lw["lnx_g"]), col(lw["lnx_b"]))


def _mlp_kernel(att_ref, rw_ref, x_ref, gt1_ref, sc2_ref, sh2_ref, gt2_ref, g2_ref, gf_ref,
                wo_ref, w1_ref, w2_ref, y_ref, *, final_norm, ff_chunk):
    half = wo_ref.shape[0] // 2
    mix = _dot(att_ref[...], wo_ref[0:half, :]) + _dot(rw_ref[...], wo_ref[half:, :])
    x1 = x_ref[...] + gt1_ref[...] * mix
    ms = jnp.mean(x1 * x1, axis=-1, keepdims=True)
    h2 = x1 * lax.rsqrt(ms + NORM_EPS) * g2_ref[...]
    h2 = (h2 * (1.0 + sc2_ref[...]) + sh2_ref[...]).astype(BF16)
    acc = jnp.zeros_like(x1)
    for c in range(w1_ref.shape[1] // ff_chunk):
        hid = _dot(h2, w1_ref[:, c * ff_chunk:(c + 1) * ff_chunk])
        hid = jnp.square(jnp.maximum(hid, 0.0)).astype(BF16)
        acc = acc + _dot(hid, w2_ref[c * ff_chunk:(c + 1) * ff_chunk, :])
    x2 = x1 + gt2_ref[...] * acc
    if final_norm:
        ms2 = jnp.mean(x2 * x2, axis=-1, keepdims=True)
        x2 = x2 * lax.rsqrt(ms2 + NORM_EPS) * gf_ref[...]
    y_ref[...] = x2


def _mlp(att, rw, x, gt1, sc2, sh2, gt2, g2, gf, wo_b, w1_b, w2_b, tm, per_row_mod, final_norm):
    G, R, D = x.shape
    dff = w1_b.shape[1]
    row_spec = lambda w: pl.BlockSpec((None, tm, w), lambda b, i: (b, i, 0))
    if per_row_mod:
        mod_spec = pl.BlockSpec((None, tm, D), lambda b, i: (b, i, 0))
    else:
        mod_spec = pl.BlockSpec((None, 1, D), lambda b, i: (b, 0, 0))
    const = lambda shape: pl.BlockSpec(shape, lambda b, i: (0, 0), pipeline_mode=pl.Buffered(1))
    return pl.pallas_call(
        functools.partial(_mlp_kernel, final_norm=final_norm, ff_chunk=1024),
        grid=(G, R // tm),
        in_specs=[row_spec(att.shape[-1]), row_spec(rw.shape[-1]), row_spec(D),
                  mod_spec, mod_spec, mod_spec, mod_spec, const((1, D)), const((1, D)),
                  const(wo_b.shape), const((D, dff)), const((dff, D))],
        out_specs=row_spec(D),
        out_shape=jax.ShapeDtypeStruct((G, R, D), F32),
        compiler_params=_params("arbitrary", "arbitrary"),
        name="mlp",
    )(att, rw, x, gt1, sc2, sh2, gt2, g2.reshape(1, D), gf.reshape(1, D), wo_b, w1_b, w2_b)


def kernel(x_prompt, x_sample, cache_k, cache_v, state_wkv, state_shift, c_prompt, c_sample, norm1_g, norm2_g, w_ada, b_ada, w_in, w_out, mu, w0, w2, a0, a2, g2, k_k, k_a, r_k, lnx_g, lnx_b, w_ff1, w_ff2, normf_g):
    B, S, D = x_prompt.shape
    DB, T, _ = x_sample.shape
    depth = w_in.shape[0]
    dt = x_prompt.dtype
    tabs_p = _rope_tables(jnp.arange(S))
    tabs_s = tuple(jnp.tile(t, (DB, 1)) for t in _rope_tables(PAST_LEN + jnp.arange(T)))
    tabs_t = tuple(jnp.repeat(t, DB, axis=0) for t in _rope_tables(PAST_LEN + jnp.arange(T)))
    c_all = jnp.concatenate([c_prompt, c_sample], axis=0)
    hp = x_prompt
    hs = x_sample.reshape(1, DB * T, D)
    outs = [[] for _ in range(8)]
    tm_s = min(ROW_TILE, DB * T)
    for l in range(depth):
        last = l == depth - 1
        mod = _ada(c_all, w_ada[l], b_ada[l])
        mod_p = [m.reshape(B, 1, D) for m in jnp.split(mod[:B], 6, axis=-1)]
        mod_s = [jnp.repeat(m, T, axis=0).reshape(1, DB * T, D) for m in jnp.split(mod[B:], 6, axis=-1)]
        mod_t = [m.reshape(1, DB, D) for m in jnp.split(mod[B:], 6, axis=-1)]
        w_in_b, wo_b = w_in[l].astype(BF16), w_out[l].astype(BF16)
        w1_b, w2_b = w_ff1[l].astype(BF16), w_ff2[l].astype(BF16)
        lw = dict(mu=mu[l], w0=w0[l], w2=w2[l], a0=a0[l], a2=a2[l], g2=g2[l], k_k=k_k[l], k_a=k_a[l],
                  r_k=r_k[l], lnx_g=lnx_g[l], lnx_b=lnx_b[l])

        sh1_p, sc1_p, gt1_p, sh2_p, sc2_p, gt2_p = mod_p
        sh1_s, sc1_s, gt1_s, sh2_s, sc2_s, gt2_s = mod_s
        q, k, v, P, kt, vt = _inproj(hp, sc1_p, sh1_p, norm1_g[l], w_in_b, tabs_p, ROW_TILE, "batch", True)
        q_s, k_s, v_s, P_s = _inproj(hs, sc1_s, sh1_s, norm1_g[l], w_in_b, tabs_s, tm_s, "row", False)
        hs_t = jnp.transpose(hs.reshape(DB, T, D), (1, 0, 2)).reshape(1, T * DB, D)
        *_, kt_s, vt_s, pt_s = _inproj(hs_t, mod_t[1], mod_t[0], norm1_g[l], w_in_b, tabs_t, DB, "tile", True,
                                       nq=T, p_t=True)

        att = _attn_fused(q, k, v)
        sample_attn_inputs = (q_s.reshape(DB, T, ATT_W), k_s.reshape(DB, T, ATT_W), v_s.reshape(DB, T, ATT_W),
                              jnp.transpose(cache_k[l], (0, 2, 3, 1)), jnp.transpose(cache_v[l], (0, 2, 3, 1)))
        rw, s_p, shift_p, att_s = _rwkv(P, jnp.zeros((B, RWKV_IN), F32),
                                        jnp.zeros((B, RWKV_HEADS, HEAD_DIM, HEAD_DIM), F32), lw, RWKV_CHUNK,
                                        sample_attn_inputs)
        hp = _mlp(att, rw, hp, gt1_p, sc2_p, sh2_p, gt2_p, norm2_g[l], normf_g, wo_b, w1_b, w2_b,
                  ROW_TILE, False, last)
        keep = min(max(w for w, _ in DIL_BRANCHES), S)
        win = lambda t: jnp.transpose(t.reshape(B, ATT_HEADS, HEAD_DIM, S)[..., S - keep:], (0, 3, 1, 2))
        outs[0].append(win(kt))
        outs[1].append(win(vt))
        outs[2].append(s_p)
        outs[3].append(shift_p)

        rw_t, s_t = _rwkv_scan(pt_s.reshape(T, RWKV_IN, DB), state_shift[l],
                               jnp.transpose(state_wkv[l], (1, 2, 3, 0)), lw)
        rw_s = jnp.transpose(rw_t, (2, 0, 1)).astype(BF16)
        hs = _mlp(att_s.reshape(1, DB * T, ATT_W), rw_s.reshape(1, DB * T, RWKV_W), hs, gt1_s, sc2_s, sh2_s, gt2_s,
                  norm2_g[l], normf_g, wo_b, w1_b, w2_b, tm_s, True, last)
        new = lambda t: jnp.transpose(t.reshape(T, ATT_HEADS, HEAD_DIM, DB), (3, 0, 1, 2))
        outs[4].append(new(kt_s))
        outs[5].append(new(vt_s))
        outs[6].append(jnp.transpose(s_t, (3, 0, 1, 2)))
        outs[7].append(P_s.reshape(DB, T, RWKV_IN)[:, T - 1])
    stack = lambda xs: jnp.stack(xs).astype(dt)
    return (hp.astype(dt), hs.reshape(DB, T, D).astype(x_sample.dtype), *[stack(o) for o in outs])
```

```python
import functools

import numpy as np
import jax
import jax.numpy as jnp
from jax import lax
from jax.experimental import pallas as pl
from jax.experimental.pallas import tpu as pltpu

F32 = jnp.float32
BF16 = jnp.bfloat16

HEAD_DIM = 64
ATT_HEADS = 8
ATT_W = ATT_HEADS * HEAD_DIM
RWKV_HEADS = 8
RWKV_W = RWKV_HEADS * HEAD_DIM
DIL_BRANCHES = ((128, 1), (512, 4), (2048, 16))
ROT_DIM = HEAD_DIM // 4
ROPE_THETA = 500000.0
DECAY_LORA = 64
AAA_LORA = 64
GATE_LORA = 128
RWKV_IN = 3 * RWKV_W + DECAY_LORA + AAA_LORA + GATE_LORA
NORM_EPS = 1e-6
LNX_EPS = 64e-5
PAST_LEN = 8192
ATT_BLOCK = 128
NEG_BIG = -1e30
LOG2_E = 1.4426950408889634
VMEM_LIMIT = 56 * 1024 * 1024
ROW_TILE = 512
RWKV_CHUNK = 64
RWKV_CHUNKS_PER_STEP = 4
ATT_GROUP = 8


def _dot(a, b):
    return jnp.dot(a, b, preferred_element_type=F32)


def _dot_nt(a, b):
    return lax.dot_general(a, b, (((1,), (1,)), ((), ())), preferred_element_type=F32)


def _dot_tn(a, b):
    return lax.dot_general(a, b, (((0,), (0,)), ((), ())), preferred_element_type=F32)


def _split2(x):
    hi = x.astype(BF16)
    lo = (x - hi.astype(F32)).astype(BF16)
    return hi, lo


def _head_sums(x, e):
    C, W = x.shape
    n = W // e.shape[0]
    xh, xl = _split2(x)
    tiles = [t[:, e.shape[0] * p:e.shape[0] * (p + 1)] for t in (xh, xl) for p in range(n)]
    o = _dot(jnp.concatenate(tiles, axis=0), e)
    return jnp.concatenate([o[C * p:C * (p + 1)] + o[C * (n + p):C * (n + p + 1)] for p in range(n)], axis=1)


def _params(*sem):
    return pltpu.CompilerParams(dimension_semantics=sem, vmem_limit_bytes=VMEM_LIMIT)


def _ada_kernel(c_ref, w_ref, b_ref, o_ref):
    c = c_ref[...]
    s = c * jax.nn.sigmoid(c)
    o_ref[...] = _dot(s.astype(BF16), w_ref[...].astype(BF16)) + b_ref[...]


def _ada(c, w_ada, b_ada):
    n, d = c.shape
    cols = w_ada.shape[1]
    tn = 512
    return pl.pallas_call(
        _ada_kernel,
        grid=(cols // tn,),
        in_specs=[pl.BlockSpec((n, d), lambda j: (0, 0)),
                  pl.BlockSpec((d, tn), lambda j: (0, j)),
                  pl.BlockSpec((1, tn), lambda j: (0, j))],
        out_specs=pl.BlockSpec((n, tn), lambda j: (0, j)),
        out_shape=jax.ShapeDtypeStruct((n, cols), F32),
        compiler_params=_params("arbitrary"),
        name="ada",
    )(c, w_ada, b_ada.reshape(1, cols))


def _rope(t, cos, s1, s2):
    parts = []
    for c in range(ATT_W // 128):
        tc = t[:, 128 * c:128 * (c + 1)]
        parts.append(tc * cos + pltpu.roll(tc, 128 - ROT_DIM // 2, 1) * s1 + pltpu.roll(tc, ROT_DIM // 2, 1) * s2)
    return jnp.concatenate(parts, axis=1)


def _inproj_kernel(x_ref, sc_ref, sh_ref, g_ref, w_ref, cos_ref, s1_ref, s2_ref, *out_refs,
                   transposed, p_t):
    q_ref, k_ref, v_ref, p_ref = out_refs[:4]
    rest = list(out_refs[4:])
    x = x_ref[...]
    ms = jnp.mean(x * x, axis=-1, keepdims=True)
    h = x * lax.rsqrt(ms + NORM_EPS) * g_ref[...]
    h = (h * (1.0 + sc_ref[...]) + sh_ref[...]).astype(BF16)
    cos, s1, s2 = cos_ref[...], s1_ref[...], s2_ref[...]
    q = _rope(_dot(h, w_ref[:, 0:ATT_W]), cos, s1, s2)
    q_ref[...] = (q * HEAD_DIM ** -0.5).astype(q_ref.dtype)
    k = _rope(_dot(h, w_ref[:, ATT_W:2 * ATT_W]), cos, s1, s2)
    v = _dot(h, w_ref[:, 2 * ATT_W:3 * ATT_W])
    k_ref[...] = k
    v_ref[...] = v
    p = _dot(h, w_ref[:, 3 * ATT_W:])
    p_ref[...] = p
    if transposed:
        rest.pop(0)[...] = k.T
        rest.pop(0)[...] = v.T
    if p_t:
        rest.pop(0)[...] = p.T


def _inproj(x, sc, sh, g, w_in_b, tabs, tm, mod_mode, transposed, nq=1, p_t=False):
    G, R, D = x.shape
    nin = w_in_b.shape[1]
    row_spec = lambda w: pl.BlockSpec((None, tm, w), lambda b, i: (b, i, 0))
    mod_spec = {"batch": pl.BlockSpec((None, 1, D), lambda b, i: (b, 0, 0)),
                "row": pl.BlockSpec((None, tm, D), lambda b, i: (b, i, 0)),
                "tile": pl.BlockSpec((None, tm, D), lambda b, i: (b, 0, 0))}[mod_mode]
    tab_spec = pl.BlockSpec((tm, 128), lambda b, i: (i, 0))
    const = lambda shape: pl.BlockSpec(shape, lambda b, i: (0, 0))
    out_specs = [row_spec(ATT_W)] * 3 + [row_spec(RWKV_IN)]
    out_shape = [jax.ShapeDtypeStruct((G, R, ATT_W), F32)] * 3 + [jax.ShapeDtypeStruct((G, R, RWKV_IN), F32)]
    if transposed:
        spq = (R // nq) // tm
        t_spec = pl.BlockSpec((None, None, ATT_W, tm), lambda b, i: (b, i // spq, 0, i % spq))
        out_specs += [t_spec] * 2
        out_shape += [jax.ShapeDtypeStruct((G, nq, ATT_W, R // nq), F32)] * 2
        if p_t:
            out_specs.append(pl.BlockSpec((None, None, RWKV_IN, tm), lambda b, i: (b, i // spq, 0, i % spq)))
            out_shape.append(jax.ShapeDtypeStruct((G, nq, RWKV_IN, R // nq), F32))
    return pl.pallas_call(
        functools.partial(_inproj_kernel, transposed=transposed, p_t=p_t),
        grid=(G, R // tm),
        in_specs=[row_spec(D), mod_spec, mod_spec, const((1, D)), const((D, nin)),
                  tab_spec, tab_spec, tab_spec],
        out_specs=out_specs,
        out_shape=out_shape,
        compiler_params=_params("arbitrary", "arbitrary"),
        name="inproj",
    )(x, sc, sh, g.reshape(1, D), w_in_b, *tabs)


def _rope_tables(pos):
    half = ROT_DIM // 2
    inv = ROPE_THETA ** (-jnp.arange(half, dtype=F32) * (2.0 / ROT_DIM))
    ang = pos.astype(F32)[:, None] * inv[None, :]
    cos, sin = jnp.cos(ang), jnp.sin(ang)
    n = pos.shape[0]
    ones = jnp.ones((n, HEAD_DIM - ROT_DIM), F32)
    zeros = jnp.zeros((n, HEAD_DIM - ROT_DIM), F32)
    zh = jnp.zeros((n, half), F32)
    ctab = jnp.concatenate([cos, cos, ones], axis=1)
    s1 = jnp.concatenate([-sin, zh, zeros], axis=1)
    s2 = jnp.concatenate([zh, sin, zeros], axis=1)
    return tuple(jnp.tile(t, (1, 128 // HEAD_DIM)) for t in (ctab, s1, s2))


def _attn_fused_kernel(q_ref, k_ref, v_ref, o_ref, acc_scr, m_scr, l_scr, *, dilations, group):
    nw, HD = ATT_BLOCK, HEAD_DIM
    S = q_ref.shape[0]
    n_tiles = S // nw
    lane = lax.broadcasted_iota(jnp.int32, (nw, 2 * HD), 1)
    first_head = lane < HD
    q_idx = lax.broadcasted_iota(jnp.int32, (nw, 2 * HD), 0)
    n_col = nw // HD
    key_idx = [HD * jc + (lane & (HD - 1)) for jc in range(n_col)]
    cur_ok = [q_idx >= key_idx[jc] for jc in range(n_col)]

    def bd(x):
        zero = jnp.zeros_like(x)
        xa, xb = jnp.where(first_head, x, zero), jnp.where(first_head, zero, x)
        parts = []
        for jc in range(n_col):
            parts += [xa[HD * jc:HD * (jc + 1)], xb[HD * jc:HD * (jc + 1)]]
        return jnp.concatenate(parts, axis=0)

    bd_row = lax.broadcasted_iota(jnp.int32, (2 * nw, 2 * HD), 0)
    bd_lane = lax.broadcasted_iota(jnp.int32, (2 * nw, 2 * HD), 1)
    ones_bd = jnp.where(((bd_row // HD) % 2 == 0) == (bd_lane < HD), 1.0, 0.0).astype(BF16)

    def with_ones(v):
        return jnp.concatenate([bd(v.astype(BF16)), ones_bd], axis=1)

    def half_reduce(x, op, fill):
        ra = op(jnp.where(first_head, x, fill), axis=-1, keepdims=True)
        rb = op(jnp.where(first_head, fill, x), axis=-1, keepdims=True)
        return jnp.where(first_head, ra, rb)

    def steps(tiles, use_prev, init, final):
        n = range(len(tiles))
        qs = [(q_ref[tiles[t][0], :] * LOG2_E).astype(BF16) for t in n]
        cols, vbd = [], []
        for t in n:
            rows, prev_rows, has_prev = tiles[t]
            s = _dot_nt(qs[t], bd(k_ref[rows, :].astype(BF16)))
            c = [jnp.where(cur_ok[jc], s[:, 2 * HD * jc:2 * HD * (jc + 1)], NEG_BIG) for jc in range(n_col)]
            vb = [with_ones(v_ref[rows, :])]
            if use_prev:
                sp = _dot_nt(qs[t], bd(k_ref[prev_rows, :].astype(BF16)))
                thresh = jnp.where(has_prev, 0, nw)
                c += [jnp.where((key_idx[jc] - q_idx) >= thresh, sp[:, 2 * HD * jc:2 * HD * (jc + 1)], NEG_BIG)
                      for jc in range(n_col)]
                vb.append(with_ones(v_ref[prev_rows, :]))
            cols.append(c)
            vbd.append(vb)
        m_new, m_old, p = [], [], []
        for t in n:
            m = half_reduce(functools.reduce(jnp.maximum, cols[t]), jnp.max, NEG_BIG)
            if not init:
                m_old.append(m_scr[tiles[t][0], :])
                m = jnp.maximum(m, m_old[t])
            m_new.append(m)
            p.append([jnp.exp2(c - m) for c in cols[t]])
        acc = []
        for t in n:
            a = _dot(jnp.concatenate(p[t][0:n_col], axis=1).astype(BF16), vbd[t][0])
            if use_prev:
                a = a + _dot(jnp.concatenate(p[t][n_col:], axis=1).astype(BF16), vbd[t][1])
            acc.append(a)
        for t in n:
            rows = tiles[t][0]
            a, l = acc[t][:, 0:2 * HD], acc[t][:, 2 * HD:]
            if not init:
                alpha = jnp.exp2(m_old[t] - m_new[t])
                a = a + acc_scr[rows, :] * alpha
                l = l + l_scr[rows, :] * alpha
            if final:
                o_ref[rows, :] = (a / l).astype(o_ref.dtype)
            else:
                acc_scr[rows, :] = a
                m_scr[rows, :] = m_new[t]
                l_scr[rows, :] = l

    n_br = len(dilations)
    for bi, dil in enumerate(dilations):
        init, final = bi == 0, bi == n_br - 1
        use_prev = S // (dil * nw) > 1

        def group_body(g, carry, dil=dil, use_prev=use_prev, init=init, final=final):
            tiles = []
            for i in range(group):
                t = g * group + i
                r, j = lax.rem(t, dil), lax.div(t, dil)
                start = r + j * (dil * nw)
                prev = r + jnp.maximum(j - 1, 0) * (dil * nw)
                if dil > 1:
                    tiles.append((pl.ds(start, nw, stride=dil), pl.ds(prev, nw, stride=dil), j > 0))
                else:
                    tiles.append((pl.ds(pl.multiple_of(start, nw), nw), pl.ds(pl.multiple_of(prev, nw), nw), j > 0))
            steps(tiles, use_prev, init, final)
            return carry

        lax.fori_loop(0, n_tiles // group, group_body, 0, unroll=True)


def _attn_fused(q, k, v):
    B, S, W = q.shape
    nw = ATT_BLOCK
    dilations = tuple(sorted((d for _, d in DIL_BRANCHES), reverse=True))
    for win, d in DIL_BRANCHES:
        assert win == d * nw and S % (d * nw) == 0
    assert (S // nw) % ATT_GROUP == 0
    PW = 2 * HEAD_DIM
    spec = pl.BlockSpec((None, S, PW), lambda b, p: (b, 0, p))
    return pl.pallas_call(
        functools.partial(_attn_fused_kernel, dilations=dilations, group=ATT_GROUP),
        grid=(B, W // PW),
        in_specs=[spec, spec, spec],
        out_specs=spec,
        out_shape=jax.ShapeDtypeStruct((B, S, W), BF16),
        scratch_shapes=[pltpu.VMEM((S, PW), F32) for _ in range(3)],
        compiler_params=_params("arbitrary", "arbitrary"),
        name="attn_fused",
    )(q, k, v)


def _sample_counts(T, WB):
    i = np.arange(T)
    c = np.arange(WB)
    delta = WB + i[None, :] - c[:, None]
    cnt_c = np.zeros((WB, T), np.float32)
    diff = i[None, :] - i[:, None]
    cnt_n = np.zeros((T, T), np.float32)
    for win, dil in DIL_BRANCHES:
        cnt_c += ((delta > 0) & (delta % dil == 0) & (delta <= win)).astype(np.float32)
        cnt_n += ((diff >= 0) & (diff % dil == 0) & (diff <= win)).astype(np.float32)
    return cnt_c, cnt_n


def _sample_attn_kernel(q_ref, kn_ref, vn_ref, kt_ref, vt_ref, cc_ref, cn_ref, o_ref):
    cc, cn = cc_ref[...], cn_ref[...]
    nb = q_ref.shape[0]
    items = [(b, h) for b in range(nb) for h in range(ATT_HEADS)]
    sl = lambda h: slice(HEAD_DIM * h, HEAD_DIM * (h + 1))
    qh = {(b, h): q_ref[b, :, sl(h)] for b, h in items}
    s_c = {i: jnp.where(cc > 0, _dot(qh[i].astype(BF16), kt_ref[i[0], i[1]].astype(BF16)), NEG_BIG)
           for i in items}
    s_n = {i: jnp.where(cn > 0, _dot_nt(qh[i], kn_ref[i[0], :, sl(i[1])]), NEG_BIG) for i in items}
    p_c, p_n, l = {}, {}, {}
    for i in items:
        m = jnp.maximum(jnp.max(s_c[i], axis=-1, keepdims=True), jnp.max(s_n[i], axis=-1, keepdims=True))
        p_c[i] = cc * jnp.exp(s_c[i] - m)
        p_n[i] = cn * jnp.exp(s_n[i] - m)
        l[i] = jnp.sum(p_c[i], axis=-1, keepdims=True) + jnp.sum(p_n[i], axis=-1, keepdims=True)
    acc = {i: _dot_nt(p_c[i].astype(BF16), vt_ref[i[0], i[1]].astype(BF16)) + _dot(p_n[i], vn_ref[i[0], :, sl(i[1])])
           for i in items}
    for b in range(nb):
        o_ref[b] = jnp.concatenate([acc[(b, h)] / l[(b, h)] for h in range(ATT_HEADS)], axis=1).astype(o_ref.dtype)


def _sample_attn_operands(q, k_new, v_new, cache_kt, cache_vt, n_steps, step_index):
    DB, T, W = q.shape
    WB = cache_kt.shape[-1]
    assert DB % n_steps == 0, "the requests must split evenly over the host call's grid steps"
    nb = DB // n_steps
    cnt_c, cnt_n = _sample_counts(T, WB)
    cc = jnp.asarray(cnt_c.T)
    cn = jnp.asarray(cnt_n.T)
    new_spec = pl.BlockSpec((nb, T, W), lambda *g: (step_index(*g), 0, 0))
    cache_spec = pl.BlockSpec((nb, ATT_HEADS, HEAD_DIM, WB), lambda *g: (step_index(*g), 0, 0, 0))
    const = lambda arr: pl.BlockSpec(arr.shape, lambda *g: (0, 0))
    args = (q, k_new, v_new, cache_kt, cache_vt, cc, cn)
    in_specs = [new_spec, new_spec, new_spec, cache_spec, cache_spec, const(cc), const(cn)]
    return args, in_specs, new_spec, jax.ShapeDtypeStruct((DB, T, W), BF16)


def _rwkv_prologue(p_ref, shift_ref, prev_scr, shout_ref, mu_ref, w0_ref, w2_ref, a0_ref, a2_ref, g2_ref,
                   kk_ref, ka_ref, rk_ref, e, chunk):
    C = p_ref.shape[0]
    P = p_ref[...]
    row = lax.broadcasted_iota(jnp.int32, (C, 1), 0)
    p_prev = jnp.where(row == 0, prev_scr[...], pltpu.roll(P, 1, 0))
    pm = P + (p_prev - P) * mu_ref[...]
    prev_scr[...] = P[C - 1:C, :]
    shout_ref[...] = P[C - 1:C, :]

    W = RWKV_W
    r, k, v = pm[:, 0:W], pm[:, W:2 * W], pm[:, 2 * W:3 * W]
    lora_wa = pm[:, 3 * W:3 * W + DECAY_LORA + AAA_LORA]
    gl = pm[:, 3 * W + DECAY_LORA + AAA_LORA:]
    wpre = w0_ref[...] + _dot(jnp.tanh(lora_wa).astype(BF16), w2_ref[...])
    neg = -wpre
    softplus = jnp.maximum(neg, 0.0) + jnp.log(1.0 + jnp.exp(-jnp.abs(neg)))
    logw = -jnp.exp(-softplus - 0.5)
    a = jax.nn.sigmoid(a0_ref[...] + _dot(lora_wa.astype(BF16), a2_ref[...]))
    g = _dot(jax.nn.sigmoid(gl).astype(BF16), g2_ref[...])
    kkr = k * kk_ref[...]
    nrm = jnp.sqrt(_head_sums(kkr * kkr, e))
    kk = kkr / jnp.maximum(nrm, 1e-12)
    kmod = k * (1.0 + (a - 1.0) * ka_ref[...])
    bonus = _head_sums(r * kmod * rk_ref[...], e) * v

    ti = lax.broadcasted_iota(jnp.int32, (C, C), 0)
    si = lax.broadcasted_iota(jnp.int32, (C, C), 1)
    tri_b = jnp.where((ti >= si) & (lax.div(ti, chunk) == lax.div(si, chunk)), 1.0, 0.0).astype(BF16)
    lw_h = logw.astype(BF16)
    lw_r = logw - lw_h.astype(F32)
    lw_m = lw_r.astype(BF16)
    lw_l = (lw_r - lw_m.astype(F32)).astype(BF16)
    L = _dot(tri_b, lw_h) + (_dot(tri_b, lw_m) + _dot(tri_b, lw_l))
    e_l = jnp.exp(L)
    e_nl = jnp.exp(-L)
    a_t = -kk * jnp.exp(L - logw)
    r_t = r * e_l
    b_t = kk * a * e_nl
    k_t = kmod * e_nl
    return v, a_t, r_t, b_t, k_t, e_l, bonus, g


def _rwkv_epilogue(o, bonus, g, e, lg_ref, lb_ref, rw_ref):
    mean = _head_sums(o, e) * (1.0 / HEAD_DIM)
    d = o - mean
    var = _head_sums(d * d, e) * (1.0 / HEAD_DIM)
    on = d * lax.rsqrt(var + LNX_EPS) * lg_ref[...] + lb_ref[...]
    rw_ref[...] = ((on + bonus) * g).astype(rw_ref.dtype)


def _rwkv_pair_kernel(*refs, n_double, n_side_in):
    (p_ref, shift_ref, s0_ref, mu_ref, w0_ref, w2_ref, a0_ref, a2_ref, g2_ref,
     kk_ref, ka_ref, rk_ref, lg_ref, lb_ref, e_ref) = refs[:15]
    side_in = refs[15:15 + n_side_in]
    rw_ref, sout_ref, shout_ref, side_out, s_scr, prev_scr = refs[15 + n_side_in:]

    C = HEAD_DIM
    n_sub = p_ref.shape[0] // C
    n_pairs = RWKV_HEADS // 2
    PW = 2 * HEAD_DIM

    @pl.when(pl.program_id(1) == 0)
    def _():
        s_scr[...] = s0_ref[...]
        prev_scr[...] = shift_ref[...]

    e = e_ref[...]
    v, a_t, r_t, b_t, k_t, e_l, bonus, g = _rwkv_prologue(
        p_ref, shift_ref, prev_scr, shout_ref, mu_ref, w0_ref, w2_ref, a0_ref, a2_ref, g2_ref,
        kk_ref, ka_ref, rk_ref, e, C)
    _sample_attn_kernel(*side_in, side_out)

    lane = lax.broadcasted_iota(jnp.int32, (C, PW), 1)
    ti = lax.broadcasted_iota(jnp.int32, (C, PW), 0)
    si = lane & (HEAD_DIM - 1)
    first_head = lane < HEAD_DIM
    tri_incl = ti >= si
    tri_strict = ti > si
    eye = jnp.where(ti == si, 1.0, 0.0)
    rows2 = lax.broadcasted_iota(jnp.int32, (2 * C, PW), 0)
    lanes2 = lax.broadcasted_iota(jnp.int32, (2 * C, PW), 1)
    same_head = (rows2 < C) == (lanes2 < HEAD_DIM)

    def bd(x):
        zero = jnp.zeros_like(x)
        return jnp.concatenate([jnp.where(first_head, x, zero), jnp.where(first_head, zero, x)], axis=0)

    def dot_bd(x, y):
        return _dot(x.astype(BF16), bd(y.astype(BF16)))

    items = [(c, p) for c in range(n_sub) for p in range(n_pairs)]
    tile = lambda t, c, p: t[C * c:C * (c + 1), PW * p:PW * (p + 1)]
    ar = {i: jnp.concatenate([tile(a_t, *i), tile(r_t, *i)], axis=0).astype(BF16) for i in items}
    bk = {i: jnp.concatenate([tile(b_t, *i), tile(k_t, *i)], axis=0).astype(BF16) for i in items}
    gm = {i: _dot_nt(ar[i], jnp.concatenate([bd(bk[i][0:C]), bd(bk[i][C:2 * C])], axis=0)) for i in items}
    m_ab = {i: jnp.where(tri_strict, gm[i][0:C, 0:PW], 0.0) for i in items}
    m_ak = {i: jnp.where(tri_strict, gm[i][0:C, PW:2 * PW], 0.0).astype(BF16) for i in items}
    n_r = {i: jnp.concatenate([jnp.where(tri_incl, gm[i][C:2 * C, 0:PW], 0.0),
                               jnp.where(tri_incl, gm[i][C:2 * C, PW:2 * PW], 0.0)], axis=1).astype(BF16)
           for i in items}
    t_inv = {i: eye + m_ab[i] for i in items}
    mp = m_ab
    for _ in range(n_double):
        mp = {i: dot_bd(mp[i], mp[i]) for i in items}
        t_inv = {i: t_inv[i] + dot_bd(t_inv[i], mp[i]) for i in items}
    v_b = {i: tile(v, *i).astype(BF16) for i in items}
    v_bd = {i: bd(v_b[i]) for i in items}
    mv = {i: _dot(m_ak[i], v_bd[i]) for i in items}
    state = [s_scr[p] for p in range(n_pairs)]
    o_rows = []
    for c in range(n_sub):
        its = [(c, p) for p in range(n_pairs)]
        ars = {i: _dot_nt(ar[i], state[i[1]].astype(BF16)) for i in its}
        u_b = {i: dot_bd(t_inv[i], ars[i][0:C] + mv[i]).astype(BF16) for i in its}
        o = [ars[i][C:2 * C] + _dot(n_r[i], jnp.concatenate([bd(u_b[i]), v_bd[i]], axis=0)) for i in its]
        o_rows.append(jnp.concatenate(o, axis=1))
        for i in its:
            upd = jnp.where(same_head, _dot_tn(jnp.concatenate([u_b[i], v_b[i]], axis=0), bk[i]), 0.0)
            state[i[1]] = (state[i[1]] + upd) * e_l[C * (c + 1) - 1:C * (c + 1), PW * i[1]:PW * (i[1] + 1)]
    for p in range(n_pairs):
        s_scr[p] = state[p]
        sout_ref[p] = state[p]
    _rwkv_epilogue(jnp.concatenate(o_rows, axis=0), bonus, g, e, lg_ref, lb_ref, rw_ref)


def _rwkv(P, shift0, s0, lw, C, sample_attn_inputs):
    B, T, _ = P.shape
    assert C == HEAD_DIM, "the pair-packed chunk kernel needs chunk length == head dim"
    rows = C * RWKV_CHUNKS_PER_STEP
    assert T % rows == 0
    n_double = max(int(np.log2(C)) - 1, 0)
    W = RWKV_W
    e = np.kron(np.eye(2, dtype=np.float32), np.ones((HEAD_DIM, HEAD_DIM), np.float32))
    zpad = jnp.zeros((DECAY_LORA, W), F32)
    w2p = jnp.concatenate([lw["w2"], jnp.zeros((AAA_LORA, W), F32)], axis=0).astype(BF16)
    a2p = jnp.concatenate([zpad, lw["a2"]], axis=0).astype(BF16)
    row = lambda t: t.reshape(1, -1)
    consts = [row(lw["mu"]), row(lw["w0"]), w2p, row(lw["a0"]), a2p, lw["g2"].astype(BF16),
              row(lw["k_k"]), row(lw["k_a"]), row(lw["r_k"]), row(lw["lnx_g"]), row(lw["lnx_b"]),
              jnp.asarray(e, BF16)]
    sp = s0.reshape(B, RWKV_HEADS // 2, 2, HEAD_DIM, HEAD_DIM)
    z = jnp.zeros_like(sp[:, :, 0])
    s0 = jnp.concatenate([jnp.concatenate([sp[:, :, 0], z], axis=-1),
                          jnp.concatenate([z, sp[:, :, 1]], axis=-1)], axis=-2)
    st_shape = s0.shape[1:]
    const = lambda arr: pl.BlockSpec(arr.shape, lambda b, c: (0, 0))
    st_spec = pl.BlockSpec((None,) + st_shape, lambda b, c: (b, 0, 0, 0))
    sh_spec = pl.BlockSpec((None, 1, RWKV_IN), lambda b, c: (b, 0, 0))
    n_c = T // rows
    side_args, side_specs, side_out_spec, side_out_shape = _sample_attn_operands(
        *sample_attn_inputs, B * n_c, lambda b, c: b * n_c + c)
    rw, s_out, sh_out, att_side = pl.pallas_call(
        functools.partial(_rwkv_pair_kernel, n_double=n_double, n_side_in=len(side_args)),
        grid=(B, n_c),
        in_specs=[pl.BlockSpec((None, rows, RWKV_IN), lambda b, c: (b, c, 0)), sh_spec, st_spec]
                 + [const(t) for t in consts] + side_specs,
        out_specs=[pl.BlockSpec((None, rows, W), lambda b, c: (b, c, 0)), st_spec, sh_spec, side_out_spec],
        out_shape=[jax.ShapeDtypeStruct((B, T, W), BF16),
                   jax.ShapeDtypeStruct((B,) + st_shape, F32),
                   jax.ShapeDtypeStruct((B, 1, RWKV_IN), F32), side_out_shape],
        scratch_shapes=[pltpu.VMEM(st_shape, F32),
                        pltpu.VMEM((1, RWKV_IN), F32)],
        compiler_params=_params("arbitrary", "arbitrary"),
        name=f"rwkv_c{C}",
    )(P, shift0.reshape(B, 1, RWKV_IN), s0, *consts, *side_args)
    s_out = jnp.stack([s_out[:, :, :HEAD_DIM, :HEAD_DIM], s_out[:, :, HEAD_DIM:, HEAD_DIM:]], axis=2)
    s_out = s_out.reshape(B, RWKV_HEADS, HEAD_DIM, HEAD_DIM)
    return rw, s_out, sh_out.reshape(B, RWKV_IN), att_side


def _rwkv_scan_kernel(r_ref, k_ref, v_ref, lo_ref, sr_ref, sk_ref, sv_ref, slo_ref, s_ref,
                      mur_ref, muk_ref, muv_ref, mulo_ref, w0_ref, w2t_ref, a0_ref, a2t_ref, g2t_ref,
                      kk_ref, ka_ref, rk_ref, lg_ref, lb_ref,
                      rw_ref, sout_ref,
                      w_scr, nkk_scr, bb_scr, km_scr, rr_scr, vv_scr, bonus_scr, g_scr, o_scr):
    T = r_ref.shape[0]

    def shifted(ref, s0_ref, mu_ref, t):
        cur = ref[t]
        prev = s0_ref[...] if t == 0 else ref[t - 1]
        return cur + (prev - cur) * mu_ref[...]

    for t in range(T):
        r = shifted(r_ref, sr_ref, mur_ref, t)
        k = shifted(k_ref, sk_ref, muk_ref, t)
        v = shifted(v_ref, sv_ref, muv_ref, t)
        lo = shifted(lo_ref, slo_ref, mulo_ref, t)
        wl, al, gl = lo[0:DECAY_LORA], lo[DECAY_LORA:DECAY_LORA + AAA_LORA], lo[DECAY_LORA + AAA_LORA:]
        neg = -(w0_ref[...] + _dot(w2t_ref[...], jnp.tanh(wl).astype(BF16)))
        softplus = jnp.maximum(neg, 0.0) + jnp.log(1.0 + jnp.exp(-jnp.abs(neg)))
        w = jnp.exp(-jnp.exp(-softplus - 0.5))
        a = jax.nn.sigmoid(a0_ref[...] + _dot(a2t_ref[...], al.astype(BF16)))
        g_scr[t] = _dot(g2t_ref[...], jax.nn.sigmoid(gl).astype(BF16))
        kkr = k * kk_ref[...]
        kk = kkr / jnp.maximum(jnp.sqrt(jnp.sum(kkr * kkr, axis=0, keepdims=True)), 1e-12)
        kmod = k * (1.0 + (a - 1.0) * ka_ref[...])
        bonus_scr[t] = jnp.sum(r * kmod * rk_ref[...], axis=0, keepdims=True) * v
        w_scr[t] = w
        nkk_scr[t] = -kk
        bb_scr[t] = kk * a
        km_scr[t] = kmod
        rr_scr[t] = r
        vv_scr[t] = v

    def row(i, carry):
        s = s_ref[i]
        for t in range(T):
            sa = jnp.sum(s * nkk_scr[t], axis=0, keepdims=True)
            s = s * w_scr[t] + sa * bb_scr[t] + vv_scr[t, pl.ds(i, 1), :] * km_scr[t]
            o_scr[t, pl.ds(i, 1), :] = jnp.sum(s * rr_scr[t], axis=0, keepdims=True)
        sout_ref[i] = s
        return carry

    lax.fori_loop(0, HEAD_DIM, row, 0, unroll=4)
    o = o_scr[...]
    mean = jnp.mean(o, axis=1, keepdims=True)
    d = o - mean
    var = jnp.mean(d * d, axis=1, keepdims=True)
    on = d * lax.rsqrt(var + LNX_EPS) * lg_ref[...] + lb_ref[...]
    rw_ref[...] = ((on + bonus_scr[...]) * g_scr[...]).astype(rw_ref.dtype)


def _rwkv_scan(pt, shift0, s0t, lw):
    T, _, DB = pt.shape
    W, HD = RWKV_W, HEAD_DIM
    lora_w = DECAY_LORA + AAA_LORA + GATE_LORA
    assert (3 * W) % lora_w == 0
    col = lambda t: jnp.broadcast_to(t.reshape(-1, 1), (t.size, DB))
    shift_t = shift0.T
    mu_c = col(lw["mu"])
    head = lambda base: (lambda h: (base + h, 0))
    pt_blk = lambda base: pl.BlockSpec((T, HD, DB), lambda h: (0, base + h, 0))
    blk2 = lambda base: pl.BlockSpec((HD, DB), head(base))
    lo_pt = pl.BlockSpec((T, lora_w, DB), lambda h: (0, 3 * W // lora_w, 0))
    lo_2d = pl.BlockSpec((lora_w, DB), lambda h: (3 * W // lora_w, 0))
    nh = W // HD
    st_spec = pl.BlockSpec((None, HD, HD, DB), lambda h: (h, 0, 0, 0))
    wt_spec = lambda n: pl.BlockSpec((HD, n), lambda h: (h, 0))
    scr = lambda: pltpu.VMEM((T, HD, DB), F32)
    return pl.pallas_call(
        _rwkv_scan_kernel,
        grid=(nh,),
        in_specs=[pt_blk(0), pt_blk(nh), pt_blk(2 * nh), lo_pt,
                  blk2(0), blk2(nh), blk2(2 * nh), lo_2d, st_spec,
                  blk2(0), blk2(nh), blk2(2 * nh), lo_2d,
                  blk2(0), wt_spec(DECAY_LORA), blk2(0), wt_spec(AAA_LORA), wt_spec(GATE_LORA),
                  blk2(0), blk2(0), blk2(0), blk2(0), blk2(0)],
        out_specs=[pl.BlockSpec((T, HD, DB), lambda h: (0, h, 0)), st_spec],
        out_shape=[jax.ShapeDtypeStruct((T, W, DB), F32), jax.ShapeDtypeStruct(s0t.shape, F32)],
        scratch_shapes=[scr() for _ in range(9)],
        compiler_params=_params("arbitrary"),
        name="rwkv_scan",
    )(pt, pt, pt, pt, shift_t, shift_t, shift_t, shift_t, s0t,
      mu_c, mu_c, mu_c, mu_c, col(lw["w0"]), lw["w2"].T.astype(BF16), col(lw["a0"]), lw["a2"].T.astype(BF16),
      lw["g2"].T.astype(BF16), col(lw["k_k"]), col(lw["k_a"]), col(lw["r_k"]), col(lw["lnx_g"]), col(lw["lnx_b"]))


def _mlp_kernel(att_ref, rw_ref, x_ref, gt1_ref, sc2_ref, sh2_ref, gt2_ref, g2_ref, gf_ref,
                wo_ref, w1_ref, w2_ref, y_ref, *, final_norm, ff_chunk):
    half = wo_ref.shape[0] // 2
    mix = _dot(att_ref[...], wo_ref[0:half, :]) + _dot(rw_ref[...], wo_ref[half:, :])
    x1 = x_ref[...] + gt1_ref[...] * mix
    ms = jnp.mean(x1 * x1, axis=-1, keepdims=True)
    h2 = x1 * lax.rsqrt(ms + NORM_EPS) * g2_ref[...]
    h2 = (h2 * (1.0 + sc2_ref[...]) + sh2_ref[...]).astype(BF16)
    acc = jnp.zeros_like(x1)
    for c in range(w1_ref.shape[1] // ff_chunk):
        hid = _dot(h2, w1_ref[:, c * ff_chunk:(c + 1) * ff_chunk])
        hid = jnp.square(jnp.maximum(hid, 0.0)).astype(BF16)
        acc = acc + _dot(hid, w2_ref[c * ff_chunk:(c + 1) * ff_chunk, :])
    x2 = x1 + gt2_ref[...] * acc
    if final_norm:
        ms2 = jnp.mean(x2 * x2, axis=-1, keepdims=True)
        x2 = x2 * lax.rsqrt(ms2 + NORM_EPS) * gf_ref[...]
    y_ref[...] = x2


def _mlp(att, rw, x, gt1, sc2, sh2, gt2, g2, gf, wo_b, w1_b, w2_b, tm, per_row_mod, final_norm):
    G, R, D = x.shape
    dff = w1_b.shape[1]
    row_spec = lambda w: pl.BlockSpec((None, tm, w), lambda b, i: (b, i, 0))
    if per_row_mod:
        mod_spec = pl.BlockSpec((None, tm, D), lambda b, i: (b, i, 0))
    else:
        mod_spec = pl.BlockSpec((None, 1, D), lambda b, i: (b, 0, 0))
    const = lambda shape: pl.BlockSpec(shape, lambda b, i: (0, 0), pipeline_mode=pl.Buffered(1))
    return pl.pallas_call(
        functools.partial(_mlp_kernel, final_norm=final_norm, ff_chunk=1024),
        grid=(G, R // tm),
        in_specs=[row_spec(att.shape[-1]), row_spec(rw.shape[-1]), row_spec(D),
                  mod_spec, mod_spec, mod_spec, mod_spec, const((1, D)), const((1, D)),
                  const(wo_b.shape), const((D, dff)), const((dff, D))],
        out_specs=row_spec(D),
        out_shape=jax.ShapeDtypeStruct((G, R, D), F32),
        compiler_params=_params("arbitrary", "arbitrary"),
        name="mlp",
    )(att, rw, x, gt1, sc2, sh2, gt2, g2.reshape(1, D), gf.reshape(1, D), wo_b, w1_b, w2_b)


def kernel(x_prompt, x_sample, cache_k, cache_v, state_wkv, state_shift, c_prompt, c_sample, norm1_g, norm2_g, w_ada, b_ada, w_in, w_out, mu, w0, w2, a0, a2, g2, k_k, k_a, r_k, lnx_g, lnx_b, w_ff1, w_ff2, normf_g):
    B, S, D = x_prompt.shape
    DB, T, _ = x_sample.shape
    depth = w_in.shape[0]
    dt = x_prompt.dtype
    tabs_p = _rope_tables(jnp.arange(S))
    tabs_s = tuple(jnp.tile(t, (DB, 1)) for t in _rope_tables(PAST_LEN + jnp.arange(T)))
    tabs_t = tuple(jnp.repeat(t, DB, axis=0) for t in _rope_tables(PAST_LEN + jnp.arange(T)))
    c_all = jnp.concatenate([c_prompt, c_sample], axis=0)
    hp = x_prompt
    hs = x_sample.reshape(1, DB * T, D)
    outs = [[] for _ in range(8)]
    tm_s = min(ROW_TILE, DB * T)
    for l in range(depth):
        last = l == depth - 1
        mod = _ada(c_all, w_ada[l], b_ada[l])
        mod_p = [m.reshape(B, 1, D) for m in jnp.split(mod[:B], 6, axis=-1)]
        mod_s = [jnp.repeat(m, T, axis=0).reshape(1, DB * T, D) for m in jnp.split(mod[B:], 6, axis=-1)]
        mod_t = [m.reshape(1, DB, D) for m in jnp.split(mod[B:], 6, axis=-1)]
        w_in_b, wo_b = w_in[l].astype(BF16), w_out[l].astype(BF16)
        w1_b, w2_b = w_ff1[l].astype(BF16), w_ff2[l].astype(BF16)
        lw = dict(mu=mu[l], w0=w0[l], w2=w2[l], a0=a0[l], a2=a2[l], g2=g2[l], k_k=k_k[l], k_a=k_a[l],
                  r_k=r_k[l], lnx_g=lnx_g[l], lnx_b=lnx_b[l])

        sh1_p, sc1_p, gt1_p, sh2_p, sc2_p, gt2_p = mod_p
        sh1_s, sc1_s, gt1_s, sh2_s, sc2_s, gt2_s = mod_s
        q, k, v, P, kt, vt = _inproj(hp, sc1_p, sh1_p, norm1_g[l], w_in_b, tabs_p, ROW_TILE, "batch", True)
        q_s, k_s, v_s, P_s = _inproj(hs, sc1_s, sh1_s, norm1_g[l], w_in_b, tabs_s, tm_s, "row", False)
        hs_t = jnp.transpose(hs.reshape(DB, T, D), (1, 0, 2)).reshape(1, T * DB, D)
        *_, kt_s, vt_s, pt_s = _inproj(hs_t, mod_t[1], mod_t[0], norm1_g[l], w_in_b, tabs_t, DB, "tile", True,
                                       nq=T, p_t=True)

        att = _attn_fused(q, k, v)
        sample_attn_inputs = (q_s.reshape(DB, T, ATT_W), k_s.reshape(DB, T, ATT_W), v_s.reshape(DB, T, ATT_W),
                              jnp.transpose(cache_k[l], (0, 2, 3, 1)), jnp.transpose(cache_v[l], (0, 2, 3, 1)))
        rw, s_p, shift_p, att_s = _rwkv(P, jnp.zeros((B, RWKV_IN), F32),
                                        jnp.zeros((B, RWKV_HEADS, HEAD_DIM, HEAD_DIM), F32), lw, RWKV_CHUNK,
                                        sample_attn_inputs)
        hp = _mlp(att, rw, hp, gt1_p, sc2_p, sh2_p, gt2_p, norm2_g[l], normf_g, wo_b, w1_b, w2_b,
                  ROW_TILE, False, last)
        keep = min(max(w for w, _ in DIL_BRANCHES), S)
        win = lambda t: jnp.transpose(t.reshape(B, ATT_HEADS, HEAD_DIM, S)[..., S - keep:], (0, 3, 1, 2))
        outs[0].append(win(kt))
        outs[1].append(win(vt))
        outs[2].append(s_p)
        outs[3].append(shift_p)

        rw_t, s_t = _rwkv_scan(pt_s.reshape(T, RWKV_IN, DB), state_shift[l],
                               jnp.transpose(state_wkv[l], (1, 2, 3, 0)), lw)
        rw_s = jnp.transpose(rw_t, (2, 0, 1)).astype(BF16)
        hs = _mlp(att_s.reshape(1, DB * T, ATT_W), rw_s.reshape(1, DB * T, RWKV_W), hs, gt1_s, sc2_s, sh2_s, gt2_s,
                  norm2_g[l], normf_g, wo_b, w1_b, w2_b, tm_s, True, last)
        new = lambda t: jnp.transpose(t.reshape(T, ATT_HEADS, HEAD_DIM, DB), (3, 0, 1, 2))
        outs[4].append(new(kt_s))
        outs[5].append(new(vt_s))
        outs[6].append(jnp.transpose(s_t, (3, 0, 1, 2)))
        outs[7].append(P_s.reshape(DB, T, RWKV_IN)[:, T - 1])
    stack = lambda xs: jnp.stack(xs).astype(dt)
    return (hp.astype(dt), hs.reshape(DB, T, D).astype(x_sample.dtype), *[stack(o) for o in outs])
```

```python
import functools

import numpy as np
import jax
import jax.numpy as jnp
from jax import lax
from jax.experimental import pallas as pl
from jax.experimental.pallas import tpu as pltpu

F32 = jnp.float32
BF16 = jnp.bfloat16

HEAD_DIM = 64
ATT_HEADS = 8
ATT_W = ATT_HEADS * HEAD_DIM
RWKV_HEADS = 8
RWKV_W = RWKV_HEADS * HEAD_DIM
DIL_BRANCHES = ((128, 1), (512, 4), (2048, 16))
ROT_DIM = HEAD_DIM // 4
ROPE_THETA = 500000.0
DECAY_LORA = 64
AAA_LORA = 64
GATE_LORA = 128
RWKV_IN = 3 * RWKV_W + DECAY_LORA + AAA_LORA + GATE_LORA
NORM_EPS = 1e-6
LNX_EPS = 64e-5
PAST_LEN = 8192
ATT_BLOCK = 128
NEG_BIG = -1e30
LOG2_E = 1.4426950408889634
VMEM_LIMIT = 56 * 1024 * 1024
ROW_TILE = 512
RWKV_CHUNK = 64
RWKV_CHUNKS_PER_STEP = 4
ATT_GROUP = 8


def _dot(a, b):
    return jnp.dot(a, b, preferred_element_type=F32)


def _dot_nt(a, b):
    return lax.dot_general(a, b, (((1,), (1,)), ((), ())), preferred_element_type=F32)


def _dot_tn(a, b):
    return lax.dot_general(a, b, (((0,), (0,)), ((), ())), preferred_element_type=F32)


def _split2(x):
    hi = x.astype(BF16)
    lo = (x - hi.astype(F32)).astype(BF16)
    return hi, lo


def _head_sums(x, e):
    C, W = x.shape
    n = W // e.shape[0]
    xb = x.astype(BF16)
    o = _dot(jnp.concatenate([xb[:, e.shape[0] * p:e.shape[0] * (p + 1)] for p in range(n)], axis=0), e)
    return jnp.concatenate([o[C * p:C * (p + 1)] for p in range(n)], axis=1)


def _params(*sem):
    return pltpu.CompilerParams(dimension_semantics=sem, vmem_limit_bytes=VMEM_LIMIT)


def _ada_kernel(c_ref, w_ref, b_ref, o_ref):
    c = c_ref[...]
    s = c * jax.nn.sigmoid(c)
    o_ref[...] = _dot(s.astype(BF16), w_ref[...].astype(BF16)) + b_ref[...]


def _ada(c, w_ada, b_ada):
    n, d = c.shape
    cols = w_ada.shape[1]
    tn = 512
    return pl.pallas_call(
        _ada_kernel,
        grid=(cols // tn,),
        in_specs=[pl.BlockSpec((n, d), lambda j: (0, 0)),
                  pl.BlockSpec((d, tn), lambda j: (0, j)),
                  pl.BlockSpec((1, tn), lambda j: (0, j))],
        out_specs=pl.BlockSpec((n, tn), lambda j: (0, j)),
        out_shape=jax.ShapeDtypeStruct((n, cols), F32),
        compiler_params=_params("arbitrary"),
        name="ada",
    )(c, w_ada, b_ada.reshape(1, cols))


def _rope(t, cos, s1, s2):
    parts = []
    for c in range(ATT_W // 128):
        tc = t[:, 128 * c:128 * (c + 1)]
        parts.append(tc * cos + pltpu.roll(tc, 128 - ROT_DIM // 2, 1) * s1 + pltpu.roll(tc, ROT_DIM // 2, 1) * s2)
    return jnp.concatenate(parts, axis=1)


def _inproj_kernel(x_ref, sc_ref, sh_ref, g_ref, w_ref, cos_ref, s1_ref, s2_ref, *out_refs,
                   transposed, p_t):
    q_ref, k_ref, v_ref, p_ref = out_refs[:4]
    rest = list(out_refs[4:])
    x = x_ref[...]
    ms = jnp.mean(x * x, axis=-1, keepdims=True)
    h = x * lax.rsqrt(ms + NORM_EPS) * g_ref[...]
    h = (h * (1.0 + sc_ref[...]) + sh_ref[...]).astype(BF16)
    cos, s1, s2 = cos_ref[...], s1_ref[...], s2_ref[...]
    q = _rope(_dot(h, w_ref[:, 0:ATT_W]), cos, s1, s2)
    q_ref[...] = (q * HEAD_DIM ** -0.5).astype(q_ref.dtype)
    k = _rope(_dot(h, w_ref[:, ATT_W:2 * ATT_W]), cos, s1, s2)
    v = _dot(h, w_ref[:, 2 * ATT_W:3 * ATT_W])
    k_ref[...] = k
    v_ref[...] = v
    p = _dot(h, w_ref[:, 3 * ATT_W:])
    p_ref[...] = p
    if transposed:
        rest.pop(0)[...] = k.T
        rest.pop(0)[...] = v.T
    if p_t:
        rest.pop(0)[...] = p.T


def _inproj(x, sc, sh, g, w_in_b, tabs, tm, mod_mode, transposed, nq=1, p_t=False):
    G, R, D = x.shape
    nin = w_in_b.shape[1]
    row_spec = lambda w: pl.BlockSpec((None, tm, w), lambda b, i: (b, i, 0))
    mod_spec = {"batch": pl.BlockSpec((None, 1, D), lambda b, i: (b, 0, 0)),
                "row": pl.BlockSpec((None, tm, D), lambda b, i: (b, i, 0)),
                "tile": pl.BlockSpec((None, tm, D), lambda b, i: (b, 0, 0))}[mod_mode]
    tab_spec = pl.BlockSpec((tm, 128), lambda b, i: (i, 0))
    const = lambda shape: pl.BlockSpec(shape, lambda b, i: (0, 0))
    out_specs = [row_spec(ATT_W)] * 3 + [row_spec(RWKV_IN)]
    out_shape = [jax.ShapeDtypeStruct((G, R, ATT_W), F32)] * 3 + [jax.ShapeDtypeStruct((G, R, RWKV_IN), F32)]
    if transposed:
        spq = (R // nq) // tm
        t_spec = pl.BlockSpec((None, None, ATT_W, tm), lambda b, i: (b, i // spq, 0, i % spq))
        out_specs += [t_spec] * 2
        out_shape += [jax.ShapeDtypeStruct((G, nq, ATT_W, R // nq), F32)] * 2
        if p_t:
            out_specs.append(pl.BlockSpec((None, None, RWKV_IN, tm), lambda b, i: (b, i // spq, 0, i % spq)))
            out_shape.append(jax.ShapeDtypeStruct((G, nq, RWKV_IN, R // nq), F32))
    return pl.pallas_call(
        functools.partial(_inproj_kernel, transposed=transposed, p_t=p_t),
        grid=(G, R // tm),
        in_specs=[row_spec(D), mod_spec, mod_spec, const((1, D)), const((D, nin)),
                  tab_spec, tab_spec, tab_spec],
        out_specs=out_specs,
        out_shape=out_shape,
        compiler_params=_params("arbitrary", "arbitrary"),
        name="inproj",
    )(x, sc, sh, g.reshape(1, D), w_in_b, *tabs)


def _rope_tables(pos):
    half = ROT_DIM // 2
    inv = ROPE_THETA ** (-jnp.arange(half, dtype=F32) * (2.0 / ROT_DIM))
    ang = pos.astype(F32)[:, None] * inv[None, :]
    cos, sin = jnp.cos(ang), jnp.sin(ang)
    n = pos.shape[0]
    ones = jnp.ones((n, HEAD_DIM - ROT_DIM), F32)
    zeros = jnp.zeros((n, HEAD_DIM - ROT_DIM), F32)
    zh = jnp.zeros((n, half), F32)
    ctab = jnp.concatenate([cos, cos, ones], axis=1)
    s1 = jnp.concatenate([-sin, zh, zeros], axis=1)
    s2 = jnp.concatenate([zh, sin, zeros], axis=1)
    return tuple(jnp.tile(t, (1, 128 // HEAD_DIM)) for t in (ctab, s1, s2))


def _attn_fused_kernel(q_ref, k_ref, v_ref, o_ref, acc_scr, m_scr, l_scr, *, dilations, group):
    nw, HD = ATT_BLOCK, HEAD_DIM
    S = q_ref.shape[0]
    n_tiles = S // nw
    lane = lax.broadcasted_iota(jnp.int32, (nw, 2 * HD), 1)
    first_head = lane < HD
    q_idx = lax.broadcasted_iota(jnp.int32, (nw, 2 * HD), 0)
    n_col = nw // HD
    key_idx = [HD * jc + (lane & (HD - 1)) for jc in range(n_col)]
    cur_ok = [q_idx >= key_idx[jc] for jc in range(n_col)]

    def bd(x):
        zero = jnp.zeros_like(x)
        xa, xb = jnp.where(first_head, x, zero), jnp.where(first_head, zero, x)
        parts = []
        for jc in range(n_col):
            parts += [xa[HD * jc:HD * (jc + 1)], xb[HD * jc:HD * (jc + 1)]]
        return jnp.concatenate(parts, axis=0)

    bd_row = lax.broadcasted_iota(jnp.int32, (2 * nw, 2 * HD), 0)
    bd_lane = lax.broadcasted_iota(jnp.int32, (2 * nw, 2 * HD), 1)
    ones_bd = jnp.where(((bd_row // HD) % 2 == 0) == (bd_lane < HD), 1.0, 0.0).astype(BF16)

    def with_ones(v):
        return jnp.concatenate([bd(v.astype(BF16)), ones_bd], axis=1)

    def half_reduce(x, op, fill):
        ra = op(jnp.where(first_head, x, fill), axis=-1, keepdims=True)
        rb = op(jnp.where(first_head, fill, x), axis=-1, keepdims=True)
        return jnp.where(first_head, ra, rb)

    def steps(tiles, use_prev, init, final):
        n = range(len(tiles))
        qs = [(q_ref[tiles[t][0], :] * LOG2_E).astype(BF16) for t in n]
        cols, vbd = [], []
        for t in n:
            rows, prev_rows, has_prev = tiles[t]
            s = _dot_nt(qs[t], bd(k_ref[rows, :].astype(BF16)))
            c = [jnp.where(cur_ok[jc], s[:, 2 * HD * jc:2 * HD * (jc + 1)], NEG_BIG) for jc in range(n_col)]
            vb = [with_ones(v_ref[rows, :])]
            if use_prev:
                sp = _dot_nt(qs[t], bd(k_ref[prev_rows, :].astype(BF16)))
                thresh = jnp.where(has_prev, 0, nw)
                c += [jnp.where((key_idx[jc] - q_idx) >= thresh, sp[:, 2 * HD * jc:2 * HD * (jc + 1)], NEG_BIG)
                      for jc in range(n_col)]
                vb.append(with_ones(v_ref[prev_rows, :]))
            cols.append(c)
            vbd.append(vb)
        m_new, m_old, p = [], [], []
        for t in n:
            m = half_reduce(functools.reduce(jnp.maximum, cols[t]), jnp.max, NEG_BIG)
            if not init:
                m_old.append(m_scr[tiles[t][0], :])
                m = jnp.maximum(m, m_old[t])
            m_new.append(m)
            p.append([jnp.exp2(c - m) for c in cols[t]])
        acc = []
        for t in n:
            a = _dot(jnp.concatenate(p[t][0:n_col], axis=1).astype(BF16), vbd[t][0])
            if use_prev:
                a = a + _dot(jnp.concatenate(p[t][n_col:], axis=1).astype(BF16), vbd[t][1])
            acc.append(a)
        for t in n:
            rows = tiles[t][0]
            a, l = acc[t][:, 0:2 * HD], acc[t][:, 2 * HD:]
            if not init:
                alpha = jnp.exp2(m_old[t] - m_new[t])
                a = a + acc_scr[rows, :] * alpha
                l = l + l_scr[rows, :] * alpha
            if final:
                o_ref[rows, :] = (a / l).astype(o_ref.dtype)
            else:
                acc_scr[rows, :] = a
                m_scr[rows, :] = m_new[t]
                l_scr[rows, :] = l

    n_br = len(dilations)
    for bi, dil in enumerate(dilations):
        init, final = bi == 0, bi == n_br - 1
        use_prev = S // (dil * nw) > 1

        def group_body(g, carry, dil=dil, use_prev=use_prev, init=init, final=final):
            tiles = []
            for i in range(group):
                t = g * group + i
                r, j = lax.rem(t, dil), lax.div(t, dil)
                start = r + j * (dil * nw)
                prev = r + jnp.maximum(j - 1, 0) * (dil * nw)
                if dil > 1:
                    tiles.append((pl.ds(start, nw, stride=dil), pl.ds(prev, nw, stride=dil), j > 0))
                else:
                    tiles.append((pl.ds(pl.multiple_of(start, nw), nw), pl.ds(pl.multiple_of(prev, nw), nw), j > 0))
            steps(tiles, use_prev, init, final)
            return carry

        lax.fori_loop(0, n_tiles // group, group_body, 0, unroll=True)


def _attn_fused(q, k, v):
    B, S, W = q.shape
    nw = ATT_BLOCK
    dilations = tuple(sorted((d for _, d in DIL_BRANCHES), reverse=True))
    for win, d in DIL_BRANCHES:
        assert win == d * nw and S % (d * nw) == 0
    assert (S // nw) % ATT_GROUP == 0
    PW = 2 * HEAD_DIM
    spec = pl.BlockSpec((None, S, PW), lambda b, p: (b, 0, p))
    return pl.pallas_call(
        functools.partial(_attn_fused_kernel, dilations=dilations, group=ATT_GROUP),
        grid=(B, W // PW),
        in_specs=[spec, spec, spec],
        out_specs=spec,
        out_shape=jax.ShapeDtypeStruct((B, S, W), BF16),
        scratch_shapes=[pltpu.VMEM((S, PW), F32) for _ in range(3)],
        compiler_params=_params("arbitrary", "arbitrary"),
        name="attn_fused",
    )(q, k, v)


def _sample_counts(T, WB):
    i = np.arange(T)
    c = np.arange(WB)
    delta = WB + i[None, :] - c[:, None]
    cnt_c = np.zeros((WB, T), np.float32)
    diff = i[None, :] - i[:, None]
    cnt_n = np.zeros((T, T), np.float32)
    for win, dil in DIL_BRANCHES:
        cnt_c += ((delta > 0) & (delta % dil == 0) & (delta <= win)).astype(np.float32)
        cnt_n += ((diff >= 0) & (diff % dil == 0) & (diff <= win)).astype(np.float32)
    return cnt_c, cnt_n


def _sample_attn_scores(q_ref, kn_ref, vn_ref, kt_ref, vt_ref, cc_ref, cn_ref):
    cc, cn = cc_ref[...], cn_ref[...]
    nb = q_ref.shape[0]
    items = [(b, h) for b in range(nb) for h in range(ATT_HEADS)]
    sl = lambda h: slice(HEAD_DIM * h, HEAD_DIM * (h + 1))
    qh = {(b, h): q_ref[b, :, sl(h)] for b, h in items}
    s_c = {i: jnp.where(cc > 0, _dot(qh[i].astype(BF16), kt_ref[i[0], i[1]].astype(BF16)), NEG_BIG)
           for i in items}
    s_n = {i: jnp.where(cn > 0, _dot_nt(qh[i], kn_ref[i[0], :, sl(i[1])]), NEG_BIG) for i in items}
    return s_c, s_n


def _sample_attn_finish(scores, q_ref, kn_ref, vn_ref, kt_ref, vt_ref, cc_ref, cn_ref, o_ref):
    s_c, s_n = scores
    cc, cn = cc_ref[...], cn_ref[...]
    nb = q_ref.shape[0]
    items = list(s_c)
    sl = lambda h: slice(HEAD_DIM * h, HEAD_DIM * (h + 1))
    p_c, p_n, l = {}, {}, {}
    for i in items:
        m = jnp.maximum(jnp.max(s_c[i], axis=-1, keepdims=True), jnp.max(s_n[i], axis=-1, keepdims=True))
        p_c[i] = cc * jnp.exp(s_c[i] - m)
        p_n[i] = cn * jnp.exp(s_n[i] - m)
        l[i] = jnp.sum(p_c[i], axis=-1, keepdims=True) + jnp.sum(p_n[i], axis=-1, keepdims=True)
    acc = {i: _dot_nt(p_c[i].astype(BF16), vt_ref[i[0], i[1]].astype(BF16)) + _dot(p_n[i], vn_ref[i[0], :, sl(i[1])])
           for i in items}
    for b in range(nb):
        o_ref[b] = jnp.concatenate([acc[(b, h)] / l[(b, h)] for h in range(ATT_HEADS)], axis=1).astype(o_ref.dtype)


def _sample_attn_operands(q, k_new, v_new, cache_kt, cache_vt, n_steps, step_index):
    DB, T, W = q.shape
    WB = cache_kt.shape[-1]
    assert DB % n_steps == 0, "the requests must split evenly over the host call's grid steps"
    nb = DB // n_steps
    cnt_c, cnt_n = _sample_counts(T, WB)
    cc = jnp.asarray(cnt_c.T)
    cn = jnp.asarray(cnt_n.T)
    new_spec = pl.BlockSpec((nb, T, W), lambda *g: (step_index(*g), 0, 0))
    cache_spec = pl.BlockSpec((nb, ATT_HEADS, HEAD_DIM, WB), lambda *g: (step_index(*g), 0, 0, 0))
    const = lambda arr: pl.BlockSpec(arr.shape, lambda *g: (0, 0))
    args = (q, k_new, v_new, cache_kt, cache_vt, cc, cn)
    in_specs = [new_spec, new_spec, new_spec, cache_spec, cache_spec, const(cc), const(cn)]
    return args, in_specs, new_spec, jax.ShapeDtypeStruct((DB, T, W), BF16)


def _rwkv_prologue(p_ref, shift_ref, prev_scr, shout_ref, mu_ref, w0_ref, w2_ref, a0_ref, a2_ref, g2_ref,
                   kk_ref, ka_ref, rk_ref, e, chunk):
    C = p_ref.shape[0]
    P = p_ref[...]
    row = lax.broadcasted_iota(jnp.int32, (C, 1), 0)
    p_prev = jnp.where(row == 0, prev_scr[...], pltpu.roll(P, 1, 0))
    pm = P + (p_prev - P) * mu_ref[...]
    prev_scr[...] = P[C - 1:C, :]
    shout_ref[...] = P[C - 1:C, :]

    W = RWKV_W
    r, k, v = pm[:, 0:W], pm[:, W:2 * W], pm[:, 2 * W:3 * W]
    lora_wa = pm[:, 3 * W:3 * W + DECAY_LORA + AAA_LORA]
    gl = pm[:, 3 * W + DECAY_LORA + AAA_LORA:]
    wpre = w0_ref[...] + _dot(jnp.tanh(lora_wa).astype(BF16), w2_ref[...])
    neg = -wpre
    softplus = jnp.maximum(neg, 0.0) + jnp.log(1.0 + jnp.exp(-jnp.abs(neg)))
    logw = -jnp.exp(-softplus - 0.5)
    a = jax.nn.sigmoid(a0_ref[...] + _dot(lora_wa.astype(BF16), a2_ref[...]))
    g = _dot(jax.nn.sigmoid(gl).astype(BF16), g2_ref[...])
    kkr = k * kk_ref[...]
    nrm = jnp.sqrt(_head_sums(kkr * kkr, e))
    kk = kkr / jnp.maximum(nrm, 1e-12)
    kmod = k * (1.0 + (a - 1.0) * ka_ref[...])
    bonus = _head_sums(r * kmod * rk_ref[...], e) * v

    ti = lax.broadcasted_iota(jnp.int32, (C, C), 0)
    si = lax.broadcasted_iota(jnp.int32, (C, C), 1)
    tri_b = jnp.where((ti >= si) & (lax.div(ti, chunk) == lax.div(si, chunk)), 1.0, 0.0).astype(BF16)
    lw_h, lw_l = _split2(logw)
    L = _dot(tri_b, lw_h) + _dot(tri_b, lw_l)
    e_l = jnp.exp(L)
    e_nl = jnp.exp(-L)
    a_t = -kk * jnp.exp(L - logw)
    r_t = r * e_l
    b_t = kk * a * e_nl
    k_t = kmod * e_nl
    return v, a_t, r_t, b_t, k_t, e_l, bonus, g


def _rwkv_epilogue(o, bonus, g, e, lg_ref, lb_ref, rw_ref):
    mean = _head_sums(o, e) * (1.0 / HEAD_DIM)
    d = o - mean
    var = _head_sums(d * d, e) * (1.0 / HEAD_DIM)
    on = d * lax.rsqrt(var + LNX_EPS) * lg_ref[...] + lb_ref[...]
    rw_ref[...] = ((on + bonus) * g).astype(rw_ref.dtype)


def _rwkv_pair_kernel(*refs, n_double, n_side_in):
    (p_ref, shift_ref, s0_ref, mu_ref, w0_ref, w2_ref, a0_ref, a2_ref, g2_ref,
     kk_ref, ka_ref, rk_ref, lg_ref, lb_ref, e_ref) = refs[:15]
    side_in = refs[15:15 + n_side_in]
    rw_ref, sout_ref, shout_ref, side_out, s_scr, prev_scr = refs[15 + n_side_in:]

    C = HEAD_DIM
    n_sub = p_ref.shape[0] // C
    n_pairs = RWKV_HEADS // 2
    PW = 2 * HEAD_DIM

    @pl.when(pl.program_id(1) == 0)
    def _():
        s_scr[...] = s0_ref[...]
        prev_scr[...] = shift_ref[...]

    side_scores = _sample_attn_scores(*side_in)
    e = e_ref[...]
    v, a_t, r_t, b_t, k_t, e_l, bonus, g = _rwkv_prologue(
        p_ref, shift_ref, prev_scr, shout_ref, mu_ref, w0_ref, w2_ref, a0_ref, a2_ref, g2_ref,
        kk_ref, ka_ref, rk_ref, e, C)
    _sample_attn_finish(side_scores, *side_in, side_out)

    lane = lax.broadcasted_iota(jnp.int32, (C, PW), 1)
    ti = lax.broadcasted_iota(jnp.int32, (C, PW), 0)
    si = lane & (HEAD_DIM - 1)
    first_head = lane < HEAD_DIM
    tri_incl = ti >= si
    tri_strict = ti > si
    eye = jnp.where(ti == si, 1.0, 0.0)
    rows2 = lax.broadcasted_iota(jnp.int32, (2 * C, PW), 0)
    lanes2 = lax.broadcasted_iota(jnp.int32, (2 * C, PW), 1)
    same_head = (rows2 < C) == (lanes2 < HEAD_DIM)

    def bd(x):
        zero = jnp.zeros_like(x)
        return jnp.concatenate([jnp.where(first_head, x, zero), jnp.where(first_head, zero, x)], axis=0)

    def dot_bd(x, y):
        return _dot(x.astype(BF16), bd(y.astype(BF16)))

    items = [(c, p) for c in range(n_sub) for p in range(n_pairs)]
    tile = lambda t, c, p: t[C * c:C * (c + 1), PW * p:PW * (p + 1)]
    ar = {i: jnp.concatenate([tile(a_t, *i), tile(r_t, *i)], axis=0).astype(BF16) for i in items}
    bk = {i: jnp.concatenate([tile(b_t, *i), tile(k_t, *i)], axis=0).astype(BF16) for i in items}
    gm = {i: _dot_nt(ar[i], jnp.concatenate([bd(bk[i][0:C]), bd(bk[i][C:2 * C])], axis=0)) for i in items}
    m_ab = {i: jnp.where(tri_strict, gm[i][0:C, 0:PW], 0.0) for i in items}
    m_ak = {i: jnp.where(tri_strict, gm[i][0:C, PW:2 * PW], 0.0).astype(BF16) for i in items}
    n_r = {i: jnp.concatenate([jnp.where(tri_incl, gm[i][C:2 * C, 0:PW], 0.0),
                               jnp.where(tri_incl, gm[i][C:2 * C, PW:2 * PW], 0.0)], axis=1).astype(BF16)
           for i in items}
    t_inv = {i: eye + m_ab[i] for i in items}
    mp = m_ab
    for _ in range(n_double):
        mp = {i: dot_bd(mp[i], mp[i]) for i in items}
        t_inv = {i: t_inv[i] + dot_bd(t_inv[i], mp[i]) for i in items}
    v_b = {i: tile(v, *i).astype(BF16) for i in items}
    v_bd = {i: bd(v_b[i]) for i in items}
    mv = {i: _dot(m_ak[i], v_bd[i]) for i in items}
    state = [s_scr[p] for p in range(n_pairs)]
    o_rows = []
    for c in range(n_sub):
        its = [(c, p) for p in range(n_pairs)]
        ars = {i: _dot_nt(ar[i], state[i[1]].astype(BF16)) for i in its}
        u_b = {i: dot_bd(t_inv[i], ars[i][0:C] + mv[i]).astype(BF16) for i in its}
        o = [ars[i][C:2 * C] + _dot(n_r[i], jnp.concatenate([bd(u_b[i]), v_bd[i]], axis=0)) for i in its]
        o_rows.append(jnp.concatenate(o, axis=1))
        for i in its:
            upd = jnp.where(same_head, _dot_tn(jnp.concatenate([u_b[i], v_b[i]], axis=0), bk[i]), 0.0)
            state[i[1]] = (state[i[1]] + upd) * e_l[C * (c + 1) - 1:C * (c + 1), PW * i[1]:PW * (i[1] + 1)]
    for p in range(n_pairs):
        s_scr[p] = state[p]
        sout_ref[p] = state[p]
    _rwkv_epilogue(jnp.concatenate(o_rows, axis=0), bonus, g, e, lg_ref, lb_ref, rw_ref)


def _rwkv(P, shift0, s0, lw, C, sample_attn_inputs):
    B, T, _ = P.shape
    assert C == HEAD_DIM, "the pair-packed chunk kernel needs chunk length == head dim"
    rows = C * RWKV_CHUNKS_PER_STEP
    assert T % rows == 0
    n_double = max(int(np.log2(C)) - 1, 0)
    W = RWKV_W
    e = np.kron(np.eye(2, dtype=np.float32), np.ones((HEAD_DIM, HEAD_DIM), np.float32))
    zpad = jnp.zeros((DECAY_LORA, W), F32)
    w2p = jnp.concatenate([lw["w2"], jnp.zeros((AAA_LORA, W), F32)], axis=0).astype(BF16)
    a2p = jnp.concatenate([zpad, lw["a2"]], axis=0).astype(BF16)
    row = lambda t: t.reshape(1, -1)
    consts = [row(lw["mu"]), row(lw["w0"]), w2p, row(lw["a0"]), a2p, lw["g2"].astype(BF16),
              row(lw["k_k"]), row(lw["k_a"]), row(lw["r_k"]), row(lw["lnx_g"]), row(lw["lnx_b"]),
              jnp.asarray(e, BF16)]
    sp = s0.reshape(B, RWKV_HEADS // 2, 2, HEAD_DIM, HEAD_DIM)
    z = jnp.zeros_like(sp[:, :, 0])
    s0 = jnp.concatenate([jnp.concatenate([sp[:, :, 0], z], axis=-1),
                          jnp.concatenate([z, sp[:, :, 1]], axis=-1)], axis=-2)
    st_shape = s0.shape[1:]
    const = lambda arr: pl.BlockSpec(arr.shape, lambda b, c: (0, 0))
    st_spec = pl.BlockSpec((None,) + st_shape, lambda b, c: (b, 0, 0, 0))
    sh_spec = pl.BlockSpec((None, 1, RWKV_IN), lambda b, c: (b, 0, 0))
    n_c = T // rows
    side_args, side_specs, side_out_spec, side_out_shape = _sample_attn_operands(
        *sample_attn_inputs, B * n_c, lambda b, c: b * n_c + c)
    rw, s_out, sh_out, att_side = pl.pallas_call(
        functools.partial(_rwkv_pair_kernel, n_double=n_double, n_side_in=len(side_args)),
        grid=(B, n_c),
        in_specs=[pl.BlockSpec((None, rows, RWKV_IN), lambda b, c: (b, c, 0)), sh_spec, st_spec]
                 + [const(t) for t in consts] + side_specs,
        out_specs=[pl.BlockSpec((None, rows, W), lambda b, c: (b, c, 0)), st_spec, sh_spec, side_out_spec],
        out_shape=[jax.ShapeDtypeStruct((B, T, W), BF16),
                   jax.ShapeDtypeStruct((B,) + st_shape, F32),
                   jax.ShapeDtypeStruct((B, 1, RWKV_IN), F32), side_out_shape],
        scratch_shapes=[pltpu.VMEM(st_shape, F32),
                        pltpu.VMEM((1, RWKV_IN), F32)],
        compiler_params=_params("arbitrary", "arbitrary"),
        name=f"rwkv_c{C}",
    )(P, shift0.reshape(B, 1, RWKV_IN), s0, *consts, *side_args)
    s_out = jnp.stack([s_out[:, :, :HEAD_DIM, :HEAD_DIM], s_out[:, :, HEAD_DIM:, HEAD_DIM:]], axis=2)
    s_out = s_out.reshape(B, RWKV_HEADS, HEAD_DIM, HEAD_DIM)
    return rw, s_out, sh_out.reshape(B, RWKV_IN), att_side


def _rwkv_scan_kernel(r_ref, k_ref, v_ref, lo_ref, sr_ref, sk_ref, sv_ref, slo_ref, s_ref,
                      mur_ref, muk_ref, muv_ref, mulo_ref, w0_ref, w2t_ref, a0_ref, a2t_ref, g2t_ref,
                      kk_ref, ka_ref, rk_ref, lg_ref, lb_ref,
                      rw_ref, sout_ref,
                      w_scr, nkk_scr, bb_scr, km_scr, rr_scr, vv_scr, bonus_scr, g_scr, o_scr):
    T = r_ref.shape[0]

    def shifted(ref, s0_ref, mu_ref, t):
        cur = ref[t]
        prev = s0_ref[...] if t == 0 else ref[t - 1]
        return cur + (prev - cur) * mu_ref[...]

    for t in range(T):
        r = shifted(r_ref, sr_ref, mur_ref, t)
        k = shifted(k_ref, sk_ref, muk_ref, t)
        v = shifted(v_ref, sv_ref, muv_ref, t)
        lo = shifted(lo_ref, slo_ref, mulo_ref, t)
        wl, al, gl = lo[0:DECAY_LORA], lo[DECAY_LORA:DECAY_LORA + AAA_LORA], lo[DECAY_LORA + AAA_LORA:]
        neg = -(w0_ref[...] + _dot(w2t_ref[...], jnp.tanh(wl).astype(BF16)))
        softplus = jnp.maximum(neg, 0.0) + jnp.log(1.0 + jnp.exp(-jnp.abs(neg)))
        w = jnp.exp(-jnp.exp(-softplus - 0.5))
        a = jax.nn.sigmoid(a0_ref[...] + _dot(a2t_ref[...], al.astype(BF16)))
        g_scr[t] = _dot(g2t_ref[...], jax.nn.sigmoid(gl).astype(BF16))
        kkr = k * kk_ref[...]
        kk = kkr / jnp.maximum(jnp.sqrt(jnp.sum(kkr * kkr, axis=0, keepdims=True)), 1e-12)
        kmod = k * (1.0 + (a - 1.0) * ka_ref[...])
        bonus_scr[t] = jnp.sum(r * kmod * rk_ref[...], axis=0, keepdims=True) * v
        w_scr[t] = w
        nkk_scr[t] = -kk
        bb_scr[t] = kk * a
        km_scr[t] = kmod
        rr_scr[t] = r
        vv_scr[t] = v

    def row(i, carry):
        s = s_ref[i]
        for t in range(T):
            sa = jnp.sum(s * nkk_scr[t], axis=0, keepdims=True)
            s = s * w_scr[t] + sa * bb_scr[t] + vv_scr[t, pl.ds(i, 1), :] * km_scr[t]
            o_scr[t, pl.ds(i, 1), :] = jnp.sum(s * rr_scr[t], axis=0, keepdims=True)
        sout_ref[i] = s
        return carry

    lax.fori_loop(0, HEAD_DIM, row, 0, unroll=4)
    o = o_scr[...]
    mean = jnp.mean(o, axis=1, keepdims=True)
    d = o - mean
    var = jnp.mean(d * d, axis=1, keepdims=True)
    on = d * lax.rsqrt(var + LNX_EPS) * lg_ref[...] + lb_ref[...]
    rw_ref[...] = ((on + bonus_scr[...]) * g_scr[...]).astype(rw_ref.dtype)


def _rwkv_scan(pt, shift0, s0t, lw):
    T, _, DB = pt.shape
    W, HD = RWKV_W, HEAD_DIM
    lora_w = DECAY_LORA + AAA_LORA + GATE_LORA
    assert (3 * W) % lora_w == 0
    col = lambda t: jnp.broadcast_to(t.reshape(-1, 1), (t.size, DB))
    shift_t = shift0.T
    mu_c = col(lw["mu"])
    head = lambda base: (lambda h: (base + h, 0))
    pt_blk = lambda base: pl.BlockSpec((T, HD, DB), lambda h: (0, base + h, 0))
    blk2 = lambda base: pl.BlockSpec((HD, DB), head(base))
    lo_pt = pl.BlockSpec((T, lora_w, DB), lambda h: (0, 3 * W // lora_w, 0))
    lo_2d = pl.BlockSpec((lora_w, DB), lambda h: (3 * W // lora_w, 0))
    nh = W // HD
    st_spec = pl.BlockSpec((None, HD, HD, DB), lambda h: (h, 0, 0, 0))
    wt_spec = lambda n: pl.BlockSpec((HD, n), lambda h: (h, 0))
    scr = lambda: pltpu.VMEM((T, HD, DB), F32)
    return pl.pallas_call(
        _rwkv_scan_kernel,
        grid=(nh,),
        in_specs=[pt_blk(0), pt_blk(nh), pt_blk(2 * nh), lo_pt,
                  blk2(0), blk2(nh), blk2(2 * nh), lo_2d, st_spec,
                  blk2(0), blk2(nh), blk2(2 * nh), lo_2d,
                  blk2(0), wt_spec(DECAY_LORA), blk2(0), wt_spec(AAA_LORA), wt_spec(GATE_LORA),
                  blk2(0), blk2(0), blk2(0), blk2(0), blk2(0)],
        out_specs=[pl.BlockSpec((T, HD, DB), lambda h: (0, h, 0)), st_spec],
        out_shape=[jax.ShapeDtypeStruct((T, W, DB), F32), jax.ShapeDtypeStruct(s0t.shape, F32)],
        scratch_shapes=[scr() for _ in range(9)],
        compiler_params=_params("arbitrary"),
        name="rwkv_scan",
    )(pt, pt, pt, pt, shift_t, shift_t, shift_t, shift_t, s0t,
      mu_c, mu_c, mu_c, mu_c, col(lw["w0"]), lw["w2"].T.astype(BF16), col(lw["a0"]), lw["a2"].T.astype(BF16),
      lw["g2"].T.astype(BF16), col(lw["k_k"]), col(lw["k_a"]), col(lw["r_k"]), col(lw["lnx_g"]), col(lw["lnx_b"]))


def _mlp_kernel(att_ref, rw_ref, x_ref, gt1_ref, sc2_ref, sh2_ref, gt2_ref, g2_ref, gf_ref,
                wo_ref, w1_ref, w2_ref, y_ref, *, final_norm, ff_chunk):
    half = wo_ref.shape[0] // 2
    mix = _dot(att_ref[...], wo_ref[0:half, :]) + _dot(rw_ref[...], wo_ref[half:, :])
    x1 = x_ref[...] + gt1_ref[...] * mix
    ms = jnp.mean(x1 * x1, axis=-1, keepdims=True)
    h2 = x1 * lax.rsqrt(ms + NORM_EPS) * g2_ref[...]
    h2 = (h2 * (1.0 + sc2_ref[...]) + sh2_ref[...]).astype(BF16)
    acc = jnp.zeros_like(x1)
    for c in range(w1_ref.shape[1] // ff_chunk):
        hid = _dot(h2, w1_ref[:, c * ff_chunk:(c + 1) * ff_chunk])
        hid = jnp.square(jnp.maximum(hid, 0.0)).astype(BF16)
        acc = acc + _dot(hid, w2_ref[c * ff_chunk:(c + 1) * ff_chunk, :])
    x2 = x1 + gt2_ref[...] * acc
    if final_norm:
        ms2 = jnp.mean(x2 * x2, axis=-1, keepdims=True)
        x2 = x2 * lax.rsqrt(ms2 + NORM_EPS) * gf_ref[...]
    y_ref[...] = x2


def _mlp(att, rw, x, gt1, sc2, sh2, gt2, g2, gf, wo_b, w1_b, w2_b, tm, per_row_mod, final_norm):
    G, R, D = x.shape
    dff = w1_b.shape[1]
    row_spec = lambda w: pl.BlockSpec((None, tm, w), lambda b, i: (b, i, 0))
    if per_row_mod:
        mod_spec = pl.BlockSpec((None, tm, D), lambda b, i: (b, i, 0))
    else:
        mod_spec = pl.BlockSpec((None, 1, D), lambda b, i: (b, 0, 0))
    const = lambda shape: pl.BlockSpec(shape, lambda b, i: (0, 0), pipeline_mode=pl.Buffered(1))
    return pl.pallas_call(
        functools.partial(_mlp_kernel, final_norm=final_norm, ff_chunk=1024),
        grid=(G, R // tm),
        in_specs=[row_spec(att.shape[-1]), row_spec(rw.shape[-1]), row_spec(D),
                  mod_spec, mod_spec, mod_spec, mod_spec, const((1, D)), const((1, D)),
                  const(wo_b.shape), const((D, dff)), const((dff, D))],
        out_specs=row_spec(D),
        out_shape=jax.ShapeDtypeStruct((G, R, D), F32),
        compiler_params=_params("arbitrary", "arbitrary"),
        name="mlp",
    )(att, rw, x, gt1, sc2, sh2, gt2, g2.reshape(1, D), gf.reshape(1, D), wo_b, w1_b, w2_b)


def kernel(x_prompt, x_sample, cache_k, cache_v, state_wkv, state_shift, c_prompt, c_sample, norm1_g, norm2_g, w_ada, b_ada, w_in, w_out, mu, w0, w2, a0, a2, g2, k_k, k_a, r_k, lnx_g, lnx_b, w_ff1, w_ff2, normf_g):
    B, S, D = x_prompt.shape
    DB, T, _ = x_sample.shape
    depth = w_in.shape[0]
    dt = x_prompt.dtype
    tabs_p = _rope_tables(jnp.arange(S))
    tabs_s = tuple(jnp.tile(t, (DB, 1)) for t in _rope_tables(PAST_LEN + jnp.arange(T)))
    tabs_t = tuple(jnp.repeat(t, DB, axis=0) for t in _rope_tables(PAST_LEN + jnp.arange(T)))
    c_all = jnp.concatenate([c_prompt, c_sample], axis=0)
    hp = x_prompt
    hs = x_sample.reshape(1, DB * T, D)
    outs = [[] for _ in range(8)]
    tm_s = min(ROW_TILE, DB * T)
    for l in range(depth):
        last = l == depth - 1
        mod = _ada(c_all, w_ada[l], b_ada[l])
        mod_p = [m.reshape(B, 1, D) for m in jnp.split(mod[:B], 6, axis=-1)]
        mod_s = [jnp.repeat(m, T, axis=0).reshape(1, DB * T, D) for m in jnp.split(mod[B:], 6, axis=-1)]
        mod_t = [m.reshape(1, DB, D) for m in jnp.split(mod[B:], 6, axis=-1)]
        w_in_b, wo_b = w_in[l].astype(BF16), w_out[l].astype(BF16)
        w1_b, w2_b = w_ff1[l].astype(BF16), w_ff2[l].astype(BF16)
        lw = dict(mu=mu[l], w0=w0[l], w2=w2[l], a0=a0[l], a2=a2[l], g2=g2[l], k_k=k_k[l], k_a=k_a[l],
                  r_k=r_k[l], lnx_g=lnx_g[l], lnx_b=lnx_b[l])

        sh1_p, sc1_p, gt1_p, sh2_p, sc2_p, gt2_p = mod_p
        sh1_s, sc1_s, gt1_s, sh2_s, sc2_s, gt2_s = mod_s
        q, k, v, P, kt, vt = _inproj(hp, sc1_p, sh1_p, norm1_g[l], w_in_b, tabs_p, ROW_TILE, "batch", True)
        q_s, k_s, v_s, P_s = _inproj(hs, sc1_s, sh1_s, norm1_g[l], w_in_b, tabs_s, tm_s, "row", False)
        hs_t = jnp.transpose(hs.reshape(DB, T, D), (1, 0, 2)).reshape(1, T * DB, D)
        *_, kt_s, vt_s, pt_s = _inproj(hs_t, mod_t[1], mod_t[0], norm1_g[l], w_in_b, tabs_t, DB, "tile", True,
                                       nq=T, p_t=True)

        att = _attn_fused(q, k, v)
        sample_attn_inputs = (q_s.reshape(DB, T, ATT_W), k_s.reshape(DB, T, ATT_W), v_s.reshape(DB, T, ATT_W),
                              jnp.transpose(cache_k[l], (0, 2, 3, 1)), jnp.transpose(cache_v[l], (0, 2, 3, 1)))
        rw, s_p, shift_p, att_s = _rwkv(P, jnp.zeros((B, RWKV_IN), F32),
                                        jnp.zeros((B, RWKV_HEADS, HEAD_DIM, HEAD_DIM), F32), lw, RWKV_CHUNK,
                                        sample_attn_inputs)
        hp = _mlp(att, rw, hp, gt1_p, sc2_p, sh2_p, gt2_p, norm2_g[l], normf_g, wo_b, w1_b, w2_b,
                  ROW_TILE, False, last)
        keep = min(max(w for w, _ in DIL_BRANCHES), S)
        win = lambda t: jnp.transpose(t.reshape(B, ATT_HEADS, HEAD_DIM, S)[..., S - keep:], (0, 3, 1, 2))
        outs[0].append(win(kt))
        outs[1].append(win(vt))
        outs[2].append(s_p)
        outs[3].append(shift_p)

        rw_t, s_t = _rwkv_scan(pt_s.reshape(T, RWKV_IN, DB), state_shift[l],
                               jnp.transpose(state_wkv[l], (1, 2, 3, 0)), lw)
        rw_s = jnp.transpose(rw_t, (2, 0, 1)).astype(BF16)
        hs = _mlp(att_s.reshape(1, DB * T, ATT_W), rw_s.reshape(1, DB * T, RWKV_W), hs, gt1_s, sc2_s, sh2_s, gt2_s,
                  norm2_g[l], normf_g, wo_b, w1_b, w2_b, tm_s, True, last)
        new = lambda t: jnp.transpose(t.reshape(T, ATT_HEADS, HEAD_DIM, DB), (3, 0, 1, 2))
        outs[4].append(new(kt_s))
        outs[5].append(new(vt_s))
        outs[6].append(jnp.transpose(s_t, (3, 0, 1, 2)))
        outs[7].append(P_s.reshape(DB, T, RWKV_IN)[:, T - 1])
    stack = lambda xs: jnp.stack(xs).astype(dt)
    return (hp.astype(dt), hs.reshape(DB, T, D).astype(x_sample.dtype), *[stack(o) for o in outs])
```

```python
import functools

import numpy as np
import jax
import jax.numpy as jnp
from jax import lax
from jax.experimental import pallas as pl
from jax.experimental.pallas import tpu as pltpu

F32 = jnp.float32
BF16 = jnp.bfloat16

HEAD_DIM = 64
ATT_HEADS = 8
ATT_W = ATT_HEADS * HEAD_DIM
RWKV_HEADS = 8
RWKV_W = RWKV_HEADS * HEAD_DIM
DIL_BRANCHES = ((128, 1), (512, 4), (2048, 16))
ROT_DIM = HEAD_DIM // 4
ROPE_THETA = 500000.0
DECAY_LORA = 64
AAA_LORA = 64
GATE_LORA = 128
RWKV_IN = 3 * RWKV_W + DECAY_LORA + AAA_LORA + GATE_LORA
NORM_EPS = 1e-6
LNX_EPS = 64e-5
PAST_LEN = 8192
ATT_BLOCK = 128
NEG_BIG = -1e30
LOG2_E = 1.4426950408889634
VMEM_LIMIT = 56 * 1024 * 1024
ROW_TILE = 512
RWKV_CHUNK = 64
RWKV_CHUNKS_PER_STEP = 4
ATT_GROUP = 8
ADA_TILE = 1024


def _dot(a, b):
    return jnp.dot(a, b, preferred_element_type=F32)


def _dot_nt(a, b):
    return lax.dot_general(a, b, (((1,), (1,)), ((), ())), preferred_element_type=F32)


def _dot_tn(a, b):
    return lax.dot_general(a, b, (((0,), (0,)), ((), ())), preferred_element_type=F32)


def _split2(x):
    hi = x.astype(BF16)
    lo = (x - hi.astype(F32)).astype(BF16)
    return hi, lo


def _head_sums(x, e):
    C, W = x.shape
    n = W // e.shape[0]
    xb = x.astype(BF16)
    o = _dot(jnp.concatenate([xb[:, e.shape[0] * p:e.shape[0] * (p + 1)] for p in range(n)], axis=0), e)
    return jnp.concatenate([o[C * p:C * (p + 1)] for p in range(n)], axis=1)


def _params(*sem):
    return pltpu.CompilerParams(dimension_semantics=sem, vmem_limit_bytes=VMEM_LIMIT)


def _ada_kernel(c_ref, w_ref, b_ref, o_ref):
    c = c_ref[...]
    s = c * jax.nn.sigmoid(c)
    o_ref[...] = _dot(s.astype(BF16), w_ref[...].astype(BF16)) + b_ref[...]


def _ada(c, w_ada, b_ada):
    n, d = c.shape
    cols = w_ada.shape[1]
    tn = ADA_TILE
    assert cols % tn == 0
    return pl.pallas_call(
        _ada_kernel,
        grid=(cols // tn,),
        in_specs=[pl.BlockSpec((n, d), lambda j: (0, 0)),
                  pl.BlockSpec((d, tn), lambda j: (0, j)),
                  pl.BlockSpec((1, tn), lambda j: (0, j))],
        out_specs=pl.BlockSpec((n, tn), lambda j: (0, j)),
        out_shape=jax.ShapeDtypeStruct((n, cols), F32),
        compiler_params=_params("arbitrary"),
        name="ada",
    )(c, w_ada, b_ada.reshape(1, cols))


def _rope(t, cos, s1, s2):
    parts = []
    for c in range(ATT_W // 128):
        tc = t[:, 128 * c:128 * (c + 1)]
        parts.append(tc * cos + pltpu.roll(tc, 128 - ROT_DIM // 2, 1) * s1 + pltpu.roll(tc, ROT_DIM // 2, 1) * s2)
    return jnp.concatenate(parts, axis=1)


def _inproj_kernel(x_ref, sc_ref, sh_ref, g_ref, w_ref, cos_ref, s1_ref, s2_ref, *out_refs,
                   transposed, p_t):
    q_ref, k_ref, v_ref, p_ref = out_refs[:4]
    rest = list(out_refs[4:])
    x = x_ref[...]
    ms = jnp.mean(x * x, axis=-1, keepdims=True)
    h = x * lax.rsqrt(ms + NORM_EPS) * g_ref[...]
    h = (h * (1.0 + sc_ref[...]) + sh_ref[...]).astype(BF16)
    cos, s1, s2 = cos_ref[...], s1_ref[...], s2_ref[...]
    q = _rope(_dot(h, w_ref[:, 0:ATT_W]), cos, s1, s2)
    q_ref[...] = (q * HEAD_DIM ** -0.5).astype(q_ref.dtype)
    k = _rope(_dot(h, w_ref[:, ATT_W:2 * ATT_W]), cos, s1, s2)
    v = _dot(h, w_ref[:, 2 * ATT_W:3 * ATT_W])
    k_ref[...] = k
    v_ref[...] = v
    p = _dot(h, w_ref[:, 3 * ATT_W:])
    p_ref[...] = p
    if transposed:
        rest.pop(0)[...] = k.T
        rest.pop(0)[...] = v.T
    if p_t:
        rest.pop(0)[...] = p.T


def _inproj(x, sc, sh, g, w_in_b, tabs, tm, mod_mode, transposed, nq=1, p_t=False):
    G, R, D = x.shape
    nin = w_in_b.shape[1]
    row_spec = lambda w: pl.BlockSpec((None, tm, w), lambda b, i: (b, i, 0))
    mod_spec = {"batch": pl.BlockSpec((None, 1, D), lambda b, i: (b, 0, 0)),
                "tile": pl.BlockSpec((None, tm, D), lambda b, i: (b, 0, 0))}[mod_mode]
    tab_spec = pl.BlockSpec((tm, 128), lambda b, i: (i, 0))
    const = lambda shape: pl.BlockSpec(shape, lambda b, i: (0, 0))
    out_specs = [row_spec(ATT_W)] * 3 + [row_spec(RWKV_IN)]
    out_shape = [jax.ShapeDtypeStruct((G, R, ATT_W), F32)] * 3 + [jax.ShapeDtypeStruct((G, R, RWKV_IN), F32)]
    if transposed:
        spq = (R // nq) // tm
        t_spec = pl.BlockSpec((None, None, ATT_W, tm), lambda b, i: (b, i // spq, 0, i % spq))
        out_specs += [t_spec] * 2
        out_shape += [jax.ShapeDtypeStruct((G, nq, ATT_W, R // nq), F32)] * 2
        if p_t:
            out_specs.append(pl.BlockSpec((None, None, RWKV_IN, tm), lambda b, i: (b, i // spq, 0, i % spq)))
            out_shape.append(jax.ShapeDtypeStruct((G, nq, RWKV_IN, R // nq), F32))
    return pl.pallas_call(
        functools.partial(_inproj_kernel, transposed=transposed, p_t=p_t),
        grid=(G, R // tm),
        in_specs=[row_spec(D), mod_spec, mod_spec, const((1, D)), const((D, nin)),
                  tab_spec, tab_spec, tab_spec],
        out_specs=out_specs,
        out_shape=out_shape,
        compiler_params=_params("arbitrary", "arbitrary"),
        name="inproj",
    )(x, sc, sh, g.reshape(1, D), w_in_b, *tabs)


def _rope_tables(pos):
    half = ROT_DIM // 2
    inv = ROPE_THETA ** (-jnp.arange(half, dtype=F32) * (2.0 / ROT_DIM))
    ang = pos.astype(F32)[:, None] * inv[None, :]
    cos, sin = jnp.cos(ang), jnp.sin(ang)
    n = pos.shape[0]
    ones = jnp.ones((n, HEAD_DIM - ROT_DIM), F32)
    zeros = jnp.zeros((n, HEAD_DIM - ROT_DIM), F32)
    zh = jnp.zeros((n, half), F32)
    ctab = jnp.concatenate([cos, cos, ones], axis=1)
    s1 = jnp.concatenate([-sin, zh, zeros], axis=1)
    s2 = jnp.concatenate([zh, sin, zeros], axis=1)
    return tuple(jnp.tile(t, (1, 128 // HEAD_DIM)) for t in (ctab, s1, s2))


def _attn_fused_kernel(q_ref, k_ref, v_ref, o_ref, acc_scr, m_scr, l_scr, *, dilations, group):
    nw, HD = ATT_BLOCK, HEAD_DIM
    S = q_ref.shape[0]
    n_tiles = S // nw
    lane = lax.broadcasted_iota(jnp.int32, (nw, 2 * HD), 1)
    first_head = lane < HD
    q_idx = lax.broadcasted_iota(jnp.int32, (nw, 2 * HD), 0)
    n_col = nw // HD
    key_idx = [HD * jc + (lane & (HD - 1)) for jc in range(n_col)]
    cur_ok = [q_idx >= key_idx[jc] for jc in range(n_col)]

    def bd(x):
        zero = jnp.zeros_like(x)
        xa, xb = jnp.where(first_head, x, zero), jnp.where(first_head, zero, x)
        parts = []
        for jc in range(n_col):
            parts += [xa[HD * jc:HD * (jc + 1)], xb[HD * jc:HD * (jc + 1)]]
        return jnp.concatenate(parts, axis=0)

    bd_row = lax.broadcasted_iota(jnp.int32, (2 * nw, 2 * HD), 0)
    bd_lane = lax.broadcasted_iota(jnp.int32, (2 * nw, 2 * HD), 1)
    ones_bd = jnp.where(((bd_row // HD) % 2 == 0) == (bd_lane < HD), 1.0, 0.0).astype(BF16)

    def with_ones(v):
        return jnp.concatenate([bd(v.astype(BF16)), ones_bd], axis=1)

    def half_reduce(x, op, fill):
        ra = op(jnp.where(first_head, x, fill), axis=-1, keepdims=True)
        rb = op(jnp.where(first_head, fill, x), axis=-1, keepdims=True)
        return jnp.where(first_head, ra, rb)

    def steps(tiles, use_prev, init, final):
        n = range(len(tiles))
        qs = [(q_ref[tiles[t][0], :] * LOG2_E).astype(BF16) for t in n]
        cols, vbd = [], []
        for t in n:
            rows, prev_rows, has_prev = tiles[t]
            s = _dot_nt(qs[t], bd(k_ref[rows, :].astype(BF16)))
            c = [jnp.where(cur_ok[jc], s[:, 2 * HD * jc:2 * HD * (jc + 1)], NEG_BIG) for jc in range(n_col)]
            vb = [with_ones(v_ref[rows, :])]
            if use_prev:
                sp = _dot_nt(qs[t], bd(k_ref[prev_rows, :].astype(BF16)))
                thresh = jnp.where(has_prev, 0, nw)
                c += [jnp.where((key_idx[jc] - q_idx) >= thresh, sp[:, 2 * HD * jc:2 * HD * (jc + 1)], NEG_BIG)
                      for jc in range(n_col)]
                vb.append(with_ones(v_ref[prev_rows, :]))
            cols.append(c)
            vbd.append(vb)
        m_new, m_old, p = [], [], []
        for t in n:
            m = half_reduce(functools.reduce(jnp.maximum, cols[t]), jnp.max, NEG_BIG)
            if not init:
                m_old.append(m_scr[tiles[t][0], :])
                m = jnp.maximum(m, m_old[t])
            m_new.append(m)
            p.append([jnp.exp2(c - m) for c in cols[t]])
        acc = []
        for t in n:
            a = _dot(jnp.concatenate(p[t][0:n_col], axis=1).astype(BF16), vbd[t][0])
            if use_prev:
                a = a + _dot(jnp.concatenate(p[t][n_col:], axis=1).astype(BF16), vbd[t][1])
            acc.append(a)
        for t in n:
            rows = tiles[t][0]
            a, l = acc[t][:, 0:2 * HD], acc[t][:, 2 * HD:]
            if not init:
                alpha = jnp.exp2(m_old[t] - m_new[t])
                a = a + acc_scr[rows, :] * alpha
                l = l + l_scr[rows, :] * alpha
            if final:
                o_ref[rows, :] = (a / l).astype(o_ref.dtype)
            else:
                acc_scr[rows, :] = a
                m_scr[rows, :] = m_new[t]
                l_scr[rows, :] = l

    n_br = len(dilations)
    for bi, dil in enumerate(dilations):
        init, final = bi == 0, bi == n_br - 1
        use_prev = S // (dil * nw) > 1

        def group_body(g, carry, dil=dil, use_prev=use_prev, init=init, final=final):
            tiles = []
            for i in range(group):
                t = g * group + i
                r, j = lax.rem(t, dil), lax.div(t, dil)
                start = r + j * (dil * nw)
                prev = r + jnp.maximum(j - 1, 0) * (dil * nw)
                if dil > 1:
                    tiles.append((pl.ds(start, nw, stride=dil), pl.ds(prev, nw, stride=dil), j > 0))
                else:
                    tiles.append((pl.ds(pl.multiple_of(start, nw), nw), pl.ds(pl.multiple_of(prev, nw), nw), j > 0))
            steps(tiles, use_prev, init, final)
            return carry

        lax.fori_loop(0, n_tiles // group, group_body, 0, unroll=True)


def _attn_fused(q, k, v):
    B, S, W = q.shape
    nw = ATT_BLOCK
    dilations = tuple(sorted((d for _, d in DIL_BRANCHES), reverse=True))
    for win, d in DIL_BRANCHES:
        assert win == d * nw and S % (d * nw) == 0
    assert (S // nw) % ATT_GROUP == 0
    PW = 2 * HEAD_DIM
    spec = pl.BlockSpec((None, S, PW), lambda b, p: (b, 0, p))
    return pl.pallas_call(
        functools.partial(_attn_fused_kernel, dilations=dilations, group=ATT_GROUP),
        grid=(B, W // PW),
        in_specs=[spec, spec, spec],
        out_specs=spec,
        out_shape=jax.ShapeDtypeStruct((B, S, W), BF16),
        scratch_shapes=[pltpu.VMEM((S, PW), F32) for _ in range(3)],
        compiler_params=_params("arbitrary", "arbitrary"),
        name="attn_fused",
    )(q, k, v)


def _sample_counts(T, WB):
    i = np.arange(T)
    c = np.arange(WB)
    delta = WB + i[None, :] - c[:, None]
    cnt_c = np.zeros((WB, T), np.float32)
    diff = i[None, :] - i[:, None]
    cnt_n = np.zeros((T, T), np.float32)
    for win, dil in DIL_BRANCHES:
        cnt_c += ((delta > 0) & (delta % dil == 0) & (delta <= win)).astype(np.float32)
        cnt_n += ((diff >= 0) & (diff % dil == 0) & (diff <= win)).astype(np.float32)
    return cnt_c, cnt_n


def _sample_attn_scores(q_ref, kn_ref, vn_ref, kt_ref, vt_ref, cc_ref, cn_ref):
    cc, cn = cc_ref[...], cn_ref[...]
    nb = q_ref.shape[0]
    items = [(b, h) for b in range(nb) for h in range(ATT_HEADS)]
    sl = lambda h: slice(HEAD_DIM * h, HEAD_DIM * (h + 1))
    qh = {(b, h): q_ref[b, :, sl(h)] for b, h in items}
    s_c = {i: jnp.where(cc > 0, _dot(qh[i].astype(BF16), kt_ref[i[0], i[1]].astype(BF16)), NEG_BIG)
           for i in items}
    s_n = {i: jnp.where(cn > 0, _dot_nt(qh[i], kn_ref[i[0], :, sl(i[1])]), NEG_BIG) for i in items}
    return s_c, s_n


def _sample_attn_finish(scores, q_ref, kn_ref, vn_ref, kt_ref, vt_ref, cc_ref, cn_ref, o_ref):
    s_c, s_n = scores
    cc, cn = cc_ref[...], cn_ref[...]
    nb = q_ref.shape[0]
    items = list(s_c)
    sl = lambda h: slice(HEAD_DIM * h, HEAD_DIM * (h + 1))
    p_c, p_n, l = {}, {}, {}
    for i in items:
        m = jnp.maximum(jnp.max(s_c[i], axis=-1, keepdims=True), jnp.max(s_n[i], axis=-1, keepdims=True))
        p_c[i] = cc * jnp.exp(s_c[i] - m)
        p_n[i] = cn * jnp.exp(s_n[i] - m)
        l[i] = jnp.sum(p_c[i], axis=-1, keepdims=True) + jnp.sum(p_n[i], axis=-1, keepdims=True)
    acc = {i: _dot_nt(p_c[i].astype(BF16), vt_ref[i[0], i[1]].astype(BF16)) + _dot(p_n[i], vn_ref[i[0], :, sl(i[1])])
           for i in items}
    for b in range(nb):
        o_ref[b] = jnp.concatenate([acc[(b, h)] / l[(b, h)] for h in range(ATT_HEADS)], axis=1).astype(o_ref.dtype)


def _sample_attn_operands(q, k_new, v_new, cache_kt, cache_vt, n_steps, step_index):
    DB, T, W = q.shape
    WB = cache_kt.shape[-1]
    assert DB % n_steps == 0, "the requests must split evenly over the host call's grid steps"
    nb = DB // n_steps
    cnt_c, cnt_n = _sample_counts(T, WB)
    cc = jnp.asarray(cnt_c.T)
    cn = jnp.asarray(cnt_n.T)
    new_spec = pl.BlockSpec((nb, T, W), lambda *g: (step_index(*g), 0, 0))
    cache_spec = pl.BlockSpec((nb, ATT_HEADS, HEAD_DIM, WB), lambda *g: (step_index(*g), 0, 0, 0))
    const = lambda arr: pl.BlockSpec(arr.shape, lambda *g: (0, 0))
    args = (q, k_new, v_new, cache_kt, cache_vt, cc, cn)
    in_specs = [new_spec, new_spec, new_spec, cache_spec, cache_spec, const(cc), const(cn)]
    return args, in_specs, new_spec, jax.ShapeDtypeStruct((DB, T, W), BF16)


def _rwkv_prologue(p_ref, shift_ref, prev_scr, shout_ref, mu_ref, w0_ref, w2_ref, a0_ref, a2_ref, g2_ref,
                   kk_ref, ka_ref, rk_ref, e, chunk):
    C = p_ref.shape[0]
    P = p_ref[...]
    row = lax.broadcasted_iota(jnp.int32, (C, 1), 0)
    p_prev = jnp.where(row == 0, prev_scr[...], pltpu.roll(P, 1, 0))
    pm = P + (p_prev - P) * mu_ref[...]
    prev_scr[...] = P[C - 1:C, :]
    shout_ref[...] = P[C - 1:C, :]

    W = RWKV_W
    r, k, v = pm[:, 0:W], pm[:, W:2 * W], pm[:, 2 * W:3 * W]
    lora_wa = pm[:, 3 * W:3 * W + DECAY_LORA + AAA_LORA]
    gl = pm[:, 3 * W + DECAY_LORA + AAA_LORA:]
    wpre = w0_ref[...] + _dot(jnp.tanh(lora_wa).astype(BF16), w2_ref[...])
    neg = -wpre
    softplus = jnp.maximum(neg, 0.0) + jnp.log(1.0 + jnp.exp(-jnp.abs(neg)))
    logw = -jnp.exp(-softplus - 0.5)
    a = jax.nn.sigmoid(a0_ref[...] + _dot(lora_wa.astype(BF16), a2_ref[...]))
    g = _dot(jax.nn.sigmoid(gl).astype(BF16), g2_ref[...])
    kkr = k * kk_ref[...]
    nrm = jnp.sqrt(_head_sums(kkr * kkr, e))
    kk = kkr / jnp.maximum(nrm, 1e-12)
    kmod = k * (1.0 + (a - 1.0) * ka_ref[...])
    bonus = _head_sums(r * kmod * rk_ref[...], e) * v

    ti = lax.broadcasted_iota(jnp.int32, (C, C), 0)
    si = lax.broadcasted_iota(jnp.int32, (C, C), 1)
    tri_b = jnp.where((ti >= si) & (lax.div(ti, chunk) == lax.div(si, chunk)), 1.0, 0.0).astype(BF16)
    lw_h, lw_l = _split2(logw)
    L = _dot(tri_b, lw_h) + _dot(tri_b, lw_l)
    e_l = jnp.exp(L)
    e_nl = jnp.exp(-L)
    a_t = -kk * jnp.exp(L - logw)
    r_t = r * e_l
    b_t = kk * a * e_nl
    k_t = kmod * e_nl
    return v, a_t, r_t, b_t, k_t, e_l, bonus, g


def _rwkv_epilogue(o, bonus, g, e, lg_ref, lb_ref, rw_ref):
    mean = _head_sums(o, e) * (1.0 / HEAD_DIM)
    d = o - mean
    var = _head_sums(d * d, e) * (1.0 / HEAD_DIM)
    on = d * lax.rsqrt(var + LNX_EPS) * lg_ref[...] + lb_ref[...]
    rw_ref[...] = ((on + bonus) * g).astype(rw_ref.dtype)


def _rwkv_pair_kernel(*refs, n_double, n_side_in):
    (p_ref, shift_ref, s0_ref, mu_ref, w0_ref, w2_ref, a0_ref, a2_ref, g2_ref,
     kk_ref, ka_ref, rk_ref, lg_ref, lb_ref, e_ref) = refs[:15]
    side_in = refs[15:15 + n_side_in]
    rw_ref, sout_ref, shout_ref, side_out, s_scr, prev_scr = refs[15 + n_side_in:]

    C = HEAD_DIM
    n_sub = p_ref.shape[0] // C
    n_pairs = RWKV_HEADS // 2
    PW = 2 * HEAD_DIM

    @pl.when(pl.program_id(1) == 0)
    def _():
        s_scr[...] = s0_ref[...]
        prev_scr[...] = shift_ref[...]

    side_scores = _sample_attn_scores(*side_in)
    e = e_ref[...]
    v, a_t, r_t, b_t, k_t, e_l, bonus, g = _rwkv_prologue(
        p_ref, shift_ref, prev_scr, shout_ref, mu_ref, w0_ref, w2_ref, a0_ref, a2_ref, g2_ref,
        kk_ref, ka_ref, rk_ref, e, C)
    _sample_attn_finish(side_scores, *side_in, side_out)

    lane = lax.broadcasted_iota(jnp.int32, (C, PW), 1)
    ti = lax.broadcasted_iota(jnp.int32, (C, PW), 0)
    si = lane & (HEAD_DIM - 1)
    first_head = lane < HEAD_DIM
    tri_incl = ti >= si
    tri_strict = ti > si
    eye = jnp.where(ti == si, 1.0, 0.0)
    rows2 = lax.broadcasted_iota(jnp.int32, (2 * C, PW), 0)
    lanes2 = lax.broadcasted_iota(jnp.int32, (2 * C, PW), 1)
    same_head = (rows2 < C) == (lanes2 < HEAD_DIM)

    def bd(x):
        zero = jnp.zeros_like(x)
        return jnp.concatenate([jnp.where(first_head, x, zero), jnp.where(first_head, zero, x)], axis=0)

    def dot_bd(x, y):
        return _dot(x.astype(BF16), bd(y.astype(BF16)))

    items = [(c, p) for c in range(n_sub) for p in range(n_pairs)]
    tile = lambda t, c, p: t[C * c:C * (c + 1), PW * p:PW * (p + 1)]
    ar = {i: jnp.concatenate([tile(a_t, *i), tile(r_t, *i)], axis=0).astype(BF16) for i in items}
    bk = {i: jnp.concatenate([tile(b_t, *i), tile(k_t, *i)], axis=0).astype(BF16) for i in items}
    gm = {i: _dot_nt(ar[i], jnp.concatenate([bd(bk[i][0:C]), bd(bk[i][C:2 * C])], axis=0)) for i in items}
    m_ab = {i: jnp.where(tri_strict, gm[i][0:C, 0:PW], 0.0) for i in items}
    m_ak = {i: jnp.where(tri_strict, gm[i][0:C, PW:2 * PW], 0.0).astype(BF16) for i in items}
    n_r = {i: jnp.concatenate([jnp.where(tri_incl, gm[i][C:2 * C, 0:PW], 0.0),
                               jnp.where(tri_incl, gm[i][C:2 * C, PW:2 * PW], 0.0)], axis=1).astype(BF16)
           for i in items}
    t_inv = {i: eye + m_ab[i] for i in items}
    mp = m_ab
    for _ in range(n_double):
        mp = {i: dot_bd(mp[i], mp[i]) for i in items}
        t_inv = {i: t_inv[i] + dot_bd(t_inv[i], mp[i]) for i in items}
    v_b = {i: tile(v, *i).astype(BF16) for i in items}
    v_bd = {i: bd(v_b[i]) for i in items}
    mv = {i: _dot(m_ak[i], v_bd[i]) for i in items}
    state = [s_scr[p] for p in range(n_pairs)]
    o_rows = []
    for c in range(n_sub):
        its = [(c, p) for p in range(n_pairs)]
        ars = {i: _dot_nt(ar[i], state[i[1]].astype(BF16)) for i in its}
        u_b = {i: dot_bd(t_inv[i], ars[i][0:C] + mv[i]).astype(BF16) for i in its}
        o = [ars[i][C:2 * C] + _dot(n_r[i], jnp.concatenate([bd(u_b[i]), v_bd[i]], axis=0)) for i in its]
        o_rows.append(jnp.concatenate(o, axis=1))
        for i in its:
            upd = jnp.where(same_head, _dot_tn(jnp.concatenate([u_b[i], v_b[i]], axis=0), bk[i]), 0.0)
            state[i[1]] = (state[i[1]] + upd) * e_l[C * (c + 1) - 1:C * (c + 1), PW * i[1]:PW * (i[1] + 1)]
    for p in range(n_pairs):
        s_scr[p] = state[p]
        sout_ref[p] = state[p]
    _rwkv_epilogue(jnp.concatenate(o_rows, axis=0), bonus, g, e, lg_ref, lb_ref, rw_ref)


def _rwkv(P, shift0, s0, lw, C, sample_attn_inputs):
    B, T, _ = P.shape
    assert C == HEAD_DIM, "the pair-packed chunk kernel needs chunk length == head dim"
    rows = C * RWKV_CHUNKS_PER_STEP
    assert T % rows == 0
    n_double = max(int(np.log2(C)) - 1, 0)
    W = RWKV_W
    e = np.kron(np.eye(2, dtype=np.float32), np.ones((HEAD_DIM, HEAD_DIM), np.float32))
    zpad = jnp.zeros((DECAY_LORA, W), F32)
    w2p = jnp.concatenate([lw["w2"], jnp.zeros((AAA_LORA, W), F32)], axis=0).astype(BF16)
    a2p = jnp.concatenate([zpad, lw["a2"]], axis=0).astype(BF16)
    row = lambda t: t.reshape(1, -1)
    consts = [row(lw["mu"]), row(lw["w0"]), w2p, row(lw["a0"]), a2p, lw["g2"].astype(BF16),
              row(lw["k_k"]), row(lw["k_a"]), row(lw["r_k"]), row(lw["lnx_g"]), row(lw["lnx_b"]),
              jnp.asarray(e, BF16)]
    sp = s0.reshape(B, RWKV_HEADS // 2, 2, HEAD_DIM, HEAD_DIM)
    z = jnp.zeros_like(sp[:, :, 0])
    s0 = jnp.concatenate([jnp.concatenate([sp[:, :, 0], z], axis=-1),
                          jnp.concatenate([z, sp[:, :, 1]], axis=-1)], axis=-2)
    st_shape = s0.shape[1:]
    const = lambda arr: pl.BlockSpec(arr.shape, lambda b, c: (0, 0))
    st_spec = pl.BlockSpec((None,) + st_shape, lambda b, c: (b, 0, 0, 0))
    sh_spec = pl.BlockSpec((None, 1, RWKV_IN), lambda b, c: (b, 0, 0))
    n_c = T // rows
    side_args, side_specs, side_out_spec, side_out_shape = _sample_attn_operands(
        *sample_attn_inputs, B * n_c, lambda b, c: b * n_c + c)
    rw, s_out, sh_out, att_side = pl.pallas_call(
        functools.partial(_rwkv_pair_kernel, n_double=n_double, n_side_in=len(side_args)),
        grid=(B, n_c),
        in_specs=[pl.BlockSpec((None, rows, RWKV_IN), lambda b, c: (b, c, 0)), sh_spec, st_spec]
                 + [const(t) for t in consts] + side_specs,
        out_specs=[pl.BlockSpec((None, rows, W), lambda b, c: (b, c, 0)), st_spec, sh_spec, side_out_spec],
        out_shape=[jax.ShapeDtypeStruct((B, T, W), BF16),
                   jax.ShapeDtypeStruct((B,) + st_shape, F32),
                   jax.ShapeDtypeStruct((B, 1, RWKV_IN), F32), side_out_shape],
        scratch_shapes=[pltpu.VMEM(st_shape, F32),
                        pltpu.VMEM((1, RWKV_IN), F32)],
        compiler_params=_params("arbitrary", "arbitrary"),
        name=f"rwkv_c{C}",
    )(P, shift0.reshape(B, 1, RWKV_IN), s0, *consts, *side_args)
    s_out = jnp.stack([s_out[:, :, :HEAD_DIM, :HEAD_DIM], s_out[:, :, HEAD_DIM:, HEAD_DIM:]], axis=2)
    s_out = s_out.reshape(B, RWKV_HEADS, HEAD_DIM, HEAD_DIM)
    return rw, s_out, sh_out.reshape(B, RWKV_IN), att_side


def _rwkv_scan_kernel(r_ref, k_ref, v_ref, lo_ref, sr_ref, sk_ref, sv_ref, slo_ref, s_ref,
                      mur_ref, muk_ref, muv_ref, mulo_ref, w0_ref, w2t_ref, a0_ref, a2t_ref, g2t_ref,
                      kk_ref, ka_ref, rk_ref, lg_ref, lb_ref,
                      rw_ref, sout_ref,
                      w_scr, nkk_scr, bb_scr, km_scr, rr_scr, vv_scr, bonus_scr, g_scr, o_scr):
    T = r_ref.shape[0]

    def shifted(ref, s0_ref, mu_ref, t):
        cur = ref[t]
        prev = s0_ref[...] if t == 0 else ref[t - 1]
        return cur + (prev - cur) * mu_ref[...]

    for t in range(T):
        r = shifted(r_ref, sr_ref, mur_ref, t)
        k = shifted(k_ref, sk_ref, muk_ref, t)
        v = shifted(v_ref, sv_ref, muv_ref, t)
        lo = shifted(lo_ref, slo_ref, mulo_ref, t)
        wl, al, gl = lo[0:DECAY_LORA], lo[DECAY_LORA:DECAY_LORA + AAA_LORA], lo[DECAY_LORA + AAA_LORA:]
        neg = -(w0_ref[...] + _dot(w2t_ref[...], jnp.tanh(wl).astype(BF16)))
        softplus = jnp.maximum(neg, 0.0) + jnp.log(1.0 + jnp.exp(-jnp.abs(neg)))
        w = jnp.exp(-jnp.exp(-softplus - 0.5))
        a = jax.nn.sigmoid(a0_ref[...] + _dot(a2t_ref[...], al.astype(BF16)))
        g_scr[t] = _dot(g2t_ref[...], jax.nn.sigmoid(gl).astype(BF16))
        kkr = k * kk_ref[...]
        kk = kkr / jnp.maximum(jnp.sqrt(jnp.sum(kkr * kkr, axis=0, keepdims=True)), 1e-12)
        kmod = k * (1.0 + (a - 1.0) * ka_ref[...])
        bonus_scr[t] = jnp.sum(r * kmod * rk_ref[...], axis=0, keepdims=True) * v
        w_scr[t] = w
        nkk_scr[t] = -kk
        bb_scr[t] = kk * a
        km_scr[t] = kmod
        rr_scr[t] = r
        vv_scr[t] = v

    def row(i, carry):
        s = s_ref[i]
        for t in range(T):
            sa = jnp.sum(s * nkk_scr[t], axis=0, keepdims=True)
            s = s * w_scr[t] + sa * bb_scr[t] + vv_scr[t, pl.ds(i, 1), :] * km_scr[t]
            o_scr[t, pl.ds(i, 1), :] = jnp.sum(s * rr_scr[t], axis=0, keepdims=True)
        sout_ref[i] = s
        return carry

    lax.fori_loop(0, HEAD_DIM, row, 0, unroll=4)
    o = o_scr[...]
    mean = jnp.mean(o, axis=1, keepdims=True)
    d = o - mean
    var = jnp.mean(d * d, axis=1, keepdims=True)
    on = d * lax.rsqrt(var + LNX_EPS) * lg_ref[...] + lb_ref[...]
    rw_ref[...] = ((on + bonus_scr[...]) * g_scr[...]).astype(rw_ref.dtype)


def _rwkv_scan(pt, shift0, s0t, lw):
    T, _, DB = pt.shape
    W, HD = RWKV_W, HEAD_DIM
    lora_w = DECAY_LORA + AAA_LORA + GATE_LORA
    assert (3 * W) % lora_w == 0
    col = lambda t: jnp.broadcast_to(t.reshape(-1, 1), (t.size, DB))
    shift_t = shift0.T
    mu_c = col(lw["mu"])
    head = lambda base: (lambda h: (base + h, 0))
    pt_blk = lambda base: pl.BlockSpec((T, HD, DB), lambda h: (0, base + h, 0))
    blk2 = lambda base: pl.BlockSpec((HD, DB), head(base))
    lo_pt = pl.BlockSpec((T, lora_w, DB), lambda h: (0, 3 * W // lora_w, 0))
    lo_2d = pl.BlockSpec((lora_w, DB), lambda h: (3 * W // lora_w, 0))
    nh = W // HD
    st_spec = pl.BlockSpec((None, HD, HD, DB), lambda h: (h, 0, 0, 0))
    wt_spec = lambda n: pl.BlockSpec((HD, n), lambda h: (h, 0))
    scr = lambda: pltpu.VMEM((T, HD, DB), F32)
    return pl.pallas_call(
        _rwkv_scan_kernel,
        grid=(nh,),
        in_specs=[pt_blk(0), pt_blk(nh), pt_blk(2 * nh), lo_pt,
                  blk2(0), blk2(nh), blk2(2 * nh), lo_2d, st_spec,
                  blk2(0), blk2(nh), blk2(2 * nh), lo_2d,
                  blk2(0), wt_spec(DECAY_LORA), blk2(0), wt_spec(AAA_LORA), wt_spec(GATE_LORA),
                  blk2(0), blk2(0), blk2(0), blk2(0), blk2(0)],
        out_specs=[pl.BlockSpec((T, HD, DB), lambda h: (0, h, 0)), st_spec],
        out_shape=[jax.ShapeDtypeStruct((T, W, DB), F32), jax.ShapeDtypeStruct(s0t.shape, F32)],
        scratch_shapes=[scr() for _ in range(9)],
        compiler_params=_params("arbitrary"),
        name="rwkv_scan",
    )(pt, pt, pt, pt, shift_t, shift_t, shift_t, shift_t, s0t,
      mu_c, mu_c, mu_c, mu_c, col(lw["w0"]), lw["w2"].T.astype(BF16), col(lw["a0"]), lw["a2"].T.astype(BF16),
      lw["g2"].T.astype(BF16), col(lw["k_k"]), col(lw["k_a"]), col(lw["r_k"]), col(lw["lnx_g"]), col(lw["lnx_b"]))


def _mlp_kernel(att_ref, rw_ref, x_ref, gt1_ref, sc2_ref, sh2_ref, gt2_ref, g2_ref, gf_ref,
                wo_ref, w1_ref, w2_ref, y_ref, *, final_norm, ff_chunk):
    half = wo_ref.shape[0] // 2
    mix = _dot(att_ref[...], wo_ref[0:half, :]) + _dot(rw_ref[...], wo_ref[half:, :])
    x1 = x_ref[...] + gt1_ref[...] * mix
    ms = jnp.mean(x1 * x1, axis=-1, keepdims=True)
    h2 = x1 * lax.rsqrt(ms + NORM_EPS) * g2_ref[...]
    h2 = (h2 * (1.0 + sc2_ref[...]) + sh2_ref[...]).astype(BF16)
    acc = jnp.zeros_like(x1)
    for c in range(w1_ref.shape[1] // ff_chunk):
        hid = _dot(h2, w1_ref[:, c * ff_chunk:(c + 1) * ff_chunk])
        hid = jnp.square(jnp.maximum(hid, 0.0)).astype(BF16)
        acc = acc + _dot(hid, w2_ref[c * ff_chunk:(c + 1) * ff_chunk, :])
    x2 = x1 + gt2_ref[...] * acc
    if final_norm:
        ms2 = jnp.mean(x2 * x2, axis=-1, keepdims=True)
        x2 = x2 * lax.rsqrt(ms2 + NORM_EPS) * gf_ref[...]
    y_ref[...] = x2


def _mlp(att, rw, x, gt1, sc2, sh2, gt2, g2, gf, wo_b, w1_b, w2_b, tm, per_row_mod, final_norm):
    G, R, D = x.shape
    dff = w1_b.shape[1]
    row_spec = lambda w: pl.BlockSpec((None, tm, w), lambda b, i: (b, i, 0))
    if per_row_mod:
        mod_spec = pl.BlockSpec((None, tm, D), lambda b, i: (b, i, 0))
    else:
        mod_spec = pl.BlockSpec((None, 1, D), lambda b, i: (b, 0, 0))
    const = lambda shape: pl.BlockSpec(shape, lambda b, i: (0, 0), pipeline_mode=pl.Buffered(1))
    return pl.pallas_call(
        functools.partial(_mlp_kernel, final_norm=final_norm, ff_chunk=1024),
        grid=(G, R // tm),
        in_specs=[row_spec(att.shape[-1]), row_spec(rw.shape[-1]), row_spec(D),
                  mod_spec, mod_spec, mod_spec, mod_spec, const((1, D)), const((1, D)),
                  const(wo_b.shape), const((D, dff)), const((dff, D))],
        out_specs=row_spec(D),
        out_shape=jax.ShapeDtypeStruct((G, R, D), F32),
        compiler_params=_params("arbitrary", "arbitrary"),
        name="mlp",
    )(att, rw, x, gt1, sc2, sh2, gt2, g2.reshape(1, D), gf.reshape(1, D), wo_b, w1_b, w2_b)


def kernel(x_prompt, x_sample, cache_k, cache_v, state_wkv, state_shift, c_prompt, c_sample, norm1_g, norm2_g, w_ada, b_ada, w_in, w_out, mu, w0, w2, a0, a2, g2, k_k, k_a, r_k, lnx_g, lnx_b, w_ff1, w_ff2, normf_g):
    B, S, D = x_prompt.shape
    DB, T, _ = x_sample.shape
    depth = w_in.shape[0]
    dt = x_prompt.dtype
    tabs_p = _rope_tables(jnp.arange(S))
    tabs_t = tuple(jnp.repeat(t, DB, axis=0) for t in _rope_tables(PAST_LEN + jnp.arange(T)))
    c_all = jnp.concatenate([c_prompt, c_sample], axis=0)
    hp = x_prompt
    hs = x_sample.reshape(1, DB * T, D)
    outs = [[] for _ in range(8)]
    tm_s = min(ROW_TILE, DB * T)
    for l in range(depth):
        last = l == depth - 1
        mod = _ada(c_all, w_ada[l], b_ada[l])
        mod_p = [m.reshape(B, 1, D) for m in jnp.split(mod[:B], 6, axis=-1)]
        mod_s = [jnp.repeat(m, T, axis=0).reshape(1, DB * T, D) for m in jnp.split(mod[B:], 6, axis=-1)]
        mod_t = [m.reshape(1, DB, D) for m in jnp.split(mod[B:], 6, axis=-1)]
        w_in_b, wo_b = w_in[l].astype(BF16), w_out[l].astype(BF16)
        w1_b, w2_b = w_ff1[l].astype(BF16), w_ff2[l].astype(BF16)
        lw = dict(mu=mu[l], w0=w0[l], w2=w2[l], a0=a0[l], a2=a2[l], g2=g2[l], k_k=k_k[l], k_a=k_a[l],
                  r_k=r_k[l], lnx_g=lnx_g[l], lnx_b=lnx_b[l])

        sh1_p, sc1_p, gt1_p, sh2_p, sc2_p, gt2_p = mod_p
        _, _, gt1_s, sh2_s, sc2_s, gt2_s = mod_s
        q, k, v, P, kt, vt = _inproj(hp, sc1_p, sh1_p, norm1_g[l], w_in_b, tabs_p, ROW_TILE, "batch", True)
        hs_t = jnp.transpose(hs.reshape(DB, T, D), (1, 0, 2)).reshape(1, T * DB, D)
        q_t, k_t, v_t, P_t, kt_s, vt_s, pt_s = _inproj(hs_t, mod_t[1], mod_t[0], norm1_g[l], w_in_b, tabs_t, DB,
                                                       "tile", True, nq=T, p_t=True)
        by_request = lambda t: jnp.transpose(t.reshape(T, DB, ATT_W), (1, 0, 2))

        att = _attn_fused(q, k, v)
        sample_attn_inputs = (by_request(q_t), by_request(k_t), by_request(v_t),
                              jnp.transpose(cache_k[l], (0, 2, 3, 1)), jnp.transpose(cache_v[l], (0, 2, 3, 1)))
        rw, s_p, shift_p, att_s = _rwkv(P, jnp.zeros((B, RWKV_IN), F32),
                                        jnp.zeros((B, RWKV_HEADS, HEAD_DIM, HEAD_DIM), F32), lw, RWKV_CHUNK,
                                        sample_attn_inputs)
        hp = _mlp(att, rw, hp, gt1_p, sc2_p, sh2_p, gt2_p, norm2_g[l], normf_g, wo_b, w1_b, w2_b,
                  ROW_TILE, False, last)
        keep = min(max(w for w, _ in DIL_BRANCHES), S)
        win = lambda t: jnp.transpose(t.reshape(B, ATT_HEADS, HEAD_DIM, S)[..., S - keep:], (0, 3, 1, 2))
        outs[0].append(win(kt))
        outs[1].append(win(vt))
        outs[2].append(s_p)
        outs[3].append(shift_p)

        rw_t, s_t = _rwkv_scan(pt_s.reshape(T, RWKV_IN, DB), state_shift[l],
                               jnp.transpose(state_wkv[l], (1, 2, 3, 0)), lw)
        rw_s = jnp.transpose(rw_t, (2, 0, 1)).astype(BF16)
        hs = _mlp(att_s.reshape(1, DB * T, ATT_W), rw_s.reshape(1, DB * T, RWKV_W), hs, gt1_s, sc2_s, sh2_s, gt2_s,
                  norm2_g[l], normf_g, wo_b, w1_b, w2_b, tm_s, True, last)
        new = lambda t: jnp.transpose(t.reshape(T, ATT_HEADS, HEAD_DIM, DB), (3, 0, 1, 2))
        outs[4].append(new(kt_s))
        outs[5].append(new(vt_s))
        outs[6].append(jnp.transpose(s_t, (3, 0, 1, 2)))
        outs[7].append(P_t.reshape(T, DB, RWKV_IN)[T - 1])
    stack = lambda xs: jnp.stack(xs).astype(dt)
    return (hp.astype(dt), hs.reshape(DB, T, D).astype(x_sample.dtype), *[stack(o) for o in outs])
```

```python
import functools

import numpy as np
import jax
import jax.numpy as jnp
from jax import lax
from jax.experimental import pallas as pl
from jax.experimental.pallas import tpu as pltpu

F32 = jnp.float32
BF16 = jnp.bfloat16

HEAD_DIM = 64
ATT_HEADS = 8
ATT_W = ATT_HEADS * HEAD_DIM
RWKV_HEADS = 8
RWKV_W = RWKV_HEADS * HEAD_DIM
DIL_BRANCHES = ((128, 1), (512, 4), (2048, 16))
ROT_DIM = HEAD_DIM // 4
ROPE_THETA = 500000.0
DECAY_LORA = 64
AAA_LORA = 64
GATE_LORA = 128
RWKV_IN = 3 * RWKV_W + DECAY_LORA + AAA_LORA + GATE_LORA
NORM_EPS = 1e-6
LNX_EPS = 64e-5
PAST_LEN = 8192
ATT_BLOCK = 128
NEG_BIG = -1e30
LOG2_E = 1.4426950408889634
VMEM_LIMIT = 56 * 1024 * 1024
ROW_TILE = 512
RWKV_CHUNK = 64
RWKV_CHUNKS_PER_STEP = 4
ATT_GROUP = 8
ADA_TILE = 1024


def _dot(a, b):
    return jnp.dot(a, b, preferred_element_type=F32)


def _dot_nt(a, b):
    return lax.dot_general(a, b, (((1,), (1,)), ((), ())), preferred_element_type=F32)


def _dot_tn(a, b):
    return lax.dot_general(a, b, (((0,), (0,)), ((), ())), preferred_element_type=F32)


def _split2(x):
    hi = x.astype(BF16)
    lo = (x - hi.astype(F32)).astype(BF16)
    return hi, lo


def _head_sums(x, e):
    C, W = x.shape
    n = W // e.shape[0]
    xb = x.astype(BF16)
    o = _dot(jnp.concatenate([xb[:, e.shape[0] * p:e.shape[0] * (p + 1)] for p in range(n)], axis=0), e)
    return jnp.concatenate([o[C * p:C * (p + 1)] for p in range(n)], axis=1)


def _params(*sem):
    return pltpu.CompilerParams(dimension_semantics=sem, vmem_limit_bytes=VMEM_LIMIT)


def _ada_kernel(c_ref, w_ref, b_ref, o_ref):
    c = c_ref[...]
    s = c * jax.nn.sigmoid(c)
    o_ref[...] = _dot(s.astype(BF16), w_ref[...].astype(BF16)) + b_ref[...]


def _ada(c, w_ada, b_ada):
    n, d = c.shape
    cols = w_ada.shape[1]
    tn = ADA_TILE
    assert cols % tn == 0
    return pl.pallas_call(
        _ada_kernel,
        grid=(cols // tn,),
        in_specs=[pl.BlockSpec((n, d), lambda j: (0, 0)),
                  pl.BlockSpec((d, tn), lambda j: (0, j)),
                  pl.BlockSpec((1, tn), lambda j: (0, j))],
        out_specs=pl.BlockSpec((n, tn), lambda j: (0, j)),
        out_shape=jax.ShapeDtypeStruct((n, cols), F32),
        compiler_params=_params("arbitrary"),
        name="ada",
    )(c, w_ada, b_ada.reshape(1, cols))


def _rope(t, cos, s1, s2):
    parts = []
    for c in range(ATT_W // 128):
        tc = t[:, 128 * c:128 * (c + 1)]
        parts.append(tc * cos + pltpu.roll(tc, 128 - ROT_DIM // 2, 1) * s1 + pltpu.roll(tc, ROT_DIM // 2, 1) * s2)
    return jnp.concatenate(parts, axis=1)


def _inproj_kernel(x_ref, sc_ref, sh_ref, g_ref, w_ref, cos_ref, s1_ref, s2_ref, *out_refs,
                   transposed, p_t):
    q_ref, k_ref, v_ref, p_ref = out_refs[:4]
    rest = list(out_refs[4:])
    x = x_ref[...]
    ms = jnp.mean(x * x, axis=-1, keepdims=True)
    h = x * lax.rsqrt(ms + NORM_EPS) * g_ref[...]
    h = (h * (1.0 + sc_ref[...]) + sh_ref[...]).astype(BF16)
    cos, s1, s2 = cos_ref[...], s1_ref[...], s2_ref[...]
    q = _rope(_dot(h, w_ref[:, 0:ATT_W]), cos, s1, s2)
    q_ref[...] = (q * HEAD_DIM ** -0.5).astype(q_ref.dtype)
    k = _rope(_dot(h, w_ref[:, ATT_W:2 * ATT_W]), cos, s1, s2)
    v = _dot(h, w_ref[:, 2 * ATT_W:3 * ATT_W])
    k_ref[...] = k
    v_ref[...] = v
    p = _dot(h, w_ref[:, 3 * ATT_W:])
    p_ref[...] = p
    if transposed:
        rest.pop(0)[...] = k.T
        rest.pop(0)[...] = v.T
    if p_t:
        rest.pop(0)[...] = p.T


def _inproj(x, sc, sh, g, w_in_b, tabs, tm, mod_mode, transposed, nq=1, p_t=False):
    G, R, D = x.shape
    nin = w_in_b.shape[1]
    row_spec = lambda w: pl.BlockSpec((None, tm, w), lambda b, i: (b, i, 0))
    mod_spec = {"batch": pl.BlockSpec((None, 1, D), lambda b, i: (b, 0, 0)),
                "tile": pl.BlockSpec((None, tm, D), lambda b, i: (b, 0, 0))}[mod_mode]
    tab_spec = pl.BlockSpec((tm, 128), lambda b, i: (i, 0))
    const = lambda shape: pl.BlockSpec(shape, lambda b, i: (0, 0))
    out_specs = [row_spec(ATT_W)] * 3 + [row_spec(RWKV_IN)]
    out_shape = [jax.ShapeDtypeStruct((G, R, ATT_W), F32)] * 3 + [jax.ShapeDtypeStruct((G, R, RWKV_IN), F32)]
    if transposed:
        spq = (R // nq) // tm
        t_spec = pl.BlockSpec((None, None, ATT_W, tm), lambda b, i: (b, i // spq, 0, i % spq))
        out_specs += [t_spec] * 2
        out_shape += [jax.ShapeDtypeStruct((G, nq, ATT_W, R // nq), F32)] * 2
        if p_t:
            out_specs.append(pl.BlockSpec((None, None, RWKV_IN, tm), lambda b, i: (b, i // spq, 0, i % spq)))
            out_shape.append(jax.ShapeDtypeStruct((G, nq, RWKV_IN, R // nq), F32))
    return pl.pallas_call(
        functools.partial(_inproj_kernel, transposed=transposed, p_t=p_t),
        grid=(G, R // tm),
        in_specs=[row_spec(D), mod_spec, mod_spec, const((1, D)), const((D, nin)),
                  tab_spec, tab_spec, tab_spec],
        out_specs=out_specs,
        out_shape=out_shape,
        compiler_params=_params("arbitrary", "arbitrary"),
        name="inproj",
    )(x, sc, sh, g.reshape(1, D), w_in_b, *tabs)


def _rope_tables(pos):
    half = ROT_DIM // 2
    inv = ROPE_THETA ** (-jnp.arange(half, dtype=F32) * (2.0 / ROT_DIM))
    ang = pos.astype(F32)[:, None] * inv[None, :]
    cos, sin = jnp.cos(ang), jnp.sin(ang)
    n = pos.shape[0]
    ones = jnp.ones((n, HEAD_DIM - ROT_DIM), F32)
    zeros = jnp.zeros((n, HEAD_DIM - ROT_DIM), F32)
    zh = jnp.zeros((n, half), F32)
    ctab = jnp.concatenate([cos, cos, ones], axis=1)
    s1 = jnp.concatenate([-sin, zh, zeros], axis=1)
    s2 = jnp.concatenate([zh, sin, zeros], axis=1)
    return tuple(jnp.tile(t, (1, 128 // HEAD_DIM)) for t in (ctab, s1, s2))


def _attn_fused_kernel(q_ref, k_ref, v_ref, o_ref, acc_scr, m_scr, l_scr, *, dilations, group):
    nw, HD = ATT_BLOCK, HEAD_DIM
    S = q_ref.shape[0]
    n_tiles = S // nw
    lane = lax.broadcasted_iota(jnp.int32, (nw, 2 * HD), 1)
    first_head = lane < HD
    q_idx = lax.broadcasted_iota(jnp.int32, (nw, 2 * HD), 0)
    n_col = nw // HD
    key_idx = [HD * jc + (lane & (HD - 1)) for jc in range(n_col)]
    cur_ok = [q_idx >= key_idx[jc] for jc in range(n_col)]

    def bd(x):
        zero = jnp.zeros_like(x)
        xa, xb = jnp.where(first_head, x, zero), jnp.where(first_head, zero, x)
        parts = []
        for jc in range(n_col):
            parts += [xa[HD * jc:HD * (jc + 1)], xb[HD * jc:HD * (jc + 1)]]
        return jnp.concatenate(parts, axis=0)

    bd_row = lax.broadcasted_iota(jnp.int32, (2 * nw, 2 * HD), 0)
    bd_lane = lax.broadcasted_iota(jnp.int32, (2 * nw, 2 * HD), 1)
    ones_bd = jnp.where(((bd_row // HD) % 2 == 0) == (bd_lane < HD), 1.0, 0.0).astype(BF16)

    def with_ones(v):
        return jnp.concatenate([bd(v.astype(BF16)), ones_bd], axis=1)

    def half_reduce(x, op, fill):
        ra = op(jnp.where(first_head, x, fill), axis=-1, keepdims=True)
        rb = op(jnp.where(first_head, fill, x), axis=-1, keepdims=True)
        return jnp.where(first_head, ra, rb)

    def steps(tiles, use_prev, init, final):
        n = range(len(tiles))
        qs = [(q_ref[tiles[t][0], :] * LOG2_E).astype(BF16) for t in n]
        cols, vbd = [], []
        for t in n:
            rows, prev_rows, has_prev = tiles[t]
            s = _dot_nt(qs[t], bd(k_ref[rows, :].astype(BF16)))
            c = [jnp.where(cur_ok[jc], s[:, 2 * HD * jc:2 * HD * (jc + 1)], NEG_BIG) for jc in range(n_col)]
            vb = [with_ones(v_ref[rows, :])]
            if use_prev:
                sp = _dot_nt(qs[t], bd(k_ref[prev_rows, :].astype(BF16)))
                thresh = jnp.where(has_prev, 0, nw)
                c += [jnp.where((key_idx[jc] - q_idx) >= thresh, sp[:, 2 * HD * jc:2 * HD * (jc + 1)], NEG_BIG)
                      for jc in range(n_col)]
                vb.append(with_ones(v_ref[prev_rows, :]))
            cols.append(c)
            vbd.append(vb)
        m_new, m_old, p = [], [], []
        for t in n:
            m = half_reduce(functools.reduce(jnp.maximum, cols[t]), jnp.max, NEG_BIG)
            if not init:
                m_old.append(m_scr[tiles[t][0], :])
                m = jnp.maximum(m, m_old[t])
            m_new.append(m)
            p.append([jnp.exp2(c - m) for c in cols[t]])
        acc = []
        for t in n:
            a = _dot(jnp.concatenate(p[t][0:n_col], axis=1).astype(BF16), vbd[t][0])
            if use_prev:
                a = a + _dot(jnp.concatenate(p[t][n_col:], axis=1).astype(BF16), vbd[t][1])
            acc.append(a)
        for t in n:
            rows = tiles[t][0]
            a, l = acc[t][:, 0:2 * HD], acc[t][:, 2 * HD:]
            if not init:
                alpha = jnp.exp2(m_old[t] - m_new[t])
                a = a + acc_scr[rows, :] * alpha
                l = l + l_scr[rows, :] * alpha
            if final:
                o_ref[rows, :] = (a / l).astype(o_ref.dtype)
            else:
                acc_scr[rows, :] = a
                m_scr[rows, :] = m_new[t]
                l_scr[rows, :] = l

    n_br = len(dilations)
    for bi, dil in enumerate(dilations):
        init, final = bi == 0, bi == n_br - 1
        use_prev = S // (dil * nw) > 1

        def group_body(g, carry, dil=dil, use_prev=use_prev, init=init, final=final):
            tiles = []
            for i in range(group):
                t = g * group + i
                r, j = lax.rem(t, dil), lax.div(t, dil)
                start = r + j * (dil * nw)
                prev = r + jnp.maximum(j - 1, 0) * (dil * nw)
                if dil > 1:
                    tiles.append((pl.ds(start, nw, stride=dil), pl.ds(prev, nw, stride=dil), j > 0))
                else:
                    tiles.append((pl.ds(pl.multiple_of(start, nw), nw), pl.ds(pl.multiple_of(prev, nw), nw), j > 0))
            steps(tiles, use_prev, init, final)
            return carry

        lax.fori_loop(0, n_tiles // group, group_body, 0, unroll=True)


def _attn_fused(q, k, v):
    B, S, W = q.shape
    nw = ATT_BLOCK
    dilations = tuple(sorted((d for _, d in DIL_BRANCHES), reverse=True))
    for win, d in DIL_BRANCHES:
        assert win == d * nw and S % (d * nw) == 0
    assert (S // nw) % ATT_GROUP == 0
    PW = 2 * HEAD_DIM
    spec = pl.BlockSpec((None, S, PW), lambda b, p: (b, 0, p))
    return pl.pallas_call(
        functools.partial(_attn_fused_kernel, dilations=dilations, group=ATT_GROUP),
        grid=(B, W // PW),
        in_specs=[spec, spec, spec],
        out_specs=spec,
        out_shape=jax.ShapeDtypeStruct((B, S, W), BF16),
        scratch_shapes=[pltpu.VMEM((S, PW), F32) for _ in range(3)],
        compiler_params=_params("arbitrary", "arbitrary"),
        name="attn_fused",
    )(q, k, v)


def _sample_counts(T, WB):
    i = np.arange(T)
    c = np.arange(WB)
    delta = WB + i[None, :] - c[:, None]
    cnt_c = np.zeros((WB, T), np.float32)
    diff = i[None, :] - i[:, None]
    cnt_n = np.zeros((T, T), np.float32)
    for win, dil in DIL_BRANCHES:
        cnt_c += ((delta > 0) & (delta % dil == 0) & (delta <= win)).astype(np.float32)
        cnt_n += ((diff >= 0) & (diff % dil == 0) & (diff <= win)).astype(np.float32)
    return cnt_c, cnt_n


def _sample_attn_scores(q_ref, kn_ref, vn_ref, kt_ref, vt_ref, cc_ref, cn_ref):
    cc, cn = cc_ref[...], cn_ref[...]
    nb = q_ref.shape[0]
    items = [(b, h) for b in range(nb) for h in range(ATT_HEADS)]
    sl = lambda h: slice(HEAD_DIM * h, HEAD_DIM * (h + 1))
    qh = {(b, h): q_ref[b, :, sl(h)] for b, h in items}
    s_c = {i: jnp.where(cc > 0, _dot(qh[i].astype(BF16), kt_ref[i[0], i[1]].astype(BF16)), NEG_BIG)
           for i in items}
    s_n = {i: jnp.where(cn > 0, _dot_nt(qh[i], kn_ref[i[0], :, sl(i[1])]), NEG_BIG) for i in items}
    return s_c, s_n


def _sample_attn_finish(scores, q_ref, kn_ref, vn_ref, kt_ref, vt_ref, cc_ref, cn_ref, o_ref):
    s_c, s_n = scores
    cc, cn = cc_ref[...], cn_ref[...]
    nb = q_ref.shape[0]
    items = list(s_c)
    sl = lambda h: slice(HEAD_DIM * h, HEAD_DIM * (h + 1))
    p_c, p_n, l = {}, {}, {}
    for i in items:
        m = jnp.maximum(jnp.max(s_c[i], axis=-1, keepdims=True), jnp.max(s_n[i], axis=-1, keepdims=True))
        p_c[i] = cc * jnp.exp(s_c[i] - m)
        p_n[i] = cn * jnp.exp(s_n[i] - m)
        l[i] = jnp.sum(p_c[i], axis=-1, keepdims=True) + jnp.sum(p_n[i], axis=-1, keepdims=True)
    acc = {i: _dot_nt(p_c[i].astype(BF16), vt_ref[i[0], i[1]].astype(BF16)) + _dot(p_n[i], vn_ref[i[0], :, sl(i[1])])
           for i in items}
    for b in range(nb):
        o_ref[b] = jnp.concatenate([acc[(b, h)] / l[(b, h)] for h in range(ATT_HEADS)], axis=1).astype(o_ref.dtype)


def _sample_attn_operands(q, k_new, v_new, cache_kt, cache_vt, n_steps, step_index):
    DB, T, W = q.shape
    WB = cache_kt.shape[-1]
    assert DB % n_steps == 0, "the requests must split evenly over the host call's grid steps"
    nb = DB // n_steps
    cnt_c, cnt_n = _sample_counts(T, WB)
    cc = jnp.asarray(cnt_c.T)
    cn = jnp.asarray(cnt_n.T)
    new_spec = pl.BlockSpec((nb, T, W), lambda *g: (step_index(*g), 0, 0))
    cache_spec = pl.BlockSpec((nb, ATT_HEADS, HEAD_DIM, WB), lambda *g: (step_index(*g), 0, 0, 0))
    const = lambda arr: pl.BlockSpec(arr.shape, lambda *g: (0, 0))
    args = (q, k_new, v_new, cache_kt, cache_vt, cc, cn)
    in_specs = [new_spec, new_spec, new_spec, cache_spec, cache_spec, const(cc), const(cn)]
    return args, in_specs, new_spec, jax.ShapeDtypeStruct((DB, T, W), BF16)


def _rwkv_prologue(p_ref, shift_ref, prev_scr, shout_ref, mu_ref, w0_ref, w2_ref, a0_ref, a2_ref, g2_ref,
                   kk_ref, ka_ref, rk_ref, e, chunk):
    C = p_ref.shape[0]
    P = p_ref[...]
    row = lax.broadcasted_iota(jnp.int32, (C, 1), 0)
    p_prev = jnp.where(row == 0, prev_scr[...], pltpu.roll(P, 1, 0))
    pm = P + (p_prev - P) * mu_ref[...]
    prev_scr[...] = P[C - 1:C, :]
    shout_ref[...] = P[C - 1:C, :]

    W = RWKV_W
    r, k, v = pm[:, 0:W], pm[:, W:2 * W], pm[:, 2 * W:3 * W]
    lora_wa = pm[:, 3 * W:3 * W + DECAY_LORA + AAA_LORA]
    gl = pm[:, 3 * W + DECAY_LORA + AAA_LORA:]
    wpre = w0_ref[...] + _dot(jnp.tanh(lora_wa).astype(BF16), w2_ref[...])
    neg = -wpre
    softplus = jnp.maximum(neg, 0.0) + jnp.log(1.0 + jnp.exp(-jnp.abs(neg)))
    logw = -jnp.exp(-softplus - 0.5)
    a = jax.nn.sigmoid(a0_ref[...] + _dot(lora_wa.astype(BF16), a2_ref[...]))
    g = _dot(jax.nn.sigmoid(gl).astype(BF16), g2_ref[...])
    kkr = k * kk_ref[...]
    nrm = jnp.sqrt(_head_sums(kkr * kkr, e))
    kk = kkr / jnp.maximum(nrm, 1e-12)
    kmod = k * (1.0 + (a - 1.0) * ka_ref[...])
    bonus = _head_sums(r * kmod * rk_ref[...], e) * v

    ti = lax.broadcasted_iota(jnp.int32, (C, C), 0)
    si = lax.broadcasted_iota(jnp.int32, (C, C), 1)
    tri_b = jnp.where((ti >= si) & (lax.div(ti, chunk) == lax.div(si, chunk)), 1.0, 0.0).astype(BF16)
    lw_h, lw_l = _split2(logw)
    L = _dot(tri_b, lw_h) + _dot(tri_b, lw_l)
    e_l = jnp.exp(L)
    e_nl = jnp.exp(-L)
    a_t = -kk * jnp.exp(L - logw)
    r_t = r * e_l
    b_t = kk * a * e_nl
    k_t = kmod * e_nl
    return v, a_t, r_t, b_t, k_t, e_l, bonus, g


def _rwkv_epilogue(o, bonus, g, e, lg_ref, lb_ref, rw_ref):
    mean = _head_sums(o, e) * (1.0 / HEAD_DIM)
    d = o - mean
    var = _head_sums(d * d, e) * (1.0 / HEAD_DIM)
    on = d * lax.rsqrt(var + LNX_EPS) * lg_ref[...] + lb_ref[...]
    rw_ref[...] = ((on + bonus) * g).astype(rw_ref.dtype)


def _rwkv_pair_kernel(*refs, n_double, n_side_in):
    (p_ref, shift_ref, s0_ref, mu_ref, w0_ref, w2_ref, a0_ref, a2_ref, g2_ref,
     kk_ref, ka_ref, rk_ref, lg_ref, lb_ref, e_ref) = refs[:15]
    side_in = refs[15:15 + n_side_in]
    rw_ref, sout_ref, shout_ref, side_out, s_scr, prev_scr = refs[15 + n_side_in:]

    C = HEAD_DIM
    n_sub = p_ref.shape[0] // C
    n_pairs = RWKV_HEADS // 2
    PW = 2 * HEAD_DIM

    @pl.when(pl.program_id(1) == 0)
    def _():
        s_scr[...] = s0_ref[...]
        prev_scr[...] = shift_ref[...]

    side_scores = _sample_attn_scores(*side_in)
    e = e_ref[...]
    v, a_t, r_t, b_t, k_t, e_l, bonus, g = _rwkv_prologue(
        p_ref, shift_ref, prev_scr, shout_ref, mu_ref, w0_ref, w2_ref, a0_ref, a2_ref, g2_ref,
        kk_ref, ka_ref, rk_ref, e, C)
    _sample_attn_finish(side_scores, *side_in, side_out)

    lane = lax.broadcasted_iota(jnp.int32, (C, PW), 1)
    ti = lax.broadcasted_iota(jnp.int32, (C, PW), 0)
    si = lane & (HEAD_DIM - 1)
    first_head = lane < HEAD_DIM
    tri_incl = ti >= si
    tri_strict = ti > si
    eye = jnp.where(ti == si, 1.0, 0.0)
    rows2 = lax.broadcasted_iota(jnp.int32, (2 * C, PW), 0)
    lanes2 = lax.broadcasted_iota(jnp.int32, (2 * C, PW), 1)
    same_head = (rows2 < C) == (lanes2 < HEAD_DIM)

    def bd(x):
        zero = jnp.zeros_like(x)
        return jnp.concatenate([jnp.where(first_head, x, zero), jnp.where(first_head, zero, x)], axis=0)

    def dot_bd(x, y):
        return _dot(x.astype(BF16), bd(y.astype(BF16)))

    items = [(c, p) for c in range(n_sub) for p in range(n_pairs)]
    tile = lambda t, c, p: t[C * c:C * (c + 1), PW * p:PW * (p + 1)]
    ar = {i: jnp.concatenate([tile(a_t, *i), tile(r_t, *i)], axis=0).astype(BF16) for i in items}
    bk = {i: jnp.concatenate([tile(b_t, *i), tile(k_t, *i)], axis=0).astype(BF16) for i in items}
    gm = {i: _dot_nt(ar[i], jnp.concatenate([bd(bk[i][0:C]), bd(bk[i][C:2 * C])], axis=0)) for i in items}
    m_ab = {i: jnp.where(tri_strict, gm[i][0:C, 0:PW], 0.0) for i in items}
    m_ak = {i: jnp.where(tri_strict, gm[i][0:C, PW:2 * PW], 0.0).astype(BF16) for i in items}
    n_r = {i: jnp.concatenate([jnp.where(tri_incl, gm[i][C:2 * C, 0:PW], 0.0),
                               jnp.where(tri_incl, gm[i][C:2 * C, PW:2 * PW], 0.0)], axis=1).astype(BF16)
           for i in items}
    t_inv = {i: eye + m_ab[i] for i in items}
    mp = m_ab
    for _ in range(n_double):
        mp = {i: dot_bd(mp[i], mp[i]) for i in items}
        t_inv = {i: t_inv[i] + dot_bd(t_inv[i], mp[i]) for i in items}
    v_b = {i: tile(v, *i).astype(BF16) for i in items}
    v_bd = {i: bd(v_b[i]) for i in items}
    mv = {i: _dot(m_ak[i], v_bd[i]) for i in items}
    state = [s_scr[p] for p in range(n_pairs)]
    o_rows = []
    for c in range(n_sub):
        its = [(c, p) for p in range(n_pairs)]
        ars = {i: _dot_nt(ar[i], state[i[1]].astype(BF16)) for i in its}
        u_b = {i: dot_bd(t_inv[i], ars[i][0:C] + mv[i]).astype(BF16) for i in its}
        o = [ars[i][C:2 * C] + _dot(n_r[i], jnp.concatenate([bd(u_b[i]), v_bd[i]], axis=0)) for i in its]
        o_rows.append(jnp.concatenate(o, axis=1))
        for i in its:
            upd = jnp.where(same_head, _dot_tn(jnp.concatenate([u_b[i], v_b[i]], axis=0), bk[i]), 0.0)
            state[i[1]] = (state[i[1]] + upd) * e_l[C * (c + 1) - 1:C * (c + 1), PW * i[1]:PW * (i[1] + 1)]
    for p in range(n_pairs):
        s_scr[p] = state[p]
        sout_ref[p] = state[p]
    _rwkv_epilogue(jnp.concatenate(o_rows, axis=0), bonus, g, e, lg_ref, lb_ref, rw_ref)


def _rwkv(P, shift0, s0, lw, C, sample_attn_inputs):
    B, T, _ = P.shape
    assert C == HEAD_DIM, "the pair-packed chunk kernel needs chunk length == head dim"
    rows = C * RWKV_CHUNKS_PER_STEP
    assert T % rows == 0
    n_double = max(int(np.log2(C)) - 1, 0)
    W = RWKV_W
    e = np.kron(np.eye(2, dtype=np.float32), np.ones((HEAD_DIM, HEAD_DIM), np.float32))
    zpad = jnp.zeros((DECAY_LORA, W), F32)
    w2p = jnp.concatenate([lw["w2"], jnp.zeros((AAA_LORA, W), F32)], axis=0).astype(BF16)
    a2p = jnp.concatenate([zpad, lw["a2"]], axis=0).astype(BF16)
    row = lambda t: t.reshape(1, -1)
    consts = [row(lw["mu"]), row(lw["w0"]), w2p, row(lw["a0"]), a2p, lw["g2"].astype(BF16),
              row(lw["k_k"]), row(lw["k_a"]), row(lw["r_k"]), row(lw["lnx_g"]), row(lw["lnx_b"]),
              jnp.asarray(e, BF16)]
    sp = s0.reshape(B, RWKV_HEADS // 2, 2, HEAD_DIM, HEAD_DIM)
    z = jnp.zeros_like(sp[:, :, 0])
    s0 = jnp.concatenate([jnp.concatenate([sp[:, :, 0], z], axis=-1),
                          jnp.concatenate([z, sp[:, :, 1]], axis=-1)], axis=-2)
    st_shape = s0.shape[1:]
    const = lambda arr: pl.BlockSpec(arr.shape, lambda b, c: (0, 0))
    st_spec = pl.BlockSpec((None,) + st_shape, lambda b, c: (b, 0, 0, 0))
    sh_spec = pl.BlockSpec((None, 1, RWKV_IN), lambda b, c: (b, 0, 0))
    n_c = T // rows
    side_args, side_specs, side_out_spec, side_out_shape = _sample_attn_operands(
        *sample_attn_inputs, B * n_c, lambda b, c: b * n_c + c)
    rw, s_out, sh_out, att_side = pl.pallas_call(
        functools.partial(_rwkv_pair_kernel, n_double=n_double, n_side_in=len(side_args)),
        grid=(B, n_c),
        in_specs=[pl.BlockSpec((None, rows, RWKV_IN), lambda b, c: (b, c, 0)), sh_spec, st_spec]
                 + [const(t) for t in consts] + side_specs,
        out_specs=[pl.BlockSpec((None, rows, W), lambda b, c: (b, c, 0)), st_spec, sh_spec, side_out_spec],
        out_shape=[jax.ShapeDtypeStruct((B, T, W), BF16),
                   jax.ShapeDtypeStruct((B,) + st_shape, F32),
                   jax.ShapeDtypeStruct((B, 1, RWKV_IN), F32), side_out_shape],
        scratch_shapes=[pltpu.VMEM(st_shape, F32),
                        pltpu.VMEM((1, RWKV_IN), F32)],
        compiler_params=_params("arbitrary", "arbitrary"),
        name=f"rwkv_c{C}",
    )(P, shift0.reshape(B, 1, RWKV_IN), s0, *consts, *side_args)
    s_out = jnp.stack([s_out[:, :, :HEAD_DIM, :HEAD_DIM], s_out[:, :, HEAD_DIM:, HEAD_DIM:]], axis=2)
    s_out = s_out.reshape(B, RWKV_HEADS, HEAD_DIM, HEAD_DIM)
    return rw, s_out, sh_out.reshape(B, RWKV_IN), att_side


def _rwkv_scan_kernel(r_ref, k_ref, v_ref, lo_ref, sr_ref, sk_ref, sv_ref, slo_ref, s_ref,
                      mur_ref, muk_ref, muv_ref, mulo_ref, w0_ref, w2t_ref, a0_ref, a2t_ref, g2t_ref,
                      kk_ref, ka_ref, rk_ref, lg_ref, lb_ref,
                      rw_ref, sout_ref,
                      w_scr, nkk_scr, bb_scr, km_scr, rr_scr, vv_scr, bonus_scr, g_scr, o_scr):
    T = r_ref.shape[0]

    def shifted(ref, s0_ref, mu_ref, t):
        cur = ref[t]
        prev = s0_ref[...] if t == 0 else ref[t - 1]
        return cur + (prev - cur) * mu_ref[...]

    for t in range(T):
        r = shifted(r_ref, sr_ref, mur_ref, t)
        k = shifted(k_ref, sk_ref, muk_ref, t)
        v = shifted(v_ref, sv_ref, muv_ref, t)
        lo = shifted(lo_ref, slo_ref, mulo_ref, t)
        wl, al, gl = lo[0:DECAY_LORA], lo[DECAY_LORA:DECAY_LORA + AAA_LORA], lo[DECAY_LORA + AAA_LORA:]
        neg = -(w0_ref[...] + _dot(w2t_ref[...], jnp.tanh(wl).astype(BF16)))
        softplus = jnp.maximum(neg, 0.0) + jnp.log(1.0 + jnp.exp(-jnp.abs(neg)))
        w = jnp.exp(-jnp.exp(-softplus - 0.5))
        a = jax.nn.sigmoid(a0_ref[...] + _dot(a2t_ref[...], al.astype(BF16)))
        g_scr[t] = _dot(g2t_ref[...], jax.nn.sigmoid(gl).astype(BF16))
        kkr = k * kk_ref[...]
        kk = kkr / jnp.maximum(jnp.sqrt(jnp.sum(kkr * kkr, axis=0, keepdims=True)), 1e-12)
        kmod = k * (1.0 + (a - 1.0) * ka_ref[...])
        bonus_scr[t] = jnp.sum(r * kmod * rk_ref[...], axis=0, keepdims=True) * v
        w_scr[t] = w
        nkk_scr[t] = -kk
        bb_scr[t] = kk * a
        km_scr[t] = kmod
        rr_scr[t] = r
        vv_scr[t] = v

    def row(i, carry):
        s = s_ref[i]
        for t in range(T):
            sa = jnp.sum(s * nkk_scr[t], axis=0, keepdims=True)
            s = s * w_scr[t] + sa * bb_scr[t] + vv_scr[t, pl.ds(i, 1), :] * km_scr[t]
            o_scr[t, pl.ds(i, 1), :] = jnp.sum(s * rr_scr[t], axis=0, keepdims=True)
        sout_ref[i] = s
        return carry

    lax.fori_loop(0, HEAD_DIM, row, 0, unroll=4)
    o = o_scr[...]
    mean = jnp.mean(o, axis=1, keepdims=True)
    d = o - mean
    var = jnp.mean(d * d, axis=1, keepdims=True)
    on = d * lax.rsqrt(var + LNX_EPS) * lg_ref[...] + lb_ref[...]
    rw_ref[...] = ((on + bonus_scr[...]) * g_scr[...]).astype(rw_ref.dtype)


def _rwkv_scan(pt, shift0, s0t, lw):
    T, _, DB = pt.shape
    W, HD = RWKV_W, HEAD_DIM
    lora_w = DECAY_LORA + AAA_LORA + GATE_LORA
    assert (3 * W) % lora_w == 0
    col = lambda t: jnp.broadcast_to(t.reshape(-1, 1), (t.size, DB))
    shift_t = shift0.T
    mu_c = col(lw["mu"])
    head = lambda base: (lambda h: (base + h, 0))
    pt_blk = lambda base: pl.BlockSpec((T, HD, DB), lambda h: (0, base + h, 0))
    blk2 = lambda base: pl.BlockSpec((HD, DB), head(base))
    lo_pt = pl.BlockSpec((T, lora_w, DB), lambda h: (0, 3 * W // lora_w, 0))
    lo_2d = pl.BlockSpec((lora_w, DB), lambda h: (3 * W // lora_w, 0))
    nh = W // HD
    st_spec = pl.BlockSpec((None, HD, HD, DB), lambda h: (h, 0, 0, 0))
    wt_spec = lambda n: pl.BlockSpec((HD, n), lambda h: (h, 0))
    scr = lambda: pltpu.VMEM((T, HD, DB), F32)
    return pl.pallas_call(
        _rwkv_scan_kernel,
        grid=(nh,),
        in_specs=[pt_blk(0), pt_blk(nh), pt_blk(2 * nh), lo_pt,
                  blk2(0), blk2(nh), blk2(2 * nh), lo_2d, st_spec,
                  blk2(0), blk2(nh), blk2(2 * nh), lo_2d,
                  blk2(0), wt_spec(DECAY_LORA), blk2(0), wt_spec(AAA_LORA), wt_spec(GATE_LORA),
                  blk2(0), blk2(0), blk2(0), blk2(0), blk2(0)],
        out_specs=[pl.BlockSpec((T, HD, DB), lambda h: (0, h, 0)), st_spec],
        out_shape=[jax.ShapeDtypeStruct((T, W, DB), F32), jax.ShapeDtypeStruct(s0t.shape, F32)],
        scratch_shapes=[scr() for _ in range(9)],
        compiler_params=_params("arbitrary"),
        name="rwkv_scan",
    )(pt, pt, pt, pt, shift_t, shift_t, shift_t, shift_t, s0t,
      mu_c, mu_c, mu_c, mu_c, col(lw["w0"]), lw["w2"].T.astype(BF16), col(lw["a0"]), lw["a2"].T.astype(BF16),
      lw["g2"].T.astype(BF16), col(lw["k_k"]), col(lw["k_a"]), col(lw["r_k"]), col(lw["lnx_g"]), col(lw["lnx_b"]))


def _mlp_kernel(att_ref, rw_ref, x_ref, gt1_ref, sc2_ref, sh2_ref, gt2_ref, g2_ref, gf_ref,
                wo_ref, w1_ref, w2_ref, y_ref, *, final_norm, ff_chunk, rep):
    def mod(ref):
        m = ref[...]
        if rep is None:
            return m
        return jnp.broadcast_to(m[:, None, :], (m.shape[0], rep, m.shape[1])).reshape(m.shape[0] * rep, m.shape[1])

    half = wo_ref.shape[0] // 2
    mix = _dot(att_ref[...], wo_ref[0:half, :]) + _dot(rw_ref[...], wo_ref[half:, :])
    x1 = x_ref[...] + mod(gt1_ref) * mix
    ms = jnp.mean(x1 * x1, axis=-1, keepdims=True)
    h2 = x1 * lax.rsqrt(ms + NORM_EPS) * g2_ref[...]
    h2 = (h2 * (1.0 + mod(sc2_ref)) + mod(sh2_ref)).astype(BF16)
    acc = jnp.zeros_like(x1)
    for c in range(w1_ref.shape[1] // ff_chunk):
        hid = _dot(h2, w1_ref[:, c * ff_chunk:(c + 1) * ff_chunk])
        hid = jnp.square(jnp.maximum(hid, 0.0)).astype(BF16)
        acc = acc + _dot(hid, w2_ref[c * ff_chunk:(c + 1) * ff_chunk, :])
    x2 = x1 + mod(gt2_ref) * acc
    if final_norm:
        ms2 = jnp.mean(x2 * x2, axis=-1, keepdims=True)
        x2 = x2 * lax.rsqrt(ms2 + NORM_EPS) * gf_ref[...]
    y_ref[...] = x2


def _mlp(att, rw, x, gt1, sc2, sh2, gt2, g2, gf, wo_b, w1_b, w2_b, tm, rep, final_norm):
    G, R, D = x.shape
    dff = w1_b.shape[1]
    row_spec = lambda w: pl.BlockSpec((None, tm, w), lambda b, i: (b, i, 0))
    if rep is not None:
        mod_spec = pl.BlockSpec((None, tm // rep, D), lambda b, i: (b, i, 0))
    else:
        mod_spec = pl.BlockSpec((None, 1, D), lambda b, i: (b, 0, 0))
    const = lambda shape: pl.BlockSpec(shape, lambda b, i: (0, 0), pipeline_mode=pl.Buffered(1))
    return pl.pallas_call(
        functools.partial(_mlp_kernel, final_norm=final_norm, ff_chunk=1024, rep=rep),
        grid=(G, R // tm),
        in_specs=[row_spec(att.shape[-1]), row_spec(rw.shape[-1]), row_spec(D),
                  mod_spec, mod_spec, mod_spec, mod_spec, const((1, D)), const((1, D)),
                  const(wo_b.shape), const((D, dff)), const((dff, D))],
        out_specs=row_spec(D),
        out_shape=jax.ShapeDtypeStruct((G, R, D), F32),
        compiler_params=_params("arbitrary", "arbitrary"),
        name="mlp",
    )(att, rw, x, gt1, sc2, sh2, gt2, g2.reshape(1, D), gf.reshape(1, D), wo_b, w1_b, w2_b)


def kernel(x_prompt, x_sample, cache_k, cache_v, state_wkv, state_shift, c_prompt, c_sample, norm1_g, norm2_g, w_ada, b_ada, w_in, w_out, mu, w0, w2, a0, a2, g2, k_k, k_a, r_k, lnx_g, lnx_b, w_ff1, w_ff2, normf_g):
    B, S, D = x_prompt.shape
    DB, T, _ = x_sample.shape
    depth = w_in.shape[0]
    dt = x_prompt.dtype
    tabs_p = _rope_tables(jnp.arange(S))
    tabs_t = tuple(jnp.repeat(t, DB, axis=0) for t in _rope_tables(PAST_LEN + jnp.arange(T)))
    c_all = jnp.concatenate([c_prompt, c_sample], axis=0)
    hp = x_prompt
    hs = x_sample.reshape(1, DB * T, D)
    outs = [[] for _ in range(8)]
    tm_s = min(ROW_TILE, DB * T)
    for l in range(depth):
        last = l == depth - 1
        mod = _ada(c_all, w_ada[l], b_ada[l])
        mod_p = [m.reshape(B, 1, D) for m in jnp.split(mod[:B], 6, axis=-1)]
        mod_t = [m.reshape(1, DB, D) for m in jnp.split(mod[B:], 6, axis=-1)]
        w_in_b, wo_b = w_in[l].astype(BF16), w_out[l].astype(BF16)
        w1_b, w2_b = w_ff1[l].astype(BF16), w_ff2[l].astype(BF16)
        lw = dict(mu=mu[l], w0=w0[l], w2=w2[l], a0=a0[l], a2=a2[l], g2=g2[l], k_k=k_k[l], k_a=k_a[l],
                  r_k=r_k[l], lnx_g=lnx_g[l], lnx_b=lnx_b[l])

        sh1_p, sc1_p, gt1_p, sh2_p, sc2_p, gt2_p = mod_p
        _, _, gt1_s, sh2_s, sc2_s, gt2_s = mod_t
        q, k, v, P, kt, vt = _inproj(hp, sc1_p, sh1_p, norm1_g[l], w_in_b, tabs_p, ROW_TILE, "batch", True)
        hs_t = jnp.transpose(hs.reshape(DB, T, D), (1, 0, 2)).reshape(1, T * DB, D)
        q_t, k_t, v_t, P_t, kt_s, vt_s, pt_s = _inproj(hs_t, mod_t[1], mod_t[0], norm1_g[l], w_in_b, tabs_t, DB,
                                                       "tile", True, nq=T, p_t=True)
        by_request = lambda t: jnp.transpose(t.reshape(T, DB, ATT_W), (1, 0, 2))

        att = _attn_fused(q, k, v)
        sample_attn_inputs = (by_request(q_t), by_request(k_t), by_request(v_t),
                              jnp.transpose(cache_k[l], (0, 2, 3, 1)), jnp.transpose(cache_v[l], (0, 2, 3, 1)))
        rw, s_p, shift_p, att_s = _rwkv(P, jnp.zeros((B, RWKV_IN), F32),
                                        jnp.zeros((B, RWKV_HEADS, HEAD_DIM, HEAD_DIM), F32), lw, RWKV_CHUNK,
                                        sample_attn_inputs)
        hp = _mlp(att, rw, hp, gt1_p, sc2_p, sh2_p, gt2_p, norm2_g[l], normf_g, wo_b, w1_b, w2_b,
                  ROW_TILE, None, last)
        keep = min(max(w for w, _ in DIL_BRANCHES), S)
        win = lambda t: jnp.transpose(t.reshape(B, ATT_HEADS, HEAD_DIM, S)[..., S - keep:], (0, 3, 1, 2))
        outs[0].append(win(kt))
        outs[1].append(win(vt))
        outs[2].append(s_p)
        outs[3].append(shift_p)

        rw_t, s_t = _rwkv_scan(pt_s.reshape(T, RWKV_IN, DB), state_shift[l],
                               jnp.transpose(state_wkv[l], (1, 2, 3, 0)), lw)
        rw_s = jnp.transpose(rw_t, (2, 0, 1)).astype(BF16)
        hs = _mlp(att_s.reshape(1, DB * T, ATT_W), rw_s.reshape(1, DB * T, RWKV_W), hs, gt1_s, sc2_s, sh2_s, gt2_s,
                  norm2_g[l], normf_g, wo_b, w1_b, w2_b, tm_s, T, last)
        new = lambda t: jnp.transpose(t.reshape(T, ATT_HEADS, HEAD_DIM, DB), (3, 0, 1, 2))
        outs[4].append(new(kt_s))
        outs[5].append(new(vt_s))
        outs[6].append(jnp.transpose(s_t, (3, 0, 1, 2)))
        outs[7].append(P_t.reshape(T, DB, RWKV_IN)[T - 1])
    stack = lambda xs: jnp.stack(xs).astype(dt)
    return (hp.astype(dt), hs.reshape(DB, T, D).astype(x_sample.dtype), *[stack(o) for o in outs])
```

```python
import functools

import numpy as np
import jax
import jax.numpy as jnp
from jax import lax
from jax.experimental import pallas as pl
from jax.experimental.pallas import tpu as pltpu

F32 = jnp.float32
BF16 = jnp.bfloat16

HEAD_DIM = 64
ATT_HEADS = 8
ATT_W = ATT_HEADS * HEAD_DIM
RWKV_HEADS = 8
RWKV_W = RWKV_HEADS * HEAD_DIM
DIL_BRANCHES = ((128, 1), (512, 4), (2048, 16))
ROT_DIM = HEAD_DIM // 4
ROPE_THETA = 500000.0
DECAY_LORA = 64
AAA_LORA = 64
GATE_LORA = 128
RWKV_IN = 3 * RWKV_W + DECAY_LORA + AAA_LORA + GATE_LORA
NORM_EPS = 1e-6
LNX_EPS = 64e-5
PAST_LEN = 8192
ATT_BLOCK = 128
NEG_BIG = -1e30
LOG2_E = 1.4426950408889634
VMEM_LIMIT = 56 * 1024 * 1024
ROW_TILE = 512
RWKV_CHUNK = 64
RWKV_CHUNKS_PER_STEP = 4
ATT_GROUP = 8
ADA_TILE = 1024


def _dot(a, b):
    return jnp.dot(a, b, preferred_element_type=F32)


def _dot_nt(a, b):
    return lax.dot_general(a, b, (((1,), (1,)), ((), ())), preferred_element_type=F32)


def _dot_tn(a, b):
    return lax.dot_general(a, b, (((0,), (0,)), ((), ())), preferred_element_type=F32)


def _split2(x):
    hi = x.astype(BF16)
    lo = (x - hi.astype(F32)).astype(BF16)
    return hi, lo


def _head_sums(x, e):
    C, W = x.shape
    n = W // e.shape[0]
    xb = x.astype(BF16)
    o = _dot(jnp.concatenate([xb[:, e.shape[0] * p:e.shape[0] * (p + 1)] for p in range(n)], axis=0), e)
    return jnp.concatenate([o[C * p:C * (p + 1)] for p in range(n)], axis=1)


def _params(*sem):
    return pltpu.CompilerParams(dimension_semantics=sem, vmem_limit_bytes=VMEM_LIMIT)


def _ada_kernel(c_ref, w_ref, b_ref, o_ref):
    c = c_ref[...]
    s = c * jax.nn.sigmoid(c)
    o_ref[...] = _dot(s.astype(BF16), w_ref[...].astype(BF16)) + b_ref[...]


def _ada(c, w_ada, b_ada):
    n, d = c.shape
    cols = w_ada.shape[1]
    tn = ADA_TILE
    assert cols % tn == 0
    return pl.pallas_call(
        _ada_kernel,
        grid=(cols // tn,),
        in_specs=[pl.BlockSpec((n, d), lambda j: (0, 0)),
                  pl.BlockSpec((d, tn), lambda j: (0, j)),
                  pl.BlockSpec((1, tn), lambda j: (0, j))],
        out_specs=pl.BlockSpec((n, tn), lambda j: (0, j)),
        out_shape=jax.ShapeDtypeStruct((n, cols), F32),
        compiler_params=_params("arbitrary"),
        name="ada",
    )(c, w_ada, b_ada.reshape(1, cols))


def _rope(t, cos, s1, s2):
    parts = []
    for c in range(ATT_W // 128):
        tc = t[:, 128 * c:128 * (c + 1)]
        parts.append(tc * cos + pltpu.roll(tc, 128 - ROT_DIM // 2, 1) * s1 + pltpu.roll(tc, ROT_DIM // 2, 1) * s2)
    return jnp.concatenate(parts, axis=1)


def _inproj_kernel(x_ref, sc_ref, sh_ref, g_ref, w_ref, cos_ref, s1_ref, s2_ref, *out_refs,
                   transposed, p_t):
    q_ref, k_ref, v_ref, p_ref = out_refs[:4]
    rest = list(out_refs[4:])
    x = x_ref[...]
    ms = jnp.mean(x * x, axis=-1, keepdims=True)
    h = x * lax.rsqrt(ms + NORM_EPS) * g_ref[...]
    h = (h * (1.0 + sc_ref[...]) + sh_ref[...]).astype(BF16)
    cos, s1, s2 = cos_ref[...], s1_ref[...], s2_ref[...]
    q = _rope(_dot(h, w_ref[:, 0:ATT_W]), cos, s1, s2)
    q_ref[...] = (q * HEAD_DIM ** -0.5).astype(q_ref.dtype)
    k = _rope(_dot(h, w_ref[:, ATT_W:2 * ATT_W]), cos, s1, s2)
    v = _dot(h, w_ref[:, 2 * ATT_W:3 * ATT_W])
    k_ref[...] = k
    v_ref[...] = v
    p = _dot(h, w_ref[:, 3 * ATT_W:])
    p_ref[...] = p
    if transposed:
        rest.pop(0)[...] = k.T
        rest.pop(0)[...] = v.T
    if p_t:
        rest.pop(0)[...] = p.T


def _inproj(x, sc, sh, g, w_in_b, tabs, tm, mod_mode, transposed, nq=1, p_t=False):
    G, R, D = x.shape
    nin = w_in_b.shape[1]
    row_spec = lambda w: pl.BlockSpec((None, tm, w), lambda b, i: (b, i, 0))
    mod_spec = {"batch": pl.BlockSpec((None, 1, D), lambda b, i: (b, 0, 0)),
                "tile": pl.BlockSpec((None, tm, D), lambda b, i: (b, 0, 0))}[mod_mode]
    tab_spec = pl.BlockSpec((tm, 128), lambda b, i: (i, 0))
    const = lambda shape: pl.BlockSpec(shape, lambda b, i: (0, 0))
    out_specs = [row_spec(ATT_W)] * 3 + [row_spec(RWKV_IN)]
    out_shape = [jax.ShapeDtypeStruct((G, R, ATT_W), F32)] * 3 + [jax.ShapeDtypeStruct((G, R, RWKV_IN), F32)]
    if transposed:
        spq = (R // nq) // tm
        t_spec = pl.BlockSpec((None, None, ATT_W, tm), lambda b, i: (b, i // spq, 0, i % spq))
        out_specs += [t_spec] * 2
        out_shape += [jax.ShapeDtypeStruct((G, nq, ATT_W, R // nq), F32)] * 2
        if p_t:
            out_specs.append(pl.BlockSpec((None, None, RWKV_IN, tm), lambda b, i: (b, i // spq, 0, i % spq)))
            out_shape.append(jax.ShapeDtypeStruct((G, nq, RWKV_IN, R // nq), F32))
    return pl.pallas_call(
        functools.partial(_inproj_kernel, transposed=transposed, p_t=p_t),
        grid=(G, R // tm),
        in_specs=[row_spec(D), mod_spec, mod_spec, const((1, D)), const((D, nin)),
                  tab_spec, tab_spec, tab_spec],
        out_specs=out_specs,
        out_shape=out_shape,
        compiler_params=_params("arbitrary", "arbitrary"),
        name="inproj",
    )(x, sc, sh, g.reshape(1, D), w_in_b, *tabs)


def _rope_tables(pos):
    half = ROT_DIM // 2
    inv = ROPE_THETA ** (-jnp.arange(half, dtype=F32) * (2.0 / ROT_DIM))
    ang = pos.astype(F32)[:, None] * inv[None, :]
    cos, sin = jnp.cos(ang), jnp.sin(ang)
    n = pos.shape[0]
    ones = jnp.ones((n, HEAD_DIM - ROT_DIM), F32)
    zeros = jnp.zeros((n, HEAD_DIM - ROT_DIM), F32)
    zh = jnp.zeros((n, half), F32)
    ctab = jnp.concatenate([cos, cos, ones], axis=1)
    s1 = jnp.concatenate([-sin, zh, zeros], axis=1)
    s2 = jnp.concatenate([zh, sin, zeros], axis=1)
    return tuple(jnp.tile(t, (1, 128 // HEAD_DIM)) for t in (ctab, s1, s2))


def _attn_fused_kernel(q_ref, k_ref, v_ref, o_ref, acc_scr, m_scr, l_scr, *, dilations, group):
    nw, HD = ATT_BLOCK, HEAD_DIM
    S = q_ref.shape[0]
    n_tiles = S // nw
    lane = lax.broadcasted_iota(jnp.int32, (nw, 2 * HD), 1)
    first_head = lane < HD
    q_idx = lax.broadcasted_iota(jnp.int32, (nw, 2 * HD), 0)
    n_col = nw // HD
    key_idx = [HD * jc + (lane & (HD - 1)) for jc in range(n_col)]
    cur_ok = [q_idx >= key_idx[jc] for jc in range(n_col)]

    def bd(x):
        zero = jnp.zeros_like(x)
        xa, xb = jnp.where(first_head, x, zero), jnp.where(first_head, zero, x)
        parts = []
        for jc in range(n_col):
            parts += [xa[HD * jc:HD * (jc + 1)], xb[HD * jc:HD * (jc + 1)]]
        return jnp.concatenate(parts, axis=0)

    bd_row = lax.broadcasted_iota(jnp.int32, (2 * nw, 2 * HD), 0)
    bd_lane = lax.broadcasted_iota(jnp.int32, (2 * nw, 2 * HD), 1)
    ones_bd = jnp.where(((bd_row // HD) % 2 == 0) == (bd_lane < HD), 1.0, 0.0).astype(BF16)

    def with_ones(v):
        return jnp.concatenate([bd(v.astype(BF16)), ones_bd], axis=1)

    def half_reduce(x, op, fill):
        ra = op(jnp.where(first_head, x, fill), axis=-1, keepdims=True)
        rb = op(jnp.where(first_head, fill, x), axis=-1, keepdims=True)
        return jnp.where(first_head, ra, rb)

    def steps(tiles, use_prev, init, final):
        n = range(len(tiles))
        qs = [(q_ref[tiles[t][0], :] * LOG2_E).astype(BF16) for t in n]
        cols, vbd = [], []
        for t in n:
            rows, prev_rows, has_prev = tiles[t]
            s = _dot_nt(qs[t], bd(k_ref[rows, :].astype(BF16)))
            c = [jnp.where(cur_ok[jc], s[:, 2 * HD * jc:2 * HD * (jc + 1)], NEG_BIG) for jc in range(n_col)]
            vb = [with_ones(v_ref[rows, :])]
            if use_prev:
                sp = _dot_nt(qs[t], bd(k_ref[prev_rows, :].astype(BF16)))
                thresh = jnp.where(has_prev, 0, nw)
                c += [jnp.where((key_idx[jc] - q_idx) >= thresh, sp[:, 2 * HD * jc:2 * HD * (jc + 1)], NEG_BIG)
                      for jc in range(n_col)]
                vb.append(with_ones(v_ref[prev_rows, :]))
            cols.append(c)
            vbd.append(vb)
        m_new, m_old, p = [], [], []
        for t in n:
            m = half_reduce(functools.reduce(jnp.maximum, cols[t]), jnp.max, NEG_BIG)
            if not init:
                m_old.append(m_scr[tiles[t][0], :])
                m = jnp.maximum(m, m_old[t])
            m_new.append(m)
            p.append([jnp.exp2(c - m) for c in cols[t]])
        acc = []
        for t in n:
            a = _dot(jnp.concatenate(p[t][0:n_col], axis=1).astype(BF16), vbd[t][0])
            if use_prev:
                a = a + _dot(jnp.concatenate(p[t][n_col:], axis=1).astype(BF16), vbd[t][1])
            acc.append(a)
        for t in n:
            rows = tiles[t][0]
            a, l = acc[t][:, 0:2 * HD], acc[t][:, 2 * HD:]
            if not init:
                alpha = jnp.exp2(m_old[t] - m_new[t])
                a = a + acc_scr[rows, :] * alpha
                l = l + l_scr[rows, :] * alpha
            if final:
                o_ref[rows, :] = (a / l).astype(o_ref.dtype)
            else:
                acc_scr[rows, :] = a
                m_scr[rows, :] = m_new[t]
                l_scr[rows, :] = l

    n_br = len(dilations)
    for bi, dil in enumerate(dilations):
        init, final = bi == 0, bi == n_br - 1
        use_prev = S // (dil * nw) > 1

        def group_body(g, carry, dil=dil, use_prev=use_prev, init=init, final=final):
            tiles = []
            for i in range(group):
                t = g * group + i
                r, j = lax.rem(t, dil), lax.div(t, dil)
                start = r + j * (dil * nw)
                prev = r + jnp.maximum(j - 1, 0) * (dil * nw)
                if dil > 1:
                    tiles.append((pl.ds(start, nw, stride=dil), pl.ds(prev, nw, stride=dil), j > 0))
                else:
                    tiles.append((pl.ds(pl.multiple_of(start, nw), nw), pl.ds(pl.multiple_of(prev, nw), nw), j > 0))
            steps(tiles, use_prev, init, final)
            return carry

        lax.fori_loop(0, n_tiles // group, group_body, 0, unroll=True)


def _attn_fused(q, k, v):
    B, S, W = q.shape
    nw = ATT_BLOCK
    dilations = tuple(sorted((d for _, d in DIL_BRANCHES), reverse=True))
    for win, d in DIL_BRANCHES:
        assert win == d * nw and S % (d * nw) == 0
    assert (S // nw) % ATT_GROUP == 0
    PW = 2 * HEAD_DIM
    spec = pl.BlockSpec((None, S, PW), lambda b, p: (b, 0, p))
    return pl.pallas_call(
        functools.partial(_attn_fused_kernel, dilations=dilations, group=ATT_GROUP),
        grid=(B, W // PW),
        in_specs=[spec, spec, spec],
        out_specs=spec,
        out_shape=jax.ShapeDtypeStruct((B, S, W), BF16),
        scratch_shapes=[pltpu.VMEM((S, PW), F32) for _ in range(3)],
        compiler_params=_params("arbitrary", "arbitrary"),
        name="attn_fused",
    )(q, k, v)


def _sample_counts(T, WB):
    i = np.arange(T)
    c = np.arange(WB)
    delta = WB + i[None, :] - c[:, None]
    cnt_c = np.zeros((WB, T), np.float32)
    diff = i[None, :] - i[:, None]
    cnt_n = np.zeros((T, T), np.float32)
    for win, dil in DIL_BRANCHES:
        cnt_c += ((delta > 0) & (delta % dil == 0) & (delta <= win)).astype(np.float32)
        cnt_n += ((diff >= 0) & (diff % dil == 0) & (diff <= win)).astype(np.float32)
    return cnt_c, cnt_n


def _sample_attn_scores(q_ref, kn_ref, vn_ref, kt_ref, vt_ref, cc_ref, cn_ref):
    cc, cn = cc_ref[...], cn_ref[...]
    nb = q_ref.shape[0]
    items = [(b, h) for b in range(nb) for h in range(ATT_HEADS)]
    sl = lambda h: slice(HEAD_DIM * h, HEAD_DIM * (h + 1))
    qh = {(b, h): q_ref[b, :, sl(h)] for b, h in items}
    s_c = {i: jnp.where(cc > 0, _dot(qh[i].astype(BF16), kt_ref[i[0], i[1]].astype(BF16)), NEG_BIG)
           for i in items}
    s_n = {i: jnp.where(cn > 0, _dot_nt(qh[i], kn_ref[i[0], :, sl(i[1])]), NEG_BIG) for i in items}
    return s_c, s_n


def _sample_attn_finish(scores, q_ref, kn_ref, vn_ref, kt_ref, vt_ref, cc_ref, cn_ref, o_ref):
    s_c, s_n = scores
    cc, cn = cc_ref[...], cn_ref[...]
    nb = q_ref.shape[0]
    items = list(s_c)
    sl = lambda h: slice(HEAD_DIM * h, HEAD_DIM * (h + 1))
    p_c, p_n, l = {}, {}, {}
    for i in items:
        m = jnp.maximum(jnp.max(s_c[i], axis=-1, keepdims=True), jnp.max(s_n[i], axis=-1, keepdims=True))
        p_c[i] = cc * jnp.exp(s_c[i] - m)
        p_n[i] = cn * jnp.exp(s_n[i] - m)
        l[i] = jnp.sum(p_c[i], axis=-1, keepdims=True) + jnp.sum(p_n[i], axis=-1, keepdims=True)
    acc = {i: _dot_nt(p_c[i].astype(BF16), vt_ref[i[0], i[1]].astype(BF16)) + _dot(p_n[i], vn_ref[i[0], :, sl(i[1])])
           for i in items}
    for b in range(nb):
        o_ref[b] = jnp.concatenate([acc[(b, h)] / l[(b, h)] for h in range(ATT_HEADS)], axis=1).astype(o_ref.dtype)


def _sample_attn_operands(q, k_new, v_new, cache_kt, cache_vt, n_steps, step_index):
    DB, T, W = q.shape
    WB = cache_kt.shape[-1]
    assert DB % n_steps == 0, "the requests must split evenly over the host call's grid steps"
    nb = DB // n_steps
    cnt_c, cnt_n = _sample_counts(T, WB)
    cc = jnp.asarray(cnt_c.T)
    cn = jnp.asarray(cnt_n.T)
    new_spec = pl.BlockSpec((nb, T, W), lambda *g: (step_index(*g), 0, 0))
    cache_spec = pl.BlockSpec((nb, ATT_HEADS, HEAD_DIM, WB), lambda *g: (step_index(*g), 0, 0, 0))
    const = lambda arr: pl.BlockSpec(arr.shape, lambda *g: (0, 0))
    args = (q, k_new, v_new, cache_kt, cache_vt, cc, cn)
    in_specs = [new_spec, new_spec, new_spec, cache_spec, cache_spec, const(cc), const(cn)]
    return args, in_specs, new_spec, jax.ShapeDtypeStruct((DB, T, W), BF16)


def _rwkv_prologue(p_ref, shift_ref, prev_scr, shout_ref, mu_ref, w0_ref, w2_ref, a0_ref, a2_ref, g2_ref,
                   kk_ref, ka_ref, rk_ref, e, chunk):
    C = p_ref.shape[0]
    P = p_ref[...]
    row = lax.broadcasted_iota(jnp.int32, (C, 1), 0)
    p_prev = jnp.where(row == 0, prev_scr[...], pltpu.roll(P, 1, 0))
    pm = P + (p_prev - P) * mu_ref[...]
    prev_scr[...] = P[C - 1:C, :]
    shout_ref[...] = P[C - 1:C, :]

    W = RWKV_W
    r, k, v = pm[:, 0:W], pm[:, W:2 * W], pm[:, 2 * W:3 * W]
    lora_wa = pm[:, 3 * W:3 * W + DECAY_LORA + AAA_LORA]
    gl = pm[:, 3 * W + DECAY_LORA + AAA_LORA:]
    wpre = w0_ref[...] + _dot(jnp.tanh(lora_wa).astype(BF16), w2_ref[...])
    neg = -wpre
    softplus = jnp.maximum(neg, 0.0) + jnp.log(1.0 + jnp.exp(-jnp.abs(neg)))
    logw = -jnp.exp(-softplus - 0.5)
    a = jax.nn.sigmoid(a0_ref[...] + _dot(lora_wa.astype(BF16), a2_ref[...]))
    g = _dot(jax.nn.sigmoid(gl).astype(BF16), g2_ref[...])
    kkr = k * kk_ref[...]
    nrm = jnp.sqrt(_head_sums(kkr * kkr, e))
    kk = kkr / jnp.maximum(nrm, 1e-12)
    kmod = k * (1.0 + (a - 1.0) * ka_ref[...])
    bonus = _head_sums(r * kmod * rk_ref[...], e) * v

    ti = lax.broadcasted_iota(jnp.int32, (C, C), 0)
    si = lax.broadcasted_iota(jnp.int32, (C, C), 1)
    tri_b = jnp.where((ti >= si) & (lax.div(ti, chunk) == lax.div(si, chunk)), 1.0, 0.0).astype(BF16)
    lw_h, lw_l = _split2(logw)
    L = _dot(tri_b, lw_h) + _dot(tri_b, lw_l)
    e_l = jnp.exp(L)
    e_nl = jnp.exp(-L)
    a_t = -kk * jnp.exp(L - logw)
    r_t = r * e_l
    b_t = kk * a * e_nl
    k_t = kmod * e_nl
    return v, a_t, r_t, b_t, k_t, e_l, bonus, g


def _rwkv_epilogue(o, bonus, g, e, lg_ref, lb_ref, rw_ref):
    mean = _head_sums(o, e) * (1.0 / HEAD_DIM)
    d = o - mean
    var = _head_sums(d * d, e) * (1.0 / HEAD_DIM)
    on = d * lax.rsqrt(var + LNX_EPS) * lg_ref[...] + lb_ref[...]
    rw_ref[...] = ((on + bonus) * g).astype(rw_ref.dtype)


def _rwkv_pair_kernel(*refs, n_double, n_side_in):
    (p_ref, shift_ref, s0_ref, mu_ref, w0_ref, w2_ref, a0_ref, a2_ref, g2_ref,
     kk_ref, ka_ref, rk_ref, lg_ref, lb_ref, e_ref) = refs[:15]
    side_in = refs[15:15 + n_side_in]
    rw_ref, sout_ref, shout_ref, side_out, s_scr, prev_scr = refs[15 + n_side_in:]

    C = HEAD_DIM
    n_sub = p_ref.shape[0] // C
    n_pairs = RWKV_HEADS // 2
    PW = 2 * HEAD_DIM

    @pl.when(pl.program_id(1) == 0)
    def _():
        s_scr[...] = s0_ref[...]
        prev_scr[...] = shift_ref[...]

    side_scores = _sample_attn_scores(*side_in)
    e = e_ref[...]
    v, a_t, r_t, b_t, k_t, e_l, bonus, g = _rwkv_prologue(
        p_ref, shift_ref, prev_scr, shout_ref, mu_ref, w0_ref, w2_ref, a0_ref, a2_ref, g2_ref,
        kk_ref, ka_ref, rk_ref, e, C)
    _sample_attn_finish(side_scores, *side_in, side_out)

    lane = lax.broadcasted_iota(jnp.int32, (C, PW), 1)
    ti = lax.broadcasted_iota(jnp.int32, (C, PW), 0)
    si = lane & (HEAD_DIM - 1)
    first_head = lane < HEAD_DIM
    tri_incl = ti >= si
    tri_strict = ti > si
    eye = jnp.where(ti == si, 1.0, 0.0)
    rows2 = lax.broadcasted_iota(jnp.int32, (2 * C, PW), 0)
    lanes2 = lax.broadcasted_iota(jnp.int32, (2 * C, PW), 1)
    same_head = (rows2 < C) == (lanes2 < HEAD_DIM)

    def bd(x):
        zero = jnp.zeros_like(x)
        return jnp.concatenate([jnp.where(first_head, x, zero), jnp.where(first_head, zero, x)], axis=0)

    def dot_bd(x, y):
        return _dot(x.astype(BF16), bd(y.astype(BF16)))

    items = [(c, p) for c in range(n_sub) for p in range(n_pairs)]
    tile = lambda t, c, p: t[C * c:C * (c + 1), PW * p:PW * (p + 1)]
    ar = {i: jnp.concatenate([tile(a_t, *i), tile(r_t, *i)], axis=0).astype(BF16) for i in items}
    bk = {i: jnp.concatenate([tile(b_t, *i), tile(k_t, *i)], axis=0).astype(BF16) for i in items}
    gm = {i: _dot_nt(ar[i], jnp.concatenate([bd(bk[i][0:C]), bd(bk[i][C:2 * C])], axis=0)) for i in items}
    m_ab = {i: jnp.where(tri_strict, gm[i][0:C, 0:PW], 0.0) for i in items}
    m_ak = {i: jnp.where(tri_strict, gm[i][0:C, PW:2 * PW], 0.0).astype(BF16) for i in items}
    n_r = {i: jnp.concatenate([jnp.where(tri_incl, gm[i][C:2 * C, 0:PW], 0.0),
                               jnp.where(tri_incl, gm[i][C:2 * C, PW:2 * PW], 0.0)], axis=1).astype(BF16)
           for i in items}
    t_inv = {i: eye + m_ab[i] for i in items}
    mp = m_ab
    for _ in range(n_double):
        mp = {i: dot_bd(mp[i], mp[i]) for i in items}
        t_inv = {i: t_inv[i] + dot_bd(t_inv[i], mp[i]) for i in items}
    v_b = {i: tile(v, *i).astype(BF16) for i in items}
    v_bd = {i: bd(v_b[i]) for i in items}
    mv = {i: _dot(m_ak[i], v_bd[i]) for i in items}
    state = [s_scr[p] for p in range(n_pairs)]
    o_rows = []
    for c in range(n_sub):
        its = [(c, p) for p in range(n_pairs)]
        ars = {i: _dot_nt(ar[i], state[i[1]].astype(BF16)) for i in its}
        u_b = {i: dot_bd(t_inv[i], ars[i][0:C] + mv[i]).astype(BF16) for i in its}
        o = [ars[i][C:2 * C] + _dot(n_r[i], jnp.concatenate([bd(u_b[i]), v_bd[i]], axis=0)) for i in its]
        o_rows.append(jnp.concatenate(o, axis=1))
        for i in its:
            upd = jnp.where(same_head, _dot_tn(jnp.concatenate([u_b[i], v_b[i]], axis=0), bk[i]), 0.0)
            state[i[1]] = (state[i[1]] + upd) * e_l[C * (c + 1) - 1:C * (c + 1), PW * i[1]:PW * (i[1] + 1)]
    for p in range(n_pairs):
        s_scr[p] = state[p]
        sout_ref[p] = state[p]
    _rwkv_epilogue(jnp.concatenate(o_rows, axis=0), bonus, g, e, lg_ref, lb_ref, rw_ref)


def _rwkv(P, shift0, s0, lw, C, sample_attn_inputs):
    B, T, _ = P.shape
    assert C == HEAD_DIM, "the pair-packed chunk kernel needs chunk length == head dim"
    rows = C * RWKV_CHUNKS_PER_STEP
    assert T % rows == 0
    n_double = max(int(np.log2(C)) - 1, 0)
    W = RWKV_W
    e = np.kron(np.eye(2, dtype=np.float32), np.ones((HEAD_DIM, HEAD_DIM), np.float32))
    zpad = jnp.zeros((DECAY_LORA, W), F32)
    w2p = jnp.concatenate([lw["w2"], jnp.zeros((AAA_LORA, W), F32)], axis=0).astype(BF16)
    a2p = jnp.concatenate([zpad, lw["a2"]], axis=0).astype(BF16)
    row = lambda t: t.reshape(1, -1)
    consts = [row(lw["mu"]), row(lw["w0"]), w2p, row(lw["a0"]), a2p, lw["g2"].astype(BF16),
              row(lw["k_k"]), row(lw["k_a"]), row(lw["r_k"]), row(lw["lnx_g"]), row(lw["lnx_b"]),
              jnp.asarray(e, BF16)]
    sp = s0.reshape(B, RWKV_HEADS // 2, 2, HEAD_DIM, HEAD_DIM)
    z = jnp.zeros_like(sp[:, :, 0])
    s0 = jnp.concatenate([jnp.concatenate([sp[:, :, 0], z], axis=-1),
                          jnp.concatenate([z, sp[:, :, 1]], axis=-1)], axis=-2)
    st_shape = s0.shape[1:]
    const = lambda arr: pl.BlockSpec(arr.shape, lambda b, c: (0, 0))
    st_spec = pl.BlockSpec((None,) + st_shape, lambda b, c: (b, 0, 0, 0))
    sh_spec = pl.BlockSpec((None, 1, RWKV_IN), lambda b, c: (b, 0, 0))
    n_c = T // rows
    side_args, side_specs, side_out_spec, side_out_shape = _sample_attn_operands(
        *sample_attn_inputs, B * n_c, lambda b, c: b * n_c + c)
    rw, s_out, sh_out, att_side = pl.pallas_call(
        functools.partial(_rwkv_pair_kernel, n_double=n_double, n_side_in=len(side_args)),
        grid=(B, n_c),
        in_specs=[pl.BlockSpec((None, rows, RWKV_IN), lambda b, c: (b, c, 0)), sh_spec, st_spec]
                 + [const(t) for t in consts] + side_specs,
        out_specs=[pl.BlockSpec((None, rows, W), lambda b, c: (b, c, 0)), st_spec, sh_spec, side_out_spec],
        out_shape=[jax.ShapeDtypeStruct((B, T, W), BF16),
                   jax.ShapeDtypeStruct((B,) + st_shape, F32),
                   jax.ShapeDtypeStruct((B, 1, RWKV_IN), F32), side_out_shape],
        scratch_shapes=[pltpu.VMEM(st_shape, F32),
                        pltpu.VMEM((1, RWKV_IN), F32)],
        compiler_params=_params("arbitrary", "arbitrary"),
        name=f"rwkv_c{C}",
    )(P, shift0.reshape(B, 1, RWKV_IN), s0, *consts, *side_args)
    s_out = jnp.stack([s_out[:, :, :HEAD_DIM, :HEAD_DIM], s_out[:, :, HEAD_DIM:, HEAD_DIM:]], axis=2)
    s_out = s_out.reshape(B, RWKV_HEADS, HEAD_DIM, HEAD_DIM)
    return rw, s_out, sh_out.reshape(B, RWKV_IN), att_side


def _rwkv_scan_kernel(r_ref, k_ref, v_ref, lo_ref, sr_ref, sk_ref, sv_ref, slo_ref, s_ref,
                      mur_ref, muk_ref, muv_ref, mulo_ref, w0_ref, w2t_ref, a0_ref, a2t_ref, g2t_ref,
                      kk_ref, ka_ref, rk_ref, lg_ref, lb_ref,
                      rw_ref, sout_ref,
                      w_scr, nkk_scr, bb_scr, km_scr, rr_scr, vv_scr, bonus_scr, g_scr, o_scr):
    T = r_ref.shape[0]

    def shifted(ref, s0_ref, mu_ref, t):
        cur = ref[t]
        prev = s0_ref[...] if t == 0 else ref[t - 1]
        return cur + (prev - cur) * mu_ref[...]

    for t in range(T):
        r = shifted(r_ref, sr_ref, mur_ref, t)
        k = shifted(k_ref, sk_ref, muk_ref, t)
        v = shifted(v_ref, sv_ref, muv_ref, t)
        lo = shifted(lo_ref, slo_ref, mulo_ref, t)
        wl, al, gl = lo[0:DECAY_LORA], lo[DECAY_LORA:DECAY_LORA + AAA_LORA], lo[DECAY_LORA + AAA_LORA:]
        neg = -(w0_ref[...] + _dot(w2t_ref[...], jnp.tanh(wl).astype(BF16)))
        softplus = jnp.maximum(neg, 0.0) + jnp.log(1.0 + jnp.exp(-jnp.abs(neg)))
        w = jnp.exp(-jnp.exp(-softplus - 0.5))
        a = jax.nn.sigmoid(a0_ref[...] + _dot(a2t_ref[...], al.astype(BF16)))
        g_scr[t] = _dot(g2t_ref[...], jax.nn.sigmoid(gl).astype(BF16))
        kkr = k * kk_ref[...]
        kk = kkr / jnp.maximum(jnp.sqrt(jnp.sum(kkr * kkr, axis=0, keepdims=True)), 1e-12)
        kmod = k * (1.0 + (a - 1.0) * ka_ref[...])
        bonus_scr[t] = jnp.sum(r * kmod * rk_ref[...], axis=0, keepdims=True) * v
        w_scr[t] = w
        nkk_scr[t] = -kk
        bb_scr[t] = kk * a
        km_scr[t] = kmod
        rr_scr[t] = r
        vv_scr[t] = v

    def row(i, carry):
        s = s_ref[i]
        for t in range(T):
            sa = jnp.sum(s * nkk_scr[t], axis=0, keepdims=True)
            s = s * w_scr[t] + sa * bb_scr[t] + vv_scr[t, pl.ds(i, 1), :] * km_scr[t]
            o_scr[t, pl.ds(i, 1), :] = jnp.sum(s * rr_scr[t], axis=0, keepdims=True)
        sout_ref[i] = s
        return carry

    lax.fori_loop(0, HEAD_DIM, row, 0, unroll=4)
    o = o_scr[...]
    mean = jnp.mean(o, axis=1, keepdims=True)
    d = o - mean
    var = jnp.mean(d * d, axis=1, keepdims=True)
    on = d * lax.rsqrt(var + LNX_EPS) * lg_ref[...] + lb_ref[...]
    rw_ref[...] = ((on + bonus_scr[...]) * g_scr[...]).astype(rw_ref.dtype)


def _rwkv_scan(pt, shift0, s0t, lw):
    T, _, DB = pt.shape
    W, HD = RWKV_W, HEAD_DIM
    lora_w = DECAY_LORA + AAA_LORA + GATE_LORA
    assert (3 * W) % lora_w == 0
    shift_t = shift0.T
    names = ("w0", "a0", "k_k", "k_a", "r_k", "lnx_g", "lnx_b")
    packed = jnp.concatenate([lw["mu"].reshape(-1)] + [lw[n].reshape(-1) for n in names])
    pc = jnp.broadcast_to(packed[:, None], (packed.size, DB))
    base = {n: (RWKV_IN + i * W) // HD for i, n in enumerate(names)}
    head = lambda base: (lambda h: (base + h, 0))
    pt_blk = lambda base: pl.BlockSpec((T, HD, DB), lambda h: (0, base + h, 0))
    blk2 = lambda base: pl.BlockSpec((HD, DB), head(base))
    lo_pt = pl.BlockSpec((T, lora_w, DB), lambda h: (0, 3 * W // lora_w, 0))
    lo_2d = pl.BlockSpec((lora_w, DB), lambda h: (3 * W // lora_w, 0))
    nh = W // HD
    st_spec = pl.BlockSpec((None, HD, HD, DB), lambda h: (h, 0, 0, 0))
    wt_spec = lambda n: pl.BlockSpec((HD, n), lambda h: (h, 0))
    scr = lambda: pltpu.VMEM((T, HD, DB), F32)
    return pl.pallas_call(
        _rwkv_scan_kernel,
        grid=(nh,),
        in_specs=[pt_blk(0), pt_blk(nh), pt_blk(2 * nh), lo_pt,
                  blk2(0), blk2(nh), blk2(2 * nh), lo_2d, st_spec,
                  blk2(0), blk2(nh), blk2(2 * nh), lo_2d,
                  blk2(base["w0"]), wt_spec(DECAY_LORA), blk2(base["a0"]), wt_spec(AAA_LORA), wt_spec(GATE_LORA),
                  blk2(base["k_k"]), blk2(base["k_a"]), blk2(base["r_k"]), blk2(base["lnx_g"]), blk2(base["lnx_b"])],
        out_specs=[pl.BlockSpec((T, HD, DB), lambda h: (0, h, 0)), st_spec],
        out_shape=[jax.ShapeDtypeStruct((T, W, DB), F32), jax.ShapeDtypeStruct(s0t.shape, F32)],
        scratch_shapes=[scr() for _ in range(9)],
        compiler_params=_params("arbitrary"),
        name="rwkv_scan",
    )(pt, pt, pt, pt, shift_t, shift_t, shift_t, shift_t, s0t,
      pc, pc, pc, pc, pc, lw["w2"].T.astype(BF16), pc, lw["a2"].T.astype(BF16),
      lw["g2"].T.astype(BF16), pc, pc, pc, pc, pc)


def _mlp_kernel(att_ref, rw_ref, x_ref, gt1_ref, sc2_ref, sh2_ref, gt2_ref, g2_ref, gf_ref,
                wo_ref, w1_ref, w2_ref, y_ref, *, final_norm, ff_chunk, rep):
    def mod(ref):
        m = ref[...]
        if rep is None:
            return m
        return jnp.broadcast_to(m[:, None, :], (m.shape[0], rep, m.shape[1])).reshape(m.shape[0] * rep, m.shape[1])

    half = wo_ref.shape[0] // 2
    mix = _dot(att_ref[...], wo_ref[0:half, :]) + _dot(rw_ref[...], wo_ref[half:, :])
    x1 = x_ref[...] + mod(gt1_ref) * mix
    ms = jnp.mean(x1 * x1, axis=-1, keepdims=True)
    h2 = x1 * lax.rsqrt(ms + NORM_EPS) * g2_ref[...]
    h2 = (h2 * (1.0 + mod(sc2_ref)) + mod(sh2_ref)).astype(BF16)
    acc = jnp.zeros_like(x1)
    for c in range(w1_ref.shape[1] // ff_chunk):
        hid = _dot(h2, w1_ref[:, c * ff_chunk:(c + 1) * ff_chunk])
        hid = jnp.square(jnp.maximum(hid, 0.0)).astype(BF16)
        acc = acc + _dot(hid, w2_ref[c * ff_chunk:(c + 1) * ff_chunk, :])
    x2 = x1 + mod(gt2_ref) * acc
    if final_norm:
        ms2 = jnp.mean(x2 * x2, axis=-1, keepdims=True)
        x2 = x2 * lax.rsqrt(ms2 + NORM_EPS) * gf_ref[...]
    y_ref[...] = x2


def _mlp(att, rw, x, gt1, sc2, sh2, gt2, g2, gf, wo_b, w1_b, w2_b, tm, rep, final_norm):
    G, R, D = x.shape
    dff = w1_b.shape[1]
    row_spec = lambda w: pl.BlockSpec((None, tm, w), lambda b, i: (b, i, 0))
    if rep is not None:
        mod_spec = pl.BlockSpec((None, tm // rep, D), lambda b, i: (b, i, 0))
    else:
        mod_spec = pl.BlockSpec((None, 1, D), lambda b, i: (b, 0, 0))
    const = lambda shape: pl.BlockSpec(shape, lambda b, i: (0, 0), pipeline_mode=pl.Buffered(1))
    return pl.pallas_call(
        functools.partial(_mlp_kernel, final_norm=final_norm, ff_chunk=1024, rep=rep),
        grid=(G, R // tm),
        in_specs=[row_spec(att.shape[-1]), row_spec(rw.shape[-1]), row_spec(D),
                  mod_spec, mod_spec, mod_spec, mod_spec, const((1, D)), const((1, D)),
                  const(wo_b.shape), const((D, dff)), const((dff, D))],
        out_specs=row_spec(D),
        out_shape=jax.ShapeDtypeStruct((G, R, D), F32),
        compiler_params=_params("arbitrary", "arbitrary"),
        name="mlp",
    )(att, rw, x, gt1, sc2, sh2, gt2, g2.reshape(1, D), gf.reshape(1, D), wo_b, w1_b, w2_b)


def kernel(x_prompt, x_sample, cache_k, cache_v, state_wkv, state_shift, c_prompt, c_sample, norm1_g, norm2_g, w_ada, b_ada, w_in, w_out, mu, w0, w2, a0, a2, g2, k_k, k_a, r_k, lnx_g, lnx_b, w_ff1, w_ff2, normf_g):
    B, S, D = x_prompt.shape
    DB, T, _ = x_sample.shape
    depth = w_in.shape[0]
    dt = x_prompt.dtype
    tabs_p = _rope_tables(jnp.arange(S))
    tabs_t = tuple(jnp.repeat(t, DB, axis=0) for t in _rope_tables(PAST_LEN + jnp.arange(T)))
    c_all = jnp.concatenate([c_prompt, c_sample], axis=0)
    hp = x_prompt
    hs = x_sample.reshape(1, DB * T, D)
    outs = [[] for _ in range(8)]
    tm_s = min(ROW_TILE, DB * T)
    for l in range(depth):
        last = l == depth - 1
        mod = _ada(c_all, w_ada[l], b_ada[l])
        mod_p = [m.reshape(B, 1, D) for m in jnp.split(mod[:B], 6, axis=-1)]
        mod_t = [m.reshape(1, DB, D) for m in jnp.split(mod[B:], 6, axis=-1)]
        w_in_b, wo_b = w_in[l].astype(BF16), w_out[l].astype(BF16)
        w1_b, w2_b = w_ff1[l].astype(BF16), w_ff2[l].astype(BF16)
        lw = dict(mu=mu[l], w0=w0[l], w2=w2[l], a0=a0[l], a2=a2[l], g2=g2[l], k_k=k_k[l], k_a=k_a[l],
                  r_k=r_k[l], lnx_g=lnx_g[l], lnx_b=lnx_b[l])

        sh1_p, sc1_p, gt1_p, sh2_p, sc2_p, gt2_p = mod_p
        _, _, gt1_s, sh2_s, sc2_s, gt2_s = mod_t
        q, k, v, P, kt, vt = _inproj(hp, sc1_p, sh1_p, norm1_g[l], w_in_b, tabs_p, ROW_TILE, "batch", True)
        hs_t = jnp.transpose(hs.reshape(DB, T, D), (1, 0, 2)).reshape(1, T * DB, D)
        q_t, k_t, v_t, P_t, kt_s, vt_s, pt_s = _inproj(hs_t, mod_t[1], mod_t[0], norm1_g[l], w_in_b, tabs_t, DB,
                                                       "tile", True, nq=T, p_t=True)
        by_request = lambda t: jnp.transpose(t.reshape(T, DB, ATT_W), (1, 0, 2))

        att = _attn_fused(q, k, v)
        sample_attn_inputs = (by_request(q_t), by_request(k_t), by_request(v_t),
                              jnp.transpose(cache_k[l], (0, 2, 3, 1)), jnp.transpose(cache_v[l], (0, 2, 3, 1)))
        rw, s_p, shift_p, att_s = _rwkv(P, jnp.zeros((B, RWKV_IN), F32),
                                        jnp.zeros((B, RWKV_HEADS, HEAD_DIM, HEAD_DIM), F32), lw, RWKV_CHUNK,
                                        sample_attn_inputs)
        hp = _mlp(att, rw, hp, gt1_p, sc2_p, sh2_p, gt2_p, norm2_g[l], normf_g, wo_b, w1_b, w2_b,
                  ROW_TILE, None, last)
        keep = min(max(w for w, _ in DIL_BRANCHES), S)
        win = lambda t: jnp.transpose(t.reshape(B, ATT_HEADS, HEAD_DIM, S)[..., S - keep:], (0, 3, 1, 2))
        outs[0].append(win(kt))
        outs[1].append(win(vt))
        outs[2].append(s_p)
        outs[3].append(shift_p)

        rw_t, s_t = _rwkv_scan(pt_s.reshape(T, RWKV_IN, DB), state_shift[l],
                               jnp.transpose(state_wkv[l], (1, 2, 3, 0)), lw)
        rw_s = jnp.transpose(rw_t, (2, 0, 1)).astype(BF16)
        hs = _mlp(att_s.reshape(1, DB * T, ATT_W), rw_s.reshape(1, DB * T, RWKV_W), hs, gt1_s, sc2_s, sh2_s, gt2_s,
                  norm2_g[l], normf_g, wo_b, w1_b, w2_b, tm_s, T, last)
        new = lambda t: jnp.transpose(t.reshape(T, ATT_HEADS, HEAD_DIM, DB), (3, 0, 1, 2))
        outs[4].append(new(kt_s))
        outs[5].append(new(vt_s))
        outs[6].append(jnp.transpose(s_t, (3, 0, 1, 2)))
        outs[7].append(P_t.reshape(T, DB, RWKV_IN)[T - 1])
    stack = lambda xs: jnp.stack(xs).astype(dt)
    return (hp.astype(dt), hs.reshape(DB, T, D).astype(x_sample.dtype), *[stack(o) for o in outs])
```

```python
import functools

import numpy as np
import jax
import jax.numpy as jnp
from jax import lax
from jax.experimental import pallas as pl
from jax.experimental.pallas import tpu as pltpu

F32 = jnp.float32
BF16 = jnp.bfloat16

HEAD_DIM = 64
ATT_HEADS = 8
ATT_W = ATT_HEADS * HEAD_DIM
RWKV_HEADS = 8
RWKV_W = RWKV_HEADS * HEAD_DIM
DIL_BRANCHES = ((128, 1), (512, 4), (2048, 16))
ROT_DIM = HEAD_DIM // 4
ROPE_THETA = 500000.0
DECAY_LORA = 64
AAA_LORA = 64
GATE_LORA = 128
RWKV_IN = 3 * RWKV_W + DECAY_LORA + AAA_LORA + GATE_LORA
NORM_EPS = 1e-6
LNX_EPS = 64e-5
PAST_LEN = 8192
ATT_BLOCK = 128
NEG_BIG = -1e30
LOG2_E = 1.4426950408889634
VMEM_LIMIT = 56 * 1024 * 1024
ROW_TILE = 512
RWKV_CHUNK = 64
RWKV_CHUNKS_PER_STEP = 4
ATT_GROUP = 8
ADA_TILE = 1024


def _dot(a, b):
    return jnp.dot(a, b, preferred_element_type=F32)


def _dot_nt(a, b):
    return lax.dot_general(a, b, (((1,), (1,)), ((), ())), preferred_element_type=F32)


def _dot_tn(a, b):
    return lax.dot_general(a, b, (((0,), (0,)), ((), ())), preferred_element_type=F32)


def _split2(x):
    hi = x.astype(BF16)
    lo = (x - hi.astype(F32)).astype(BF16)
    return hi, lo


def _head_sums(x, e):
    C, W = x.shape
    n = W // e.shape[0]
    xb = x.astype(BF16)
    o = _dot(jnp.concatenate([xb[:, e.shape[0] * p:e.shape[0] * (p + 1)] for p in range(n)], axis=0), e)
    return jnp.concatenate([o[C * p:C * (p + 1)] for p in range(n)], axis=1)


def _params(*sem):
    return pltpu.CompilerParams(dimension_semantics=sem, vmem_limit_bytes=VMEM_LIMIT)


def _ada_kernel(c_ref, w_ref, b_ref, o_ref):
    c = c_ref[...]
    s = c * jax.nn.sigmoid(c)
    o_ref[...] = _dot(s.astype(BF16), w_ref[...].astype(BF16)) + b_ref[...]


def _ada(c, w_ada, b_ada):
    n, d = c.shape
    cols = w_ada.shape[1]
    tn = ADA_TILE
    assert cols % tn == 0
    return pl.pallas_call(
        _ada_kernel,
        grid=(cols // tn,),
        in_specs=[pl.BlockSpec((n, d), lambda j: (0, 0)),
                  pl.BlockSpec((d, tn), lambda j: (0, j)),
                  pl.BlockSpec((1, tn), lambda j: (0, j))],
        out_specs=pl.BlockSpec((n, tn), lambda j: (0, j)),
        out_shape=jax.ShapeDtypeStruct((n, cols), F32),
        compiler_params=_params("arbitrary"),
        name="ada",
    )(c, w_ada, b_ada.reshape(1, cols))


def _rope(t, cos, s1, s2):
    parts = []
    for c in range(ATT_W // 128):
        tc = t[:, 128 * c:128 * (c + 1)]
        parts.append(tc * cos + pltpu.roll(tc, 128 - ROT_DIM // 2, 1) * s1 + pltpu.roll(tc, ROT_DIM // 2, 1) * s2)
    return jnp.concatenate(parts, axis=1)


def _inproj_kernel(x_ref, sc_ref, sh_ref, g_ref, w_ref, cos_ref, s1_ref, s2_ref, *out_refs,
                   transposed, p_t):
    q_ref, k_ref, v_ref, p_ref = out_refs[:4]
    rest = list(out_refs[4:])
    x = x_ref[...]
    ms = jnp.mean(x * x, axis=-1, keepdims=True)
    h = x * lax.rsqrt(ms + NORM_EPS) * g_ref[...]
    h = (h * (1.0 + sc_ref[...]) + sh_ref[...]).astype(BF16)
    cos, s1, s2 = cos_ref[...], s1_ref[...], s2_ref[...]
    q = _rope(_dot(h, w_ref[:, 0:ATT_W]), cos, s1, s2)
    q_ref[...] = (q * HEAD_DIM ** -0.5).astype(q_ref.dtype)
    k = _rope(_dot(h, w_ref[:, ATT_W:2 * ATT_W]), cos, s1, s2)
    v = _dot(h, w_ref[:, 2 * ATT_W:3 * ATT_W])
    k_ref[...] = k
    v_ref[...] = v
    p = _dot(h, w_ref[:, 3 * ATT_W:])
    p_ref[...] = p
    if transposed:
        rest.pop(0)[...] = k.T
        rest.pop(0)[...] = v.T
    if p_t:
        rest.pop(0)[...] = p.T


def _inproj(x, sc, sh, g, w_in_b, tabs, tm, mod_mode, transposed, nq=1, p_t=False):
    G, R, D = x.shape
    nin = w_in_b.shape[1]
    row_spec = lambda w: pl.BlockSpec((None, tm, w), lambda b, i: (b, i, 0))
    mod_spec = {"batch": pl.BlockSpec((None, 1, D), lambda b, i: (b, 0, 0)),
                "tile": pl.BlockSpec((None, tm, D), lambda b, i: (b, 0, 0))}[mod_mode]
    tab_spec = pl.BlockSpec((tm, 128), lambda b, i: (i, 0))
    const = lambda shape: pl.BlockSpec(shape, lambda b, i: (0, 0))
    out_specs = [row_spec(ATT_W)] * 3 + [row_spec(RWKV_IN)]
    out_shape = [jax.ShapeDtypeStruct((G, R, ATT_W), F32)] * 3 + [jax.ShapeDtypeStruct((G, R, RWKV_IN), F32)]
    if transposed:
        spq = (R // nq) // tm
        t_spec = pl.BlockSpec((None, None, ATT_W, tm), lambda b, i: (b, i // spq, 0, i % spq))
        out_specs += [t_spec] * 2
        out_shape += [jax.ShapeDtypeStruct((G, nq, ATT_W, R // nq), F32)] * 2
        if p_t:
            out_specs.append(pl.BlockSpec((None, None, RWKV_IN, tm), lambda b, i: (b, i // spq, 0, i % spq)))
            out_shape.append(jax.ShapeDtypeStruct((G, nq, RWKV_IN, R // nq), F32))
    return pl.pallas_call(
        functools.partial(_inproj_kernel, transposed=transposed, p_t=p_t),
        grid=(G, R // tm),
        in_specs=[row_spec(D), mod_spec, mod_spec, const((1, D)), const((D, nin)),
                  tab_spec, tab_spec, tab_spec],
        out_specs=out_specs,
        out_shape=out_shape,
        compiler_params=_params("arbitrary", "arbitrary"),
        name="inproj",
    )(x, sc, sh, g.reshape(1, D), w_in_b, *tabs)


def _rope_tables(pos):
    half = ROT_DIM // 2
    inv = ROPE_THETA ** (-jnp.arange(half, dtype=F32) * (2.0 / ROT_DIM))
    ang = pos.astype(F32)[:, None] * inv[None, :]
    cos, sin = jnp.cos(ang), jnp.sin(ang)
    n = pos.shape[0]
    ones = jnp.ones((n, HEAD_DIM - ROT_DIM), F32)
    zeros = jnp.zeros((n, HEAD_DIM - ROT_DIM), F32)
    zh = jnp.zeros((n, half), F32)
    ctab = jnp.concatenate([cos, cos, ones], axis=1)
    s1 = jnp.concatenate([-sin, zh, zeros], axis=1)
    s2 = jnp.concatenate([zh, sin, zeros], axis=1)
    return tuple(jnp.tile(t, (1, 128 // HEAD_DIM)) for t in (ctab, s1, s2))


def _attn_fused_kernel(q_ref, k_ref, v_ref, o_ref, acc_scr, m_scr, l_scr, *, dilations, group):
    nw, HD = ATT_BLOCK, HEAD_DIM
    S = q_ref.shape[0]
    n_tiles = S // nw
    lane = lax.broadcasted_iota(jnp.int32, (nw, 2 * HD), 1)
    first_head = lane < HD
    q_idx = lax.broadcasted_iota(jnp.int32, (nw, 2 * HD), 0)
    n_col = nw // HD
    key_idx = [HD * jc + (lane & (HD - 1)) for jc in range(n_col)]
    cur_ok = [q_idx >= key_idx[jc] for jc in range(n_col)]

    def bd(x):
        zero = jnp.zeros_like(x)
        xa, xb = jnp.where(first_head, x, zero), jnp.where(first_head, zero, x)
        parts = []
        for jc in range(n_col):
            parts += [xa[HD * jc:HD * (jc + 1)], xb[HD * jc:HD * (jc + 1)]]
        return jnp.concatenate(parts, axis=0)

    bd_row = lax.broadcasted_iota(jnp.int32, (2 * nw, 2 * HD), 0)
    bd_lane = lax.broadcasted_iota(jnp.int32, (2 * nw, 2 * HD), 1)
    ones_bd = jnp.where(((bd_row // HD) % 2 == 0) == (bd_lane < HD), 1.0, 0.0).astype(BF16)

    def with_ones(v):
        return jnp.concatenate([bd(v.astype(BF16)), ones_bd], axis=1)

    def half_reduce(x, op, fill):
        ra = op(jnp.where(first_head, x, fill), axis=-1, keepdims=True)
        rb = op(jnp.where(first_head, fill, x), axis=-1, keepdims=True)
        return jnp.where(first_head, ra, rb)

    def steps(tiles, use_prev, init, final):
        n = range(len(tiles))
        qs = [(q_ref[tiles[t][0], :] * LOG2_E).astype(BF16) for t in n]
        cols, vbd = [], []
        for t in n:
            rows, prev_rows, has_prev = tiles[t]
            s = _dot_nt(qs[t], bd(k_ref[rows, :].astype(BF16)))
            c = [jnp.where(cur_ok[jc], s[:, 2 * HD * jc:2 * HD * (jc + 1)], NEG_BIG) for jc in range(n_col)]
            vb = [with_ones(v_ref[rows, :])]
            if use_prev:
                sp = _dot_nt(qs[t], bd(k_ref[prev_rows, :].astype(BF16)))
                thresh = jnp.where(has_prev, 0, nw)
                c += [jnp.where((key_idx[jc] - q_idx) >= thresh, sp[:, 2 * HD * jc:2 * HD * (jc + 1)], NEG_BIG)
                      for jc in range(n_col)]
                vb.append(with_ones(v_ref[prev_rows, :]))
            cols.append(c)
            vbd.append(vb)
        m_new, m_old, p = [], [], []
        for t in n:
            m = half_reduce(functools.reduce(jnp.maximum, cols[t]), jnp.max, NEG_BIG)
            if not init:
                m_old.append(m_scr[tiles[t][0], :])
                m = jnp.maximum(m, m_old[t])
            m_new.append(m)
            p.append([jnp.exp2(c - m) for c in cols[t]])
        acc = []
        for t in n:
            a = _dot(jnp.concatenate(p[t][0:n_col], axis=1).astype(BF16), vbd[t][0])
            if use_prev:
                a = a + _dot(jnp.concatenate(p[t][n_col:], axis=1).astype(BF16), vbd[t][1])
            acc.append(a)
        for t in n:
            rows = tiles[t][0]
            a, l = acc[t][:, 0:2 * HD], acc[t][:, 2 * HD:]
            if not init:
                alpha = jnp.exp2(m_old[t] - m_new[t])
                a = a + acc_scr[rows, :] * alpha
                l = l + l_scr[rows, :] * alpha
            if final:
                o_ref[rows, :] = (a / l).astype(o_ref.dtype)
            else:
                acc_scr[rows, :] = a
                m_scr[rows, :] = m_new[t]
                l_scr[rows, :] = l

    n_br = len(dilations)
    for bi, dil in enumerate(dilations):
        init, final = bi == 0, bi == n_br - 1
        use_prev = S // (dil * nw) > 1

        def group_body(g, carry, dil=dil, use_prev=use_prev, init=init, final=final):
            tiles = []
            for i in range(group):
                t = g * group + i
                r, j = lax.rem(t, dil), lax.div(t, dil)
                start = r + j * (dil * nw)
                prev = r + jnp.maximum(j - 1, 0) * (dil * nw)
                if dil > 1:
                    tiles.append((pl.ds(start, nw, stride=dil), pl.ds(prev, nw, stride=dil), j > 0))
                else:
                    tiles.append((pl.ds(pl.multiple_of(start, nw), nw), pl.ds(pl.multiple_of(prev, nw), nw), j > 0))
            steps(tiles, use_prev, init, final)
            return carry

        lax.fori_loop(0, n_tiles // group, group_body, 0, unroll=True)


def _attn_fused(q, k, v):
    B, S, W = q.shape
    nw = ATT_BLOCK
    dilations = tuple(sorted((d for _, d in DIL_BRANCHES), reverse=True))
    for win, d in DIL_BRANCHES:
        assert win == d * nw and S % (d * nw) == 0
    assert (S // nw) % ATT_GROUP == 0
    PW = 2 * HEAD_DIM
    spec = pl.BlockSpec((None, S, PW), lambda b, p: (b, 0, p))
    return pl.pallas_call(
        functools.partial(_attn_fused_kernel, dilations=dilations, group=ATT_GROUP),
        grid=(B, W // PW),
        in_specs=[spec, spec, spec],
        out_specs=spec,
        out_shape=jax.ShapeDtypeStruct((B, S, W), BF16),
        scratch_shapes=[pltpu.VMEM((S, PW), F32) for _ in range(3)],
        compiler_params=_params("arbitrary", "arbitrary"),
        name="attn_fused",
    )(q, k, v)


def _sample_counts(T, WB):
    i = np.arange(T)
    c = np.arange(WB)
    delta = WB + i[None, :] - c[:, None]
    cnt_c = np.zeros((WB, T), np.float32)
    diff = i[None, :] - i[:, None]
    cnt_n = np.zeros((T, T), np.float32)
    for win, dil in DIL_BRANCHES:
        cnt_c += ((delta > 0) & (delta % dil == 0) & (delta <= win)).astype(np.float32)
        cnt_n += ((diff >= 0) & (diff % dil == 0) & (diff <= win)).astype(np.float32)
    return cnt_c, cnt_n


def _sample_attn_scores(q_ref, kn_ref, vn_ref, kt_ref, vt_ref, cc_ref, cn_ref, items=None):
    cc, cn = cc_ref[...], cn_ref[...]
    nb = q_ref.shape[0]
    if items is None:
        items = [(b, h) for b in range(nb) for h in range(ATT_HEADS)]
    sl = lambda h: slice(HEAD_DIM * h, HEAD_DIM * (h + 1))
    qh = {(b, h): q_ref[b, :, sl(h)] for b, h in items}
    s_c = {i: jnp.where(cc > 0, _dot(qh[i].astype(BF16), kt_ref[i[0], i[1]].astype(BF16)), NEG_BIG)
           for i in items}
    s_n = {i: jnp.where(cn > 0, _dot_nt(qh[i], kn_ref[i[0], :, sl(i[1])]), NEG_BIG) for i in items}
    return s_c, s_n


def _sample_attn_finish(scores, q_ref, kn_ref, vn_ref, kt_ref, vt_ref, cc_ref, cn_ref, o_ref):
    s_c, s_n = scores
    cc, cn = cc_ref[...], cn_ref[...]
    nb = q_ref.shape[0]
    items = list(s_c)
    sl = lambda h: slice(HEAD_DIM * h, HEAD_DIM * (h + 1))
    p_c, p_n, l = {}, {}, {}
    for i in items:
        m = jnp.maximum(jnp.max(s_c[i], axis=-1, keepdims=True), jnp.max(s_n[i], axis=-1, keepdims=True))
        p_c[i] = cc * jnp.exp(s_c[i] - m)
        p_n[i] = cn * jnp.exp(s_n[i] - m)
        l[i] = jnp.sum(p_c[i], axis=-1, keepdims=True) + jnp.sum(p_n[i], axis=-1, keepdims=True)
    acc = {i: _dot_nt(p_c[i].astype(BF16), vt_ref[i[0], i[1]].astype(BF16)) + _dot(p_n[i], vn_ref[i[0], :, sl(i[1])])
           for i in items}
    for b in range(nb):
        o_ref[b] = jnp.concatenate([acc[(b, h)] / l[(b, h)] for h in range(ATT_HEADS)], axis=1).astype(o_ref.dtype)


def _sample_attn_operands(q, k_new, v_new, cache_kt, cache_vt, n_steps, step_index):
    DB, T, W = q.shape
    WB = cache_kt.shape[-1]
    assert DB % n_steps == 0, "the requests must split evenly over the host call's grid steps"
    nb = DB // n_steps
    cnt_c, cnt_n = _sample_counts(T, WB)
    cc = jnp.asarray(cnt_c.T)
    cn = jnp.asarray(cnt_n.T)
    new_spec = pl.BlockSpec((nb, T, W), lambda *g: (step_index(*g), 0, 0))
    cache_spec = pl.BlockSpec((nb, ATT_HEADS, HEAD_DIM, WB), lambda *g: (step_index(*g), 0, 0, 0))
    const = lambda arr: pl.BlockSpec(arr.shape, lambda *g: (0, 0))
    args = (q, k_new, v_new, cache_kt, cache_vt, cc, cn)
    in_specs = [new_spec, new_spec, new_spec, cache_spec, cache_spec, const(cc), const(cn)]
    return args, in_specs, new_spec, jax.ShapeDtypeStruct((DB, T, W), BF16)


def _rwkv_prologue(p_ref, shift_ref, prev_scr, shout_ref, mu_ref, w0_ref, w2_ref, a0_ref, a2_ref, g2_ref,
                   kk_ref, ka_ref, rk_ref, e, chunk, between=()):
    C = p_ref.shape[0]
    W = RWKV_W
    PW = e.shape[0]
    row = lax.broadcasted_iota(jnp.int32, (C, 1), 0)

    def shifted(lo, hi):
        cur = p_ref[:, lo:hi]
        prev = jnp.where(row == 0, prev_scr[:, lo:hi], pltpu.roll(cur, 1, 0))
        return cur + (prev - cur) * mu_ref[:, lo:hi]

    lora_wa = shifted(3 * W, 3 * W + DECAY_LORA + AAA_LORA)
    gl = shifted(3 * W + DECAY_LORA + AAA_LORA, RWKV_IN)
    w_lora = _dot(jnp.tanh(lora_wa).astype(BF16), w2_ref[...])
    a_lora = _dot(lora_wa.astype(BF16), a2_ref[...])
    g = _dot(jax.nn.sigmoid(gl).astype(BF16), g2_ref[...])

    ti = lax.broadcasted_iota(jnp.int32, (C, C), 0)
    si = lax.broadcasted_iota(jnp.int32, (C, C), 1)
    tri_b = jnp.where((ti >= si) & (lax.div(ti, chunk) == lax.div(si, chunk)), 1.0, 0.0).astype(BF16)
    cols = {n: [] for n in ("v", "a_t", "r_t", "b_t", "k_t", "e_l", "bonus")}
    for p in range(W // PW):
        if p < len(between):
            between[p]()
        lo, hi = PW * p, PW * (p + 1)
        r, k, v = shifted(lo, hi), shifted(W + lo, W + hi), shifted(2 * W + lo, 2 * W + hi)
        neg = -(w0_ref[:, lo:hi] + w_lora[:, lo:hi])
        softplus = jnp.maximum(neg, 0.0) + jnp.log(1.0 + jnp.exp(-jnp.abs(neg)))
        logw = -jnp.exp(-softplus - 0.5)
        a = jax.nn.sigmoid(a0_ref[:, lo:hi] + a_lora[:, lo:hi])
        kkr = k * kk_ref[:, lo:hi]
        kmod = k * (1.0 + (a - 1.0) * ka_ref[:, lo:hi])
        lw_h, lw_l = _split2(logw)
        L = _dot(tri_b, lw_h) + _dot(tri_b, lw_l)
        nrm = jnp.sqrt(_dot((kkr * kkr).astype(BF16), e))
        bonus = _dot((r * kmod * rk_ref[:, lo:hi]).astype(BF16), e) * v
        kk = kkr / jnp.maximum(nrm, 1e-12)
        e_l = jnp.exp(L)
        e_nl = jnp.exp(-L)
        cols["v"].append(v)
        cols["a_t"].append(-kk * jnp.exp(L - logw))
        cols["r_t"].append(r * e_l)
        cols["b_t"].append(kk * a * e_nl)
        cols["k_t"].append(kmod * e_nl)
        cols["e_l"].append(e_l)
        cols["bonus"].append(bonus)
    last = p_ref[C - 1:C, :]
    prev_scr[...] = last
    shout_ref[...] = last
    cat = lambda n: jnp.concatenate(cols[n], axis=1)
    return cat("v"), cat("a_t"), cat("r_t"), cat("b_t"), cat("k_t"), cat("e_l"), cat("bonus"), g


def _rwkv_epilogue(o, bonus, g, e, lg_ref, lb_ref, rw_ref):
    mean = _head_sums(o, e) * (1.0 / HEAD_DIM)
    d = o - mean
    var = _head_sums(d * d, e) * (1.0 / HEAD_DIM)
    on = d * lax.rsqrt(var + LNX_EPS) * lg_ref[...] + lb_ref[...]
    rw_ref[...] = ((on + bonus) * g).astype(rw_ref.dtype)


def _rwkv_pair_kernel(*refs, n_double, n_side_in):
    (p_ref, shift_ref, s0_ref, mu_ref, w0_ref, w2_ref, a0_ref, a2_ref, g2_ref,
     kk_ref, ka_ref, rk_ref, lg_ref, lb_ref, e_ref) = refs[:15]
    side_in = refs[15:15 + n_side_in]
    rw_ref, sout_ref, shout_ref, side_out, s_scr, prev_scr = refs[15 + n_side_in:]

    C = HEAD_DIM
    n_sub = p_ref.shape[0] // C
    n_pairs = RWKV_HEADS // 2
    PW = 2 * HEAD_DIM

    @pl.when(pl.program_id(1) == 0)
    def _():
        s_scr[...] = s0_ref[...]
        prev_scr[...] = shift_ref[...]

    side_items = [(b, h) for b in range(side_in[0].shape[0]) for h in range(ATT_HEADS)]
    side_s_c, side_s_n = {}, {}

    def score_slice(k):
        def emit():
            part = _sample_attn_scores(*side_in, items=side_items[k::n_pairs])
            side_s_c.update(part[0])
            side_s_n.update(part[1])
        return emit

    e = e_ref[...]
    v, a_t, r_t, b_t, k_t, e_l, bonus, g = _rwkv_prologue(
        p_ref, shift_ref, prev_scr, shout_ref, mu_ref, w0_ref, w2_ref, a0_ref, a2_ref, g2_ref,
        kk_ref, ka_ref, rk_ref, e, C, between=[score_slice(k) for k in range(n_pairs)])
    _sample_attn_finish((side_s_c, side_s_n), *side_in, side_out)

    lane = lax.broadcasted_iota(jnp.int32, (C, PW), 1)
    ti = lax.broadcasted_iota(jnp.int32, (C, PW), 0)
    si = lane & (HEAD_DIM - 1)
    first_head = lane < HEAD_DIM
    tri_incl = ti >= si
    tri_strict = ti > si
    eye = jnp.where(ti == si, 1.0, 0.0)
    rows2 = lax.broadcasted_iota(jnp.int32, (2 * C, PW), 0)
    lanes2 = lax.broadcasted_iota(jnp.int32, (2 * C, PW), 1)
    same_head = (rows2 < C) == (lanes2 < HEAD_DIM)

    def bd(x):
        zero = jnp.zeros_like(x)
        return jnp.concatenate([jnp.where(first_head, x, zero), jnp.where(first_head, zero, x)], axis=0)

    def dot_bd(x, y):
        return _dot(x.astype(BF16), bd(y.astype(BF16)))

    items = [(c, p) for c in range(n_sub) for p in range(n_pairs)]
    tile = lambda t, c, p: t[C * c:C * (c + 1), PW * p:PW * (p + 1)]
    ar = {i: jnp.concatenate([tile(a_t, *i), tile(r_t, *i)], axis=0).astype(BF16) for i in items}
    bk = {i: jnp.concatenate([tile(b_t, *i), tile(k_t, *i)], axis=0).astype(BF16) for i in items}
    gm = {i: _dot_nt(ar[i], jnp.concatenate([bd(bk[i][0:C]), bd(bk[i][C:2 * C])], axis=0)) for i in items}
    m_ab = {i: jnp.where(tri_strict, gm[i][0:C, 0:PW], 0.0) for i in items}
    m_ak = {i: jnp.where(tri_strict, gm[i][0:C, PW:2 * PW], 0.0).astype(BF16) for i in items}
    n_r = {i: jnp.concatenate([jnp.where(tri_incl, gm[i][C:2 * C, 0:PW], 0.0),
                               jnp.where(tri_incl, gm[i][C:2 * C, PW:2 * PW], 0.0)], axis=1).astype(BF16)
           for i in items}
    t_inv = {i: eye + m_ab[i] for i in items}
    mp = m_ab
    for _ in range(n_double):
        mp = {i: dot_bd(mp[i], mp[i]) for i in items}
        t_inv = {i: t_inv[i] + dot_bd(t_inv[i], mp[i]) for i in items}
    v_b = {i: tile(v, *i).astype(BF16) for i in items}
    v_bd = {i: bd(v_b[i]) for i in items}
    mv = {i: _dot(m_ak[i], v_bd[i]) for i in items}
    state = [s_scr[p] for p in range(n_pairs)]
    o_rows = []
    for c in range(n_sub):
        its = [(c, p) for p in range(n_pairs)]
        ars = {i: _dot_nt(ar[i], state[i[1]].astype(BF16)) for i in its}
        u_b = {i: dot_bd(t_inv[i], ars[i][0:C] + mv[i]).astype(BF16) for i in its}
        o = [ars[i][C:2 * C] + _dot(n_r[i], jnp.concatenate([bd(u_b[i]), v_bd[i]], axis=0)) for i in its]
        o_rows.append(jnp.concatenate(o, axis=1))
        for i in its:
            upd = jnp.where(same_head, _dot_tn(jnp.concatenate([u_b[i], v_b[i]], axis=0), bk[i]), 0.0)
            state[i[1]] = (state[i[1]] + upd) * e_l[C * (c + 1) - 1:C * (c + 1), PW * i[1]:PW * (i[1] + 1)]
    for p in range(n_pairs):
        s_scr[p] = state[p]
        sout_ref[p] = state[p]
    _rwkv_epilogue(jnp.concatenate(o_rows, axis=0), bonus, g, e, lg_ref, lb_ref, rw_ref)


def _rwkv(P, shift0, s0, lw, C, sample_attn_inputs):
    B, T, _ = P.shape
    assert C == HEAD_DIM, "the pair-packed chunk kernel needs chunk length == head dim"
    rows = C * RWKV_CHUNKS_PER_STEP
    assert T % rows == 0
    n_double = max(int(np.log2(C)) - 1, 0)
    W = RWKV_W
    e = np.kron(np.eye(2, dtype=np.float32), np.ones((HEAD_DIM, HEAD_DIM), np.float32))
    zpad = jnp.zeros((DECAY_LORA, W), F32)
    w2p = jnp.concatenate([lw["w2"], jnp.zeros((AAA_LORA, W), F32)], axis=0).astype(BF16)
    a2p = jnp.concatenate([zpad, lw["a2"]], axis=0).astype(BF16)
    row = lambda t: t.reshape(1, -1)
    consts = [row(lw["mu"]), row(lw["w0"]), w2p, row(lw["a0"]), a2p, lw["g2"].astype(BF16),
              row(lw["k_k"]), row(lw["k_a"]), row(lw["r_k"]), row(lw["lnx_g"]), row(lw["lnx_b"]),
              jnp.asarray(e, BF16)]
    sp = s0.reshape(B, RWKV_HEADS // 2, 2, HEAD_DIM, HEAD_DIM)
    z = jnp.zeros_like(sp[:, :, 0])
    s0 = jnp.concatenate([jnp.concatenate([sp[:, :, 0], z], axis=-1),
                          jnp.concatenate([z, sp[:, :, 1]], axis=-1)], axis=-2)
    st_shape = s0.shape[1:]
    const = lambda arr: pl.BlockSpec(arr.shape, lambda b, c: (0, 0))
    st_spec = pl.BlockSpec((None,) + st_shape, lambda b, c: (b, 0, 0, 0))
    sh_spec = pl.BlockSpec((None, 1, RWKV_IN), lambda b, c: (b, 0, 0))
    n_c = T // rows
    side_args, side_specs, side_out_spec, side_out_shape = _sample_attn_operands(
        *sample_attn_inputs, B * n_c, lambda b, c: b * n_c + c)
    rw, s_out, sh_out, att_side = pl.pallas_call(
        functools.partial(_rwkv_pair_kernel, n_double=n_double, n_side_in=len(side_args)),
        grid=(B, n_c),
        in_specs=[pl.BlockSpec((None, rows, RWKV_IN), lambda b, c: (b, c, 0)), sh_spec, st_spec]
                 + [const(t) for t in consts] + side_specs,
        out_specs=[pl.BlockSpec((None, rows, W), lambda b, c: (b, c, 0)), st_spec, sh_spec, side_out_spec],
        out_shape=[jax.ShapeDtypeStruct((B, T, W), BF16),
                   jax.ShapeDtypeStruct((B,) + st_shape, F32),
                   jax.ShapeDtypeStruct((B, 1, RWKV_IN), F32), side_out_shape],
        scratch_shapes=[pltpu.VMEM(st_shape, F32),
                        pltpu.VMEM((1, RWKV_IN), F32)],
        compiler_params=_params("arbitrary", "arbitrary"),
        name=f"rwkv_c{C}",
    )(P, shift0.reshape(B, 1, RWKV_IN), s0, *consts, *side_args)
    s_out = jnp.stack([s_out[:, :, :HEAD_DIM, :HEAD_DIM], s_out[:, :, HEAD_DIM:, HEAD_DIM:]], axis=2)
    s_out = s_out.reshape(B, RWKV_HEADS, HEAD_DIM, HEAD_DIM)
    return rw, s_out, sh_out.reshape(B, RWKV_IN), att_side


def _rwkv_scan_kernel(r_ref, k_ref, v_ref, lo_ref, sr_ref, sk_ref, sv_ref, slo_ref, s_ref,
                      mur_ref, muk_ref, muv_ref, mulo_ref, w0_ref, w2t_ref, a0_ref, a2t_ref, g2t_ref,
                      kk_ref, ka_ref, rk_ref, lg_ref, lb_ref,
                      rw_ref, sout_ref,
                      w_scr, nkk_scr, bb_scr, km_scr, rr_scr, vv_scr, bonus_scr, g_scr, o_scr):
    T = r_ref.shape[0]

    def shifted(ref, s0_ref, mu_ref, t):
        cur = ref[t]
        prev = s0_ref[...] if t == 0 else ref[t - 1]
        return cur + (prev - cur) * mu_ref[...]

    for t in range(T):
        r = shifted(r_ref, sr_ref, mur_ref, t)
        k = shifted(k_ref, sk_ref, muk_ref, t)
        v = shifted(v_ref, sv_ref, muv_ref, t)
        lo = shifted(lo_ref, slo_ref, mulo_ref, t)
        wl, al, gl = lo[0:DECAY_LORA], lo[DECAY_LORA:DECAY_LORA + AAA_LORA], lo[DECAY_LORA + AAA_LORA:]
        neg = -(w0_ref[...] + _dot(w2t_ref[...], jnp.tanh(wl).astype(BF16)))
        softplus = jnp.maximum(neg, 0.0) + jnp.log(1.0 + jnp.exp(-jnp.abs(neg)))
        w = jnp.exp(-jnp.exp(-softplus - 0.5))
        a = jax.nn.sigmoid(a0_ref[...] + _dot(a2t_ref[...], al.astype(BF16)))
        g_scr[t] = _dot(g2t_ref[...], jax.nn.sigmoid(gl).astype(BF16))
        kkr = k * kk_ref[...]
        kk = kkr / jnp.maximum(jnp.sqrt(jnp.sum(kkr * kkr, axis=0, keepdims=True)), 1e-12)
        kmod = k * (1.0 + (a - 1.0) * ka_ref[...])
        bonus_scr[t] = jnp.sum(r * kmod * rk_ref[...], axis=0, keepdims=True) * v
        w_scr[t] = w
        nkk_scr[t] = -kk
        bb_scr[t] = kk * a
        km_scr[t] = kmod
        rr_scr[t] = r
        vv_scr[t] = v

    def row(i, carry):
        s = s_ref[i]
        for t in range(T):
            sa = jnp.sum(s * nkk_scr[t], axis=0, keepdims=True)
            s = s * w_scr[t] + sa * bb_scr[t] + vv_scr[t, pl.ds(i, 1), :] * km_scr[t]
            o_scr[t, pl.ds(i, 1), :] = jnp.sum(s * rr_scr[t], axis=0, keepdims=True)
        sout_ref[i] = s
        return carry

    lax.fori_loop(0, HEAD_DIM, row, 0, unroll=4)
    o = o_scr[...]
    mean = jnp.mean(o, axis=1, keepdims=True)
    d = o - mean
    var = jnp.mean(d * d, axis=1, keepdims=True)
    on = d * lax.rsqrt(var + LNX_EPS) * lg_ref[...] + lb_ref[...]
    rw_ref[...] = ((on + bonus_scr[...]) * g_scr[...]).astype(rw_ref.dtype)


def _rwkv_scan(pt, shift0, s0t, lw):
    T, _, DB = pt.shape
    W, HD = RWKV_W, HEAD_DIM
    lora_w = DECAY_LORA + AAA_LORA + GATE_LORA
    assert (3 * W) % lora_w == 0
    shift_t = shift0.T
    names = ("w0", "a0", "k_k", "k_a", "r_k", "lnx_g", "lnx_b")
    packed = jnp.concatenate([lw["mu"].reshape(-1)] + [lw[n].reshape(-1) for n in names])
    pc = jnp.broadcast_to(packed[:, None], (packed.size, DB))
    base = {n: (RWKV_IN + i * W) // HD for i, n in enumerate(names)}
    head = lambda base: (lambda h: (base + h, 0))
    pt_blk = lambda base: pl.BlockSpec((T, HD, DB), lambda h: (0, base + h, 0))
    blk2 = lambda base: pl.BlockSpec((HD, DB), head(base))
    lo_pt = pl.BlockSpec((T, lora_w, DB), lambda h: (0, 3 * W // lora_w, 0))
    lo_2d = pl.BlockSpec((lora_w, DB), lambda h: (3 * W // lora_w, 0))
    nh = W // HD
    st_spec = pl.BlockSpec((None, HD, HD, DB), lambda h: (h, 0, 0, 0))
    wt_spec = lambda n: pl.BlockSpec((HD, n), lambda h: (h, 0))
    scr = lambda: pltpu.VMEM((T, HD, DB), F32)
    return pl.pallas_call(
        _rwkv_scan_kernel,
        grid=(nh,),
        in_specs=[pt_blk(0), pt_blk(nh), pt_blk(2 * nh), lo_pt,
                  blk2(0), blk2(nh), blk2(2 * nh), lo_2d, st_spec,
                  blk2(0), blk2(nh), blk2(2 * nh), lo_2d,
                  blk2(base["w0"]), wt_spec(DECAY_LORA), blk2(base["a0"]), wt_spec(AAA_LORA), wt_spec(GATE_LORA),
                  blk2(base["k_k"]), blk2(base["k_a"]), blk2(base["r_k"]), blk2(base["lnx_g"]), blk2(base["lnx_b"])],
        out_specs=[pl.BlockSpec((T, HD, DB), lambda h: (0, h, 0)), st_spec],
        out_shape=[jax.ShapeDtypeStruct((T, W, DB), F32), jax.ShapeDtypeStruct(s0t.shape, F32)],
        scratch_shapes=[scr() for _ in range(9)],
        compiler_params=_params("arbitrary"),
        name="rwkv_scan",
    )(pt, pt, pt, pt, shift_t, shift_t, shift_t, shift_t, s0t,
      pc, pc, pc, pc, pc, lw["w2"].T.astype(BF16), pc, lw["a2"].T.astype(BF16),
      lw["g2"].T.astype(BF16), pc, pc, pc, pc, pc)


def _mlp_kernel(att_ref, rw_ref, x_ref, gt1_ref, sc2_ref, sh2_ref, gt2_ref, g2_ref, gf_ref,
                wo_ref, w1_ref, w2_ref, y_ref, *, final_norm, ff_chunk, rep):
    def mod(ref):
        m = ref[...]
        if rep is None:
            return m
        return jnp.broadcast_to(m[:, None, :], (m.shape[0], rep, m.shape[1])).reshape(m.shape[0] * rep, m.shape[1])

    half = wo_ref.shape[0] // 2
    mix = _dot(att_ref[...], wo_ref[0:half, :]) + _dot(rw_ref[...], wo_ref[half:, :])
    x1 = x_ref[...] + mod(gt1_ref) * mix
    ms = jnp.mean(x1 * x1, axis=-1, keepdims=True)
    h2 = x1 * lax.rsqrt(ms + NORM_EPS) * g2_ref[...]
    h2 = (h2 * (1.0 + mod(sc2_ref)) + mod(sh2_ref)).astype(BF16)
    acc = jnp.zeros_like(x1)
    for c in range(w1_ref.shape[1] // ff_chunk):
        hid = _dot(h2, w1_ref[:, c * ff_chunk:(c + 1) * ff_chunk])
        hid = jnp.square(jnp.maximum(hid, 0.0)).astype(BF16)
        acc = acc + _dot(hid, w2_ref[c * ff_chunk:(c + 1) * ff_chunk, :])
    x2 = x1 + mod(gt2_ref) * acc
    if final_norm:
        ms2 = jnp.mean(x2 * x2, axis=-1, keepdims=True)
        x2 = x2 * lax.rsqrt(ms2 + NORM_EPS) * gf_ref[...]
    y_ref[...] = x2


def _mlp(att, rw, x, gt1, sc2, sh2, gt2, g2, gf, wo_b, w1_b, w2_b, tm, rep, final_norm):
    G, R, D = x.shape
    dff = w1_b.shape[1]
    row_spec = lambda w: pl.BlockSpec((None, tm, w), lambda b, i: (b, i, 0))
    if rep is not None:
        mod_spec = pl.BlockSpec((None, tm // rep, D), lambda b, i: (b, i, 0))
    else:
        mod_spec = pl.BlockSpec((None, 1, D), lambda b, i: (b, 0, 0))
    const = lambda shape: pl.BlockSpec(shape, lambda b, i: (0, 0), pipeline_mode=pl.Buffered(1))
    return pl.pallas_call(
        functools.partial(_mlp_kernel, final_norm=final_norm, ff_chunk=1024, rep=rep),
        grid=(G, R // tm),
        in_specs=[row_spec(att.shape[-1]), row_spec(rw.shape[-1]), row_spec(D),
                  mod_spec, mod_spec, mod_spec, mod_spec, const((1, D)), const((1, D)),
                  const(wo_b.shape), const((D, dff)), const((dff, D))],
        out_specs=row_spec(D),
        out_shape=jax.ShapeDtypeStruct((G, R, D), F32),
        compiler_params=_params("arbitrary", "arbitrary"),
        name="mlp",
    )(att, rw, x, gt1, sc2, sh2, gt2, g2.reshape(1, D), gf.reshape(1, D), wo_b, w1_b, w2_b)


def kernel(x_prompt, x_sample, cache_k, cache_v, state_wkv, state_shift, c_prompt, c_sample, norm1_g, norm2_g, w_ada, b_ada, w_in, w_out, mu, w0, w2, a0, a2, g2, k_k, k_a, r_k, lnx_g, lnx_b, w_ff1, w_ff2, normf_g):
    B, S, D = x_prompt.shape
    DB, T, _ = x_sample.shape
    depth = w_in.shape[0]
    dt = x_prompt.dtype
    tabs_p = _rope_tables(jnp.arange(S))
    tabs_t = tuple(jnp.repeat(t, DB, axis=0) for t in _rope_tables(PAST_LEN + jnp.arange(T)))
    c_all = jnp.concatenate([c_prompt, c_sample], axis=0)
    hp = x_prompt
    hs = x_sample.reshape(1, DB * T, D)
    outs = [[] for _ in range(8)]
    tm_s = min(ROW_TILE, DB * T)
    for l in range(depth):
        last = l == depth - 1
        mod = _ada(c_all, w_ada[l], b_ada[l])
        mod_p = [m.reshape(B, 1, D) for m in jnp.split(mod[:B], 6, axis=-1)]
        mod_t = [m.reshape(1, DB, D) for m in jnp.split(mod[B:], 6, axis=-1)]
        w_in_b, wo_b = w_in[l].astype(BF16), w_out[l].astype(BF16)
        w1_b, w2_b = w_ff1[l].astype(BF16), w_ff2[l].astype(BF16)
        lw = dict(mu=mu[l], w0=w0[l], w2=w2[l], a0=a0[l], a2=a2[l], g2=g2[l], k_k=k_k[l], k_a=k_a[l],
                  r_k=r_k[l], lnx_g=lnx_g[l], lnx_b=lnx_b[l])

        sh1_p, sc1_p, gt1_p, sh2_p, sc2_p, gt2_p = mod_p
        _, _, gt1_s, sh2_s, sc2_s, gt2_s = mod_t
        q, k, v, P, kt, vt = _inproj(hp, sc1_p, sh1_p, norm1_g[l], w_in_b, tabs_p, ROW_TILE, "batch", True)
        hs_t = jnp.transpose(hs.reshape(DB, T, D), (1, 0, 2)).reshape(1, T * DB, D)
        q_t, k_t, v_t, P_t, kt_s, vt_s, pt_s = _inproj(hs_t, mod_t[1], mod_t[0], norm1_g[l], w_in_b, tabs_t, DB,
                                                       "tile", True, nq=T, p_t=True)
        by_request = lambda t: jnp.transpose(t.reshape(T, DB, ATT_W), (1, 0, 2))

        att = _attn_fused(q, k, v)
        sample_attn_inputs = (by_request(q_t), by_request(k_t), by_request(v_t),
                              jnp.transpose(cache_k[l], (0, 2, 3, 1)), jnp.transpose(cache_v[l], (0, 2, 3, 1)))
        rw, s_p, shift_p, att_s = _rwkv(P, jnp.zeros((B, RWKV_IN), F32),
                                        jnp.zeros((B, RWKV_HEADS, HEAD_DIM, HEAD_DIM), F32), lw, RWKV_CHUNK,
                                        sample_attn_inputs)
        hp = _mlp(att, rw, hp, gt1_p, sc2_p, sh2_p, gt2_p, norm2_g[l], normf_g, wo_b, w1_b, w2_b,
                  ROW_TILE, None, last)
        keep = min(max(w for w, _ in DIL_BRANCHES), S)
        win = lambda t: jnp.transpose(t.reshape(B, ATT_HEADS, HEAD_DIM, S)[..., S - keep:], (0, 3, 1, 2))
        outs[0].append(win(kt))
        outs[1].append(win(vt))
        outs[2].append(s_p)
        outs[3].append(shift_p)

        rw_t, s_t = _rwkv_scan(pt_s.reshape(T, RWKV_IN, DB), state_shift[l],
                               jnp.transpose(state_wkv[l], (1, 2, 3, 0)), lw)
        rw_s = jnp.transpose(rw_t, (2, 0, 1)).astype(BF16)
        hs = _mlp(att_s.reshape(1, DB * T, ATT_W), rw_s.reshape(1, DB * T, RWKV_W), hs, gt1_s, sc2_s, sh2_s, gt2_s,
                  norm2_g[l], normf_g, wo_b, w1_b, w2_b, tm_s, T, last)
        new = lambda t: jnp.transpose(t.reshape(T, ATT_HEADS, HEAD_DIM, DB), (3, 0, 1, 2))
        outs[4].append(new(kt_s))
        outs[5].append(new(vt_s))
        outs[6].append(jnp.transpose(s_t, (3, 0, 1, 2)))
        outs[7].append(P_t.reshape(T, DB, RWKV_IN)[T - 1])
    stack = lambda xs: jnp.stack(xs).astype(dt)
    return (hp.astype(dt), hs.reshape(DB, T, D).astype(x_sample.dtype), *[stack(o) for o in outs])
```
